```python
import math
import jax, jax.numpy as jnp
from jax import lax
import numpy as np

D_MODEL = 1024
BATCH = 8
SEQ = 2048
DEPTH = 4
DEC_BATCH = 128
DEC_SEQ = 1
PAST_LEN = 16384
PAGE_SIZE = 128

POOL_WIDTH = D_MODEL // 2
LRU_WIDTH = D_MODEL - POOL_WIDTH
MIX_WIDTH = POOL_WIDTH + LRU_WIDTH
IN_WIDTH = POOL_WIDTH + 2 * LRU_WIDTH
POOL_WINDOWS = (2, 4, 8, 16)
N_POOL_GROUPS = len(POOL_WINDOWS)
POOL_GROUP_DIM = POOL_WIDTH // N_POOL_GROUPS
POOL_BUF = max(POOL_WINDOWS) - 1
N_LRU_HEADS = 8
LRU_HEAD_DIM = LRU_WIDTH // N_LRU_HEADS
CONV_WIDTH = 4
LRU_C = 8.0
D_FF = 128 * int(round(8 * D_MODEL / 3 / 128))
N_MOD = 9
ALPHA = (2 * DEPTH) ** 0.25
BETA = (8 * DEPTH) ** -0.25
LN_EPS = 1e-5

kernel_name = "hybrid_pool_rglru_macaron_deepnorm_step"


def layer_norm(x, g, b):
    xf = x.astype(jnp.float32)
    mu = jnp.mean(xf, -1, keepdims=True)
    var = jnp.mean(jnp.square(xf - mu), -1, keepdims=True)
    y = (xf - mu) * lax.rsqrt(var + LN_EPS)
    return (y * g.astype(jnp.float32) + b.astype(jnp.float32)).astype(x.dtype)


def modulate(x, shift, scale):
    return x * (1.0 + scale[:, None, :]) + shift[:, None, :]


def swiglu(h, w_gu, w_down):
    gu = jnp.einsum('bld,df->blf', h, w_gu)
    g, u = jnp.split(gu, 2, axis=-1)
    return jnp.einsum('blf,fd->bld', jax.nn.silu(g) * u, w_down)


def pool_mixer(u, buf, pos0, w_pool, pool_scale):
    B, L, _ = u.shape
    z = jnp.concatenate([buf.astype(u.dtype), u], axis=1)
    cs = jnp.cumsum(z.astype(jnp.float32), axis=1)
    cs = jnp.pad(cs, ((0, 0), (1, 0), (0, 0)))
    end = cs[:, POOL_BUF + 1:]
    pos = pos0 + jnp.arange(L)
    means = []
    for g, w in enumerate(POOL_WINDOWS):
        sl = slice(g * POOL_GROUP_DIM, (g + 1) * POOL_GROUP_DIM)
        start = cs[:, POOL_BUF + 1 - w: POOL_BUF + 1 - w + L, sl]
        cnt = jnp.minimum(pos + 1, w).astype(jnp.float32)[None, :, None]
        means.append((end[..., sl] - start) / cnt)
    pooled = jnp.concatenate(means, axis=-1).astype(u.dtype) - u
    pooled = pooled.reshape(B, L, N_POOL_GROUPS, POOL_GROUP_DIM)
    y = jnp.einsum('blgc,gcd->blgd', pooled, w_pool).reshape(B, L, POOL_WIDTH)
    return y * pool_scale, z[:, -POOL_BUF:]


def causal_conv(v, buf, conv_w, conv_b):
    L = v.shape[1]
    z = jnp.concatenate([buf.astype(v.dtype), v], axis=1)
    y = conv_b + sum(z[:, k:k + L] * conv_w[k] for k in range(CONV_WIDTH))
    return y, z[:, -(CONV_WIDTH - 1):]


def rglru(xc, h0, w_a, b_a, w_x, b_x, lam):
    B, L, _ = xc.shape
    xh = xc.reshape(B, L, N_LRU_HEADS, LRU_HEAD_DIM)
    r = jax.nn.sigmoid(jnp.einsum('blhi,hij->blhj', xh, w_a).reshape(B, L, LRU_WIDTH) + b_a)
    i = jax.nn.sigmoid(jnp.einsum('blhi,hij->blhj', xh, w_x).reshape(B, L, LRU_WIDTH) + b_x)
    log_a = -LRU_C * r.astype(jnp.float32) * jax.nn.softplus(-lam.astype(jnp.float32))
    a = jnp.exp(log_a)
    mult = jnp.sqrt(-jnp.expm1(2.0 * log_a))
    bx = mult * (i * xc).astype(jnp.float32)

    def step(h, inp):
        a_t, b_t = inp
        h = a_t * h + b_t
        return h, h

    hT, hs = lax.scan(step, h0.astype(jnp.float32), (jnp.swapaxes(a, 0, 1), jnp.swapaxes(bx, 0, 1)))
    return jnp.swapaxes(hs, 0, 1).astype(xc.dtype), hT.astype(h0.dtype)


def layer(x, c, st_pool, st_conv, st_h, pos0,
          w_ada, b_ada, ffn1_wgu, ffn1_wdown, w_in, w_out, w_pool, pool_scale,
          conv_w, conv_b, w_rg_a, b_rg_a, w_rg_x, b_rg_x, lru_lambda,
          ffn2_wgu, ffn2_wdown, ln_g, ln_b):
    B = x.shape[0]
    mod = (jax.nn.silu(c) @ w_ada + b_ada).reshape(B, N_MOD, D_MODEL)
    s1, sc1, g1, s2, sc2, g2, s3, sc3, g3 = [mod[:, k] for k in range(N_MOD)]
    h = modulate(x, s1, sc1)
    x = layer_norm(ALPHA * x + 0.5 * (1.0 + g1)[:, None] * swiglu(h, ffn1_wgu, ffn1_wdown), ln_g[0], ln_b[0])
    h = modulate(x, s2, sc2)
    proj = jnp.einsum('bld,de->ble', h, w_in)
    u_pool = proj[..., :POOL_WIDTH]
    u_lru = proj[..., POOL_WIDTH:POOL_WIDTH + LRU_WIDTH]
    u_gate = proj[..., POOL_WIDTH + LRU_WIDTH:]
    y_pool, new_pool = pool_mixer(u_pool, st_pool, pos0, w_pool, pool_scale)
    xc, new_conv = causal_conv(u_lru, st_conv, conv_w, conv_b)
    y_lru, new_h = rglru(xc, st_h, w_rg_a, b_rg_a, w_rg_x, b_rg_x, lru_lambda)
    y_lru = y_lru * jax.nn.gelu(u_gate)
    mix = jnp.einsum('ble,ed->bld', jnp.concatenate([y_pool, y_lru], axis=-1), w_out)
    x = layer_norm(ALPHA * x + (1.0 + g2)[:, None] * mix, ln_g[1], ln_b[1])
    h = modulate(x, s3, sc3)
    x = layer_norm(ALPHA * x + 0.5 * (1.0 + g3)[:, None] * swiglu(h, ffn2_wgu, ffn2_wdown), ln_g[2], ln_b[2])
    return x, new_pool, new_conv, new_h


def setup_inputs(seed: int = 0) -> dict:
    key = jax.random.key(seed)
    ks = jax.random.split(key, 32)
    f32 = jnp.float32
    n = lambda k, s, sc: jax.random.normal(k, s, f32) * sc
    u = jax.random.uniform(ks[20], (DEPTH, LRU_WIDTH), f32, 0.9, 0.999)
    s = u ** (1.0 / LRU_C)
    lam = jnp.log(s) - jnp.log1p(-s)
    return {
        "x_prompt": n(ks[0], (BATCH, SEQ, D_MODEL), 1.0),
        "x_sample": n(ks[1], (DEC_BATCH, DEC_SEQ, D_MODEL), 1.0),
        "state_pool": n(ks[2], (DEPTH, DEC_BATCH, POOL_BUF, POOL_WIDTH), 1.0),
        "state_conv": n(ks[3], (DEPTH, DEC_BATCH, CONV_WIDTH - 1, LRU_WIDTH), 1.0),
        "state_rglru": n(ks[4], (DEPTH, DEC_BATCH, LRU_WIDTH), 0.5),
        "c_prompt": n(ks[5], (BATCH, D_MODEL), 1.0),
        "c_sample": n(ks[6], (DEC_BATCH, D_MODEL), 1.0),
        "w_ada": n(ks[7], (DEPTH, D_MODEL, N_MOD * D_MODEL), 0.1 * D_MODEL ** -0.5),
        "b_ada": n(ks[8], (DEPTH, N_MOD * D_MODEL), 0.01),
        "ffn1_wgu": n(ks[9], (DEPTH, D_MODEL, 2 * D_FF), D_MODEL ** -0.5),
        "ffn1_wdown": n(ks[10], (DEPTH, D_FF, D_MODEL), BETA * D_FF ** -0.5),
        "w_in": n(ks[11], (DEPTH, D_MODEL, IN_WIDTH), D_MODEL ** -0.5),
        "w_out": n(ks[12], (DEPTH, MIX_WIDTH, D_MODEL), BETA * MIX_WIDTH ** -0.5),
        "w_pool": n(ks[13], (DEPTH, N_POOL_GROUPS, POOL_GROUP_DIM, POOL_GROUP_DIM), POOL_GROUP_DIM ** -0.5),
        "pool_scale": 1.0 + n(ks[14], (DEPTH, POOL_WIDTH), 0.1),
        "conv_w": n(ks[15], (DEPTH, CONV_WIDTH, LRU_WIDTH), CONV_WIDTH ** -0.5),
        "conv_b": n(ks[16], (DEPTH, LRU_WIDTH), 0.01),
        "w_rg_a": n(ks[17], (DEPTH, N_LRU_HEADS, LRU_HEAD_DIM, LRU_HEAD_DIM), LRU_HEAD_DIM ** -0.5),
        "b_rg_a": n(ks[18], (DEPTH, LRU_WIDTH), 0.01),
        "w_rg_x": n(ks[19], (DEPTH, N_LRU_HEADS, LRU_HEAD_DIM, LRU_HEAD_DIM), LRU_HEAD_DIM ** -0.5),
        "b_rg_x": n(ks[21], (DEPTH, LRU_WIDTH), 0.01),
        "lru_lambda": lam,
        "ffn2_wgu": n(ks[22], (DEPTH, D_MODEL, 2 * D_FF), D_MODEL ** -0.5),
        "ffn2_wdown": n(ks[23], (DEPTH, D_FF, D_MODEL), BETA * D_FF ** -0.5),
        "ln_g": 1.0 + n(ks[24], (DEPTH, 3, D_MODEL), 0.02),
        "ln_b": n(ks[25], (DEPTH, 3, D_MODEL), 0.02),
    }


def reference(x_prompt, x_sample, state_pool, state_conv, state_rglru, c_prompt, c_sample,
              w_ada, b_ada, ffn1_wgu, ffn1_wdown, w_in, w_out, w_pool, pool_scale,
              conv_w, conv_b, w_rg_a, b_rg_a, w_rg_x, b_rg_x, lru_lambda,
              ffn2_wgu, ffn2_wdown, ln_g, ln_b):
    bp = x_prompt.shape[0]
    dt = x_prompt.dtype
    zero_pool = jnp.zeros((bp, POOL_BUF, POOL_WIDTH), dt)
    zero_conv = jnp.zeros((bp, CONV_WIDTH - 1, LRU_WIDTH), dt)
    zero_h = jnp.zeros((bp, LRU_WIDTH), dt)
    xp, xs = x_prompt, x_sample
    pool_p, conv_p, h_p, pool_s, conv_s, h_s = [], [], [], [], [], []
    for l in range(DEPTH):
        wl = (w_ada[l], b_ada[l], ffn1_wgu[l], ffn1_wdown[l], w_in[l], w_out[l], w_pool[l], pool_scale[l],
              conv_w[l], conv_b[l], w_rg_a[l], b_rg_a[l], w_rg_x[l], b_rg_x[l], lru_lambda[l],
              ffn2_wgu[l], ffn2_wdown[l], ln_g[l], ln_b[l])
        xp, npool, nconv, nh = layer(xp, c_prompt, zero_pool, zero_conv, zero_h, 0, *wl)
        pool_p.append(npool); conv_p.append(nconv); h_p.append(nh)
        xs, npool, nconv, nh = layer(xs, c_sample, state_pool[l], state_conv[l], state_rglru[l], PAST_LEN, *wl)
        pool_s.append(npool); conv_s.append(nconv); h_s.append(nh)
    return (xp, xs,
            jnp.stack(pool_p), jnp.stack(conv_p), jnp.stack(h_p),
            jnp.stack(pool_s), jnp.stack(conv_s), jnp.stack(h_s))
```

```python
import functools
import math

import jax
import jax.numpy as jnp
from jax import lax
from jax.experimental import pallas as pl
from jax.experimental.pallas import tpu as pltpu

D_MODEL = 1024
DEPTH = 4
PAST_LEN = 16384
POOL_WIDTH = D_MODEL // 2
LRU_WIDTH = D_MODEL - POOL_WIDTH
IN_WIDTH = POOL_WIDTH + 2 * LRU_WIDTH
POOL_WINDOWS = (2, 4, 8, 16)
POOL_GROUP_DIM = POOL_WIDTH // len(POOL_WINDOWS)
POOL_BUF = max(POOL_WINDOWS) - 1
N_LRU_HEADS = 8
LRU_HEAD_DIM = LRU_WIDTH // N_LRU_HEADS
CONV_WIDTH = 4
LRU_C = 8.0
D_FF = 128 * int(round(8 * D_MODEL / 3 / 128))
N_MOD = 9
ALPHA = (2 * DEPTH) ** 0.25
LN_EPS = 1e-5

F32 = jnp.float32
BF16 = jnp.bfloat16

SUBLANES = 8
LANES = 128
MIB = 1024 * 1024

FFN_CHUNK = D_FF // 3
FFN_N_CHUNKS = D_FF // FFN_CHUNK
POOL_HIST = 16
CONV_HIST = SUBLANES


def _layer_norm(y, g, b):
    mu = jnp.mean(y, axis=-1, keepdims=True)
    d = y - mu
    var = jnp.mean(d * d, axis=-1, keepdims=True)
    return d * lax.rsqrt(var + LN_EPS) * g + b


def _silu(x):
    return x * jax.nn.sigmoid(x)


def _softplus(x):
    return jnp.maximum(x, 0.0) + jnp.log1p(jnp.exp(-jnp.abs(x)))


def _mod_rows(sh_ref, sc_ref, g_ref, per_row, tiles_per_batch):
    if per_row:
        return sh_ref[...], sc_ref[...], g_ref[...]
    b = pl.program_id(0) // tiles_per_batch
    return sh_ref[pl.ds(b, 1), :], sc_ref[pl.ds(b, 1), :], g_ref[pl.ds(b, 1), :]


def _ada_kernel(c_ref, w_ref, b_ref, o_ref):
    sc = _silu(c_ref[...]).astype(BF16)
    w = w_ref[...].astype(BF16)
    o_ref[...] = jnp.dot(sc, w, preferred_element_type=F32) + b_ref[...]


def _ada_call(c_all, w_ada, b_ada):
    n_rows = c_all.shape[0]
    n_out = w_ada.shape[-1]
    tn = n_out // 8
    return pl.pallas_call(
        _ada_kernel,
        out_shape=jax.ShapeDtypeStruct((DEPTH, n_rows, n_out), F32),
        grid=(DEPTH, n_out // tn),
        in_specs=[
            pl.BlockSpec((n_rows, D_MODEL), lambda l, j: (0, 0)),
            pl.BlockSpec((None, D_MODEL, tn), lambda l, j: (l, 0, j)),
            pl.BlockSpec((None, 1, tn), lambda l, j: (l, 0, j)),
        ],
        out_specs=pl.BlockSpec((None, n_rows, tn), lambda l, j: (l, 0, j)),
        compiler_params=pltpu.CompilerParams(
            dimension_semantics=("arbitrary", "arbitrary"),
            vmem_limit_bytes=32 * MIB),
        name="ada",
    )(c_all, w_ada, b_ada.reshape(DEPTH, 1, n_out))


def _ffn_kernel(x_ref, sh_ref, sc_ref, g_ref, wgu_ref, wd_ref, lng_ref, lnb_ref, o_ref,
                *, per_row, tiles_per_batch, ln_idx):
    x = x_ref[...]
    sh, sc, g = _mod_rows(sh_ref, sc_ref, g_ref, per_row, tiles_per_batch)
    h = (x * (1.0 + sc) + sh).astype(BF16)
    acc = None
    for c in range(FFN_N_CHUNKS):
        gu = jnp.dot(h, wgu_ref[:, 2 * c * FFN_CHUNK:2 * (c + 1) * FFN_CHUNK], preferred_element_type=F32)
        act = (_silu(gu[:, :FFN_CHUNK]) * gu[:, FFN_CHUNK:]).astype(BF16)
        part = jnp.dot(act, wd_ref[c * FFN_CHUNK:(c + 1) * FFN_CHUNK, :], preferred_element_type=F32)
        acc = part if acc is None else acc + part
    y = ALPHA * x + (0.5 * (1.0 + g)) * acc
    o_ref[...] = _layer_norm(y, lng_ref[ln_idx:ln_idx + 1, :], lnb_ref[ln_idx:ln_idx + 1, :])


def _mod_specs(layer, sub, per_row, n_sample, n_prompt):
    if per_row:
        return [pl.BlockSpec((None, n_sample, D_MODEL), functools.partial(lambda i, k: (layer, 0, k), k=3 * sub + t))
                for t in range(3)]
    prompt_block = n_sample // n_prompt
    return [pl.BlockSpec((None, n_prompt, D_MODEL), functools.partial(lambda i, k: (layer, prompt_block, k), k=3 * sub + t))
            for t in range(3)]


def _resident(shape, layer):
    nd = len(shape)
    return pl.BlockSpec((None,) + tuple(shape[1:]), lambda i: (layer,) + (0,) * (nd - 1),
                        pipeline_mode=pl.Buffered(1))


def _ffn_call(x2d, mod, wgu, wd, ln_g, ln_b, *, layer, sub, ln_idx, per_row, tm, seq, n_sample, n_prompt):
    rows = x2d.shape[0]
    kern = functools.partial(_ffn_kernel, per_row=per_row, tiles_per_batch=max(seq // tm, 1), ln_idx=ln_idx)
    return pl.pallas_call(
        kern,
        out_shape=jax.ShapeDtypeStruct((rows, D_MODEL), F32),
        grid=(rows // tm,),
        in_specs=[pl.BlockSpec((tm, D_MODEL), lambda i: (i, 0))]
        + _mod_specs(layer, sub, per_row, n_sample, n_prompt)
        + [_resident(wgu.shape, layer), _resident(wd.shape, layer),
           _resident(ln_g.shape, layer), _resident(ln_b.shape, layer)],
        out_specs=pl.BlockSpec((tm, D_MODEL), lambda i: (i, 0)),
        compiler_params=pltpu.CompilerParams(
            dimension_semantics=("arbitrary",),
            vmem_limit_bytes=52 * MIB),
        name=f"ffn{sub}_{'s' if per_row else 'p'}",
    )(x2d, mod, mod, mod, wgu, wd, ln_g, ln_b)


def _gates(xc, wg_ref, ba_ref, bx_ref, lam_ref):
    gates = jnp.dot(xc.astype(BF16), wg_ref[...], preferred_element_type=F32)
    r = jax.nn.sigmoid(gates[:, :LRU_WIDTH] + ba_ref[...])
    i = jax.nn.sigmoid(gates[:, LRU_WIDTH:] + bx_ref[...])
    log_a = (-LRU_C * r) * _softplus(-lam_ref[...])
    a = jnp.exp(log_a)
    t = jnp.tanh(log_a)
    mult = jnp.sqrt((-2.0 * t) / (1.0 - t))
    return a, mult * (i * xc)


def _mix_out(x, g, y_pool, y_lru, wout_ref, lng_ref, lnb_ref):
    mix = jnp.dot(y_pool.astype(BF16), wout_ref[:POOL_WIDTH, :], preferred_element_type=F32)
    mix = mix + jnp.dot(y_lru.astype(BF16), wout_ref[POOL_WIDTH:, :], preferred_element_type=F32)
    y = ALPHA * x + (1.0 + g) * mix
    return _layer_norm(y, lng_ref[1:2, :], lnb_ref[1:2, :])


def _mixer_prompt_kernel(x_ref, sh_ref, sc_ref, g_ref, win_ref, wpool_ref, pscale_ref, cw_ref, cb_ref,
                         wg_ref, ba_ref, bx_ref, lam_ref, wout_ref, lng_ref, lnb_ref,
                         o_ref, npool_ref, nconv_ref, nh_ref,
                         zbuf, cbuf, hst, sa, sb, *, tm, tiles_per_batch, scan_pad):
    pid = pl.program_id(0)
    j = pid % tiles_per_batch
    n_steps = int(math.log2(tm))

    @pl.when(pid == 0)
    def _init_scan_prefix():
        sa[:, 0:scan_pad, :] = jnp.ones((2, scan_pad, LRU_WIDTH), F32)
        sb[:, 0:scan_pad, :] = jnp.zeros((2, scan_pad, LRU_WIDTH), F32)

    @pl.when(j == 0)
    def _new_sequence():
        zbuf[0:POOL_HIST, :] = jnp.zeros((POOL_HIST, POOL_WIDTH), F32)
        cbuf[0:CONV_HIST, :] = jnp.zeros((CONV_HIST, LRU_WIDTH), F32)
        hst[...] = jnp.zeros_like(hst)

    x = x_ref[...]
    sh, sc, g = _mod_rows(sh_ref, sc_ref, g_ref, False, tiles_per_batch)
    h = (x * (1.0 + sc) + sh).astype(BF16)
    proj = jnp.dot(h, win_ref[...], preferred_element_type=F32)
    u_pool = proj[:, :POOL_WIDTH]
    u_lru = proj[:, POOL_WIDTH:POOL_WIDTH + LRU_WIDTH]
    u_gate = proj[:, POOL_WIDTH + LRU_WIDTH:]

    zbuf[POOL_HIST:POOL_HIST + tm, :] = u_pool
    pos = j * tm + lax.broadcasted_iota(jnp.int32, (tm, POOL_GROUP_DIM), 0)
    pooled = []
    for gi, w in enumerate(POOL_WINDOWS):
        cols = slice(gi * POOL_GROUP_DIM, (gi + 1) * POOL_GROUP_DIM)
        s = zbuf[POOL_HIST:POOL_HIST + tm, cols]
        for back in range(1, w):
            s = s + zbuf[POOL_HIST - back:POOL_HIST - back + tm, cols]
        cnt = jnp.minimum(pos + 1, w).astype(F32)
        pooled.append(s / cnt - u_pool[:, cols])
    pooled = jnp.concatenate(pooled, axis=-1).astype(BF16)
    y_pool = jnp.dot(pooled, wpool_ref[...], preferred_element_type=F32) * pscale_ref[...]
    npool_ref[...] = zbuf[tm + POOL_HIST - POOL_BUF:tm + POOL_HIST, :]
    zbuf[0:POOL_HIST, :] = zbuf[tm:tm + POOL_HIST, :]

    cbuf[CONV_HIST:CONV_HIST + tm, :] = u_lru
    xc = None
    for k in range(CONV_WIDTH):
        off = CONV_HIST - (CONV_WIDTH - 1) + k
        term = cbuf[off:off + tm, :] * cw_ref[k:k + 1, :]
        xc = term if xc is None else xc + term
    xc = cb_ref[...] + xc
    nconv_ref[...] = cbuf[tm + CONV_HIST - (CONV_WIDTH - 1):tm + CONV_HIST, :]
    cbuf[0:CONV_HIST, :] = cbuf[tm:tm + CONV_HIST, :]

    a, b = _gates(xc, wg_ref, ba_ref, bx_ref, lam_ref)
    sa[0, scan_pad:scan_pad + tm, :] = a
    sb[0, scan_pad:scan_pad + tm, :] = b
    sb[0, scan_pad:scan_pad + 1, :] = a[0:1, :] * hst[...] + b[0:1, :]
    hs = None
    for s_i in range(n_steps):
        d = 1 << s_i
        src, dst = s_i % 2, 1 - s_i % 2
        a_cur = sa[src, scan_pad:scan_pad + tm, :]
        b_new = a_cur * sb[src, scan_pad - d:scan_pad - d + tm, :] + sb[src, scan_pad:scan_pad + tm, :]
        if s_i == n_steps - 1:
            hs = b_new
        else:
            sa[dst, scan_pad:scan_pad + tm, :] = a_cur * sa[src, scan_pad - d:scan_pad - d + tm, :]
            sb[dst, scan_pad:scan_pad + tm, :] = b_new
    hst[...] = hs[tm - 1:tm, :]
    nh_ref[...] = hs[tm - 1:tm, :]
    y_lru = hs * jax.nn.gelu(u_gate)

    o_ref[...] = _mix_out(x, g, y_pool, y_lru, wout_ref, lng_ref, lnb_ref)


def _mixer_prompt_call(x2d, mod, wts, *, layer, tm, seq, n_sample, n_prompt):
    rows = x2d.shape[0]
    tpb = seq // tm
    scan_pad = max(tm // 2, SUBLANES)
    kern = functools.partial(_mixer_prompt_kernel, tm=tm, tiles_per_batch=tpb, scan_pad=scan_pad)
    out_shapes = (
        jax.ShapeDtypeStruct((rows, D_MODEL), F32),
        jax.ShapeDtypeStruct((n_prompt, POOL_BUF, POOL_WIDTH), F32),
        jax.ShapeDtypeStruct((n_prompt, CONV_WIDTH - 1, LRU_WIDTH), F32),
        jax.ShapeDtypeStruct((n_prompt, 1, LRU_WIDTH), F32),
    )
    return pl.pallas_call(
        kern,
        out_shape=out_shapes,
        grid=(rows // tm,),
        in_specs=[pl.BlockSpec((tm, D_MODEL), lambda i: (i, 0))]
        + _mod_specs(layer, 1, False, n_sample, n_prompt)
        + [_resident(w.shape, layer) for w in wts],
        out_specs=(
            pl.BlockSpec((tm, D_MODEL), lambda i: (i, 0)),
            pl.BlockSpec((None, POOL_BUF, POOL_WIDTH), lambda i: (i // tpb, 0, 0)),
            pl.BlockSpec((None, CONV_WIDTH - 1, LRU_WIDTH), lambda i: (i // tpb, 0, 0)),
            pl.BlockSpec((None, 1, LRU_WIDTH), lambda i: (i // tpb, 0, 0)),
        ),
        scratch_shapes=[
            pltpu.VMEM((POOL_HIST + tm, POOL_WIDTH), F32),
            pltpu.VMEM((CONV_HIST + tm, LRU_WIDTH), F32),
            pltpu.VMEM((1, LRU_WIDTH), F32),
            pltpu.VMEM((2, scan_pad + tm, LRU_WIDTH), F32),
            pltpu.VMEM((2, scan_pad + tm, LRU_WIDTH), F32),
        ],
        compiler_params=pltpu.CompilerParams(
            dimension_semantics=("arbitrary",),
            vmem_limit_bytes=48 * MIB),
        name="mixer_p",
    )(x2d, mod, mod, mod, *wts)


def _mixer_sample_kernel(x_ref, sh_ref, sc_ref, g_ref, pst_ref, cst_ref, h0_ref,
                         win_ref, wpool_ref, pscale_ref, cw_ref, cb_ref,
                         wg_ref, ba_ref, bx_ref, lam_ref, wout_ref, lng_ref, lnb_ref,
                         o_ref, npool_ref, nconv_ref, nh_ref):
    x = x_ref[...]
    sh, sc, g = sh_ref[...], sc_ref[...], g_ref[...]
    h = (x * (1.0 + sc) + sh).astype(BF16)
    proj = jnp.dot(h, win_ref[...], preferred_element_type=F32)
    u_pool = proj[:, :POOL_WIDTH]
    u_lru = proj[:, POOL_WIDTH:POOL_WIDTH + LRU_WIDTH]
    u_gate = proj[:, POOL_WIDTH + LRU_WIDTH:]

    pooled = []
    for gi, w in enumerate(POOL_WINDOWS):
        lo = gi * POOL_GROUP_DIM
        s = u_pool[:, lo:lo + POOL_GROUP_DIM]
        for back in range(1, w):
            base = (POOL_BUF - back) * POOL_WIDTH + lo
            s = s + pst_ref[:, base:base + POOL_GROUP_DIM]
        cnt = float(min(PAST_LEN + 1, w))
        pooled.append(s / cnt - u_pool[:, lo:lo + POOL_GROUP_DIM])
    pooled = jnp.concatenate(pooled, axis=-1).astype(BF16)
    y_pool = jnp.dot(pooled, wpool_ref[...], preferred_element_type=F32) * pscale_ref[...]
    npool_ref[:, :(POOL_BUF - 1) * POOL_WIDTH] = pst_ref[:, POOL_WIDTH:]
    npool_ref[:, (POOL_BUF - 1) * POOL_WIDTH:] = u_pool

    xc = None
    for k in range(CONV_WIDTH - 1):
        term = cst_ref[:, k * LRU_WIDTH:(k + 1) * LRU_WIDTH] * cw_ref[k:k + 1, :]
        xc = term if xc is None else xc + term
    xc = cb_ref[...] + (xc + u_lru * cw_ref[CONV_WIDTH - 1:CONV_WIDTH, :])
    nconv_ref[:, :(CONV_WIDTH - 2) * LRU_WIDTH] = cst_ref[:, LRU_WIDTH:]
    nconv_ref[:, (CONV_WIDTH - 2) * LRU_WIDTH:] = u_lru

    a, b = _gates(xc, wg_ref, ba_ref, bx_ref, lam_ref)
    hs = a * h0_ref[...] + b
    nh_ref[...] = hs
    y_lru = hs * jax.nn.gelu(u_gate)

    o_ref[...] = _mix_out(x, g, y_pool, y_lru, wout_ref, lng_ref, lnb_ref)


def _mixer_sample_call(x2d, mod, pst, cst, h0, wts, *, layer, n_sample, n_prompt):
    rows = x2d.shape[0]
    whole = lambda shape: pl.BlockSpec(tuple(shape), lambda i: (0,) * len(shape))
    out_shapes = (
        jax.ShapeDtypeStruct((rows, D_MODEL), F32),
        jax.ShapeDtypeStruct(pst.shape, F32),
        jax.ShapeDtypeStruct(cst.shape, F32),
        jax.ShapeDtypeStruct(h0.shape, F32),
    )
    return pl.pallas_call(
        _mixer_sample_kernel,
        out_shape=out_shapes,
        grid=(1,),
        in_specs=[whole(x2d.shape)]
        + _mod_specs(layer, 1, True, n_sample, n_prompt)
        + [whole(pst.shape), whole(cst.shape), whole(h0.shape)]
        + [_resident(w.shape, layer) for w in wts],
        out_specs=tuple(whole(s.shape) for s in out_shapes),
        compiler_params=pltpu.CompilerParams(
            dimension_semantics=("arbitrary",),
            vmem_limit_bytes=48 * MIB),
        name="mixer_s",
    )(x2d, mod, mod, mod, pst, cst, h0, *wts)


def _block_diag(w):
    depth, groups, c, _ = w.shape
    eye = jnp.eye(groups, dtype=w.dtype)
    return jnp.einsum('lgij,gh->lgihj', w, eye).reshape(depth, groups * c, groups * c)


def _chunked_gu(wgu):
    depth, d, _ = wgu.shape
    w = wgu.reshape(depth, d, 2, FFN_N_CHUNKS, FFN_CHUNK)
    return jnp.transpose(w, (0, 1, 3, 2, 4)).reshape(depth, d, 2 * D_FF).astype(BF16)


def kernel(x_prompt, x_sample, state_pool, state_conv, state_rglru, c_prompt, c_sample, w_ada, b_ada, ffn1_wgu, ffn1_wdown, w_in, w_out, w_pool, pool_scale, conv_w, conv_b, w_rg_a, b_rg_a, w_rg_x, b_rg_x, lru_lambda, ffn2_wgu, ffn2_wdown, ln_g, ln_b):
    n_prompt, seq, _ = x_prompt.shape
    n_sample = x_sample.shape[0]
    assert x_sample.shape[1] == 1 and n_sample % n_prompt == 0 and n_prompt == SUBLANES

    wgu1, wgu2 = _chunked_gu(ffn1_wgu), _chunked_gu(ffn2_wgu)
    wd1, wd2 = ffn1_wdown.astype(BF16), ffn2_wdown.astype(BF16)
    row = lambda v: v.reshape(DEPTH, 1, v.shape[-1])
    mixer_w = (
        w_in.astype(BF16), _block_diag(w_pool).astype(BF16), row(pool_scale), conv_w, row(conv_b),
        jnp.concatenate([_block_diag(w_rg_a), _block_diag(w_rg_x)], axis=-1).astype(BF16),
        row(b_rg_a), row(b_rg_x), row(lru_lambda), w_out.astype(BF16), ln_g, ln_b,
    )

    mod = _ada_call(jnp.concatenate([c_sample, c_prompt], axis=0), w_ada, b_ada)

    xp = x_prompt.reshape(n_prompt * seq, D_MODEL)
    xs = x_sample.reshape(n_sample, D_MODEL)
    pst_all = state_pool.reshape(DEPTH, n_sample, POOL_BUF * POOL_WIDTH)
    cst_all = state_conv.reshape(DEPTH, n_sample, (CONV_WIDTH - 1) * LRU_WIDTH)

    common = dict(n_sample=n_sample, n_prompt=n_prompt)
    outs = [[] for _ in range(6)]
    for l in range(DEPTH):
        ffn_p = functools.partial(_ffn_call, layer=l, per_row=False, tm=512, seq=seq, **common)
        ffn_s = functools.partial(_ffn_call, layer=l, per_row=True, tm=n_sample, seq=1, **common)
        xp = ffn_p(xp, mod, wgu1, wd1, ln_g, ln_b, sub=0, ln_idx=0)
        xs = ffn_s(xs, mod, wgu1, wd1, ln_g, ln_b, sub=0, ln_idx=0)
        xp, pool_p, conv_p, h_p = _mixer_prompt_call(xp, mod, mixer_w, layer=l, tm=256, seq=seq, **common)
        xs, pool_s, conv_s, h_s = _mixer_sample_call(xs, mod, pst_all[l], cst_all[l], state_rglru[l], mixer_w,
                                                     layer=l, **common)
        xp = ffn_p(xp, mod, wgu2, wd2, ln_g, ln_b, sub=2, ln_idx=2)
        xs = ffn_s(xs, mod, wgu2, wd2, ln_g, ln_b, sub=2, ln_idx=2)
        for acc, v in zip(outs, (pool_p, conv_p, h_p.reshape(n_prompt, LRU_WIDTH),
                                 pool_s.reshape(n_sample, POOL_BUF, POOL_WIDTH),
                                 conv_s.reshape(n_sample, CONV_WIDTH - 1, LRU_WIDTH), h_s)):
            acc.append(v)

    return (xp.reshape(n_prompt, seq, D_MODEL), xs.reshape(n_sample, 1, D_MODEL)) + tuple(jnp.stack(o) for o in outs)
```

```python
import functools
import math

import jax
import jax.numpy as jnp
from jax import lax
from jax.experimental import pallas as pl
from jax.experimental.pallas import tpu as pltpu

D_MODEL = 1024
DEPTH = 4
PAST_LEN = 16384
POOL_WIDTH = D_MODEL // 2
LRU_WIDTH = D_MODEL - POOL_WIDTH
IN_WIDTH = POOL_WIDTH + 2 * LRU_WIDTH
POOL_WINDOWS = (2, 4, 8, 16)
POOL_GROUP_DIM = POOL_WIDTH // len(POOL_WINDOWS)
POOL_BUF = max(POOL_WINDOWS) - 1
N_LRU_HEADS = 8
LRU_HEAD_DIM = LRU_WIDTH // N_LRU_HEADS
CONV_WIDTH = 4
LRU_C = 8.0
D_FF = 128 * int(round(8 * D_MODEL / 3 / 128))
N_MOD = 9
ALPHA = (2 * DEPTH) ** 0.25
LN_EPS = 1e-5

F32 = jnp.float32
BF16 = jnp.bfloat16

SUBLANES = 8
LANES = 128
MIB = 1024 * 1024

FFN_TILE = 512
FFN_SUBTILES = 2
MIXER_TILE = 256
POOL_HIST = 16
CONV_HIST = SUBLANES


def _layer_norm(y, g, b):
    mu = jnp.mean(y, axis=-1, keepdims=True)
    d = y - mu
    var = jnp.mean(d * d, axis=-1, keepdims=True)
    return d * lax.rsqrt(var + LN_EPS) * g + b


def _silu(x):
    return x * jax.nn.sigmoid(x)


def _softplus(x):
    return jnp.maximum(x, 0.0) + jnp.log1p(jnp.exp(-jnp.abs(x)))


def _mod_rows(sh_ref, sc_ref, g_ref, per_row, tiles_per_batch):
    if per_row:
        return sh_ref[...], sc_ref[...], g_ref[...]
    b = pl.program_id(0) // tiles_per_batch
    return sh_ref[pl.ds(b, 1), :], sc_ref[pl.ds(b, 1), :], g_ref[pl.ds(b, 1), :]


def _ada_kernel(c_ref, w_ref, b_ref, o_ref):
    sc = _silu(c_ref[...]).astype(BF16)
    w = w_ref[...].astype(BF16)
    o_ref[...] = jnp.dot(sc, w, preferred_element_type=F32) + b_ref[...]


def _ada_call(c_all, w_ada, b_ada):
    n_rows = c_all.shape[0]
    n_out = w_ada.shape[-1]
    tn = n_out // 8
    return pl.pallas_call(
        _ada_kernel,
        out_shape=jax.ShapeDtypeStruct((DEPTH, n_rows, n_out), F32),
        grid=(DEPTH, n_out // tn),
        in_specs=[
            pl.BlockSpec((n_rows, D_MODEL), lambda l, j: (0, 0)),
            pl.BlockSpec((None, D_MODEL, tn), lambda l, j: (l, 0, j)),
            pl.BlockSpec((None, 1, tn), lambda l, j: (l, 0, j)),
        ],
        out_specs=pl.BlockSpec((None, n_rows, tn), lambda l, j: (l, 0, j)),
        compiler_params=pltpu.CompilerParams(
            dimension_semantics=("arbitrary", "arbitrary"),
            vmem_limit_bytes=32 * MIB),
        name="ada",
    )(c_all, w_ada, b_ada.reshape(DEPTH, 1, n_out))


def _ffn_kernel(x_ref, sh_ref, sc_ref, g_ref, wgu_ref, wd_ref, lng_ref, lnb_ref, o_ref,
                *, per_row, tiles_per_batch, ln_idx, subtiles):
    sh, sc, g = _mod_rows(sh_ref, sc_ref, g_ref, per_row, tiles_per_batch)
    rows = x_ref.shape[0] // subtiles
    for s_i in range(subtiles):
        rs = slice(s_i * rows, (s_i + 1) * rows)
        x = x_ref[rs, :]
        h = (x * (1.0 + sc) + sh).astype(BF16)
        gu = jnp.dot(h, wgu_ref[...], preferred_element_type=F32)
        act = (_silu(gu[:, :D_FF]) * gu[:, D_FF:]).astype(BF16)
        acc = jnp.dot(act, wd_ref[...], preferred_element_type=F32)
        y = ALPHA * x + (0.5 * (1.0 + g)) * acc
        o_ref[rs, :] = _layer_norm(y, lng_ref[ln_idx:ln_idx + 1, :], lnb_ref[ln_idx:ln_idx + 1, :])


def _mod_specs(layer, sub, per_row, n_sample, n_prompt):
    if per_row:
        return [pl.BlockSpec((None, n_sample, D_MODEL), functools.partial(lambda i, k: (layer, 0, k), k=3 * sub + t))
                for t in range(3)]
    prompt_block = n_sample // n_prompt
    return [pl.BlockSpec((None, n_prompt, D_MODEL), functools.partial(lambda i, k: (layer, prompt_block, k), k=3 * sub + t))
            for t in range(3)]


def _resident(shape, layer):
    nd = len(shape)
    return pl.BlockSpec((None,) + tuple(shape[1:]), lambda i: (layer,) + (0,) * (nd - 1),
                        pipeline_mode=pl.Buffered(1))


def _layer_block(shape, layer):
    nd = len(shape)
    return pl.BlockSpec((None,) + tuple(shape[1:]), lambda i: (layer,) + (0,) * (nd - 1))


def _ffn_call(x2d, mod, wgu, wd, ln_g, ln_b, *, layer, sub, ln_idx, per_row, tm, seq, n_sample, n_prompt):
    rows = x2d.shape[0]
    kern = functools.partial(_ffn_kernel, per_row=per_row, tiles_per_batch=max(seq // tm, 1), ln_idx=ln_idx,
                             subtiles=1 if per_row else FFN_SUBTILES)
    return pl.pallas_call(
        kern,
        out_shape=jax.ShapeDtypeStruct((rows, D_MODEL), F32),
        grid=(rows // tm,),
        in_specs=[pl.BlockSpec((tm, D_MODEL), lambda i: (i, 0))]
        + _mod_specs(layer, sub, per_row, n_sample, n_prompt)
        + [_resident(wgu.shape, layer), _resident(wd.shape, layer),
           _resident(ln_g.shape, layer), _resident(ln_b.shape, layer)],
        out_specs=pl.BlockSpec((tm, D_MODEL), lambda i: (i, 0)),
        compiler_params=pltpu.CompilerParams(
            dimension_semantics=("arbitrary",),
            vmem_limit_bytes=52 * MIB),
        name=f"ffn{sub}_{'s' if per_row else 'p'}",
    )(x2d, mod, mod, mod, wgu, wd, ln_g, ln_b)


def _gates(xc, wg_ref, ba_ref, bx_ref, lam_ref):
    gates = jnp.dot(xc.astype(BF16), wg_ref[...], preferred_element_type=F32)
    r = jax.nn.sigmoid(gates[:, :LRU_WIDTH] + ba_ref[...])
    i = jax.nn.sigmoid(gates[:, LRU_WIDTH:] + bx_ref[...])
    log_a = (-LRU_C * r) * _softplus(-lam_ref[...])
    a = jnp.exp(log_a)
    t = jnp.tanh(log_a)
    mult = jnp.sqrt((-2.0 * t) / (1.0 - t))
    return a, mult * (i * xc)


def _mix_out(x, g, y_pool, y_lru, wout_ref, lng_ref, lnb_ref):
    mix = jnp.dot(y_pool.astype(BF16), wout_ref[:POOL_WIDTH, :], preferred_element_type=F32)
    mix = mix + jnp.dot(y_lru.astype(BF16), wout_ref[POOL_WIDTH:, :], preferred_element_type=F32)
    y = ALPHA * x + (1.0 + g) * mix
    return _layer_norm(y, lng_ref[1:2, :], lnb_ref[1:2, :])


def _mixer_prompt_kernel(x_ref, sh_ref, sc_ref, g_ref, win_ref, wpool_ref, pscale_ref, cw_ref, cb_ref,
                         wg_ref, ba_ref, bx_ref, lam_ref, wout_ref, lng_ref, lnb_ref,
                         npool_in, nconv_in, nh_in,
                         o_ref, npool_ref, nconv_ref, nh_ref,
                         zbuf, cbuf, hst, sa, sb, *, tm, tiles_per_batch, scan_pad):
    del npool_in, nconv_in, nh_in
    pid = pl.program_id(0)
    bidx = pid // tiles_per_batch
    j = pid % tiles_per_batch
    n_steps = int(math.log2(tm))

    @pl.when(pid == 0)
    def _init_scan_prefix():
        sa[:, 0:scan_pad, :] = jnp.ones((2, scan_pad, LRU_WIDTH), F32)
        sb[:, 0:scan_pad, :] = jnp.zeros((2, scan_pad, LRU_WIDTH), F32)

    @pl.when(j == 0)
    def _new_sequence():
        zbuf[0:POOL_HIST, :] = jnp.zeros((POOL_HIST, POOL_WIDTH), F32)
        cbuf[0:CONV_HIST, :] = jnp.zeros((CONV_HIST, LRU_WIDTH), F32)
        hst[...] = jnp.zeros_like(hst)

    x = x_ref[...]
    sh, sc, g = _mod_rows(sh_ref, sc_ref, g_ref, False, tiles_per_batch)
    h = (x * (1.0 + sc) + sh).astype(BF16)
    proj = jnp.dot(h, win_ref[...], preferred_element_type=F32)
    u_pool = proj[:, :POOL_WIDTH]
    u_lru = proj[:, POOL_WIDTH:POOL_WIDTH + LRU_WIDTH]
    u_gate = proj[:, POOL_WIDTH + LRU_WIDTH:]

    zbuf[POOL_HIST:POOL_HIST + tm, :] = u_pool
    pos = j * tm + lax.broadcasted_iota(jnp.int32, (tm, POOL_GROUP_DIM), 0)
    pooled = []
    for gi, w in enumerate(POOL_WINDOWS):
        cols = slice(gi * POOL_GROUP_DIM, (gi + 1) * POOL_GROUP_DIM)
        s = zbuf[POOL_HIST:POOL_HIST + tm, cols]
        for back in range(1, w):
            s = s + zbuf[POOL_HIST - back:POOL_HIST - back + tm, cols]
        cnt = jnp.minimum(pos + 1, w).astype(F32)
        pooled.append(s / cnt - u_pool[:, cols])
    pooled = jnp.concatenate(pooled, axis=-1).astype(BF16)
    y_pool = jnp.dot(pooled, wpool_ref[...], preferred_element_type=F32) * pscale_ref[...]
    npool_ref[...] = zbuf[tm + POOL_HIST - POOL_BUF:tm + POOL_HIST, :]
    zbuf[0:POOL_HIST, :] = zbuf[tm:tm + POOL_HIST, :]

    cbuf[CONV_HIST:CONV_HIST + tm, :] = u_lru
    xc = None
    for k in range(CONV_WIDTH):
        off = CONV_HIST - (CONV_WIDTH - 1) + k
        term = cbuf[off:off + tm, :] * cw_ref[k:k + 1, :]
        xc = term if xc is None else xc + term
    xc = cb_ref[...] + xc
    nconv_ref[...] = cbuf[tm + CONV_HIST - (CONV_WIDTH - 1):tm + CONV_HIST, :]
    cbuf[0:CONV_HIST, :] = cbuf[tm:tm + CONV_HIST, :]

    a, b = _gates(xc, wg_ref, ba_ref, bx_ref, lam_ref)
    sa[0, scan_pad:scan_pad + tm, :] = a
    sb[0, scan_pad:scan_pad + tm, :] = b
    sb[0, scan_pad:scan_pad + 1, :] = a[0:1, :] * hst[...] + b[0:1, :]
    hs = None
    for s_i in range(n_steps):
        d = 1 << s_i
        src, dst = s_i % 2, 1 - s_i % 2
        a_cur = sa[src, scan_pad:scan_pad + tm, :]
        b_new = a_cur * sb[src, scan_pad - d:scan_pad - d + tm, :] + sb[src, scan_pad:scan_pad + tm, :]
        if s_i == n_steps - 1:
            hs = b_new
        else:
            sa[dst, scan_pad:scan_pad + tm, :] = a_cur * sa[src, scan_pad - d:scan_pad - d + tm, :]
            sb[dst, scan_pad:scan_pad + tm, :] = b_new
    hst[...] = hs[tm - 1:tm, :]
    nh_ref[pl.ds(bidx, 1), :] = hs[tm - 1:tm, :]
    y_lru = hs * jax.nn.gelu(u_gate)

    o_ref[...] = _mix_out(x, g, y_pool, y_lru, wout_ref, lng_ref, lnb_ref)


def _mixer_prompt_call(x2d, mod, wts, new_states, *, layer, tm, seq, n_sample, n_prompt):
    rows = x2d.shape[0]
    tpb = seq // tm
    scan_pad = max(tm // 2, SUBLANES)
    kern = functools.partial(_mixer_prompt_kernel, tm=tm, tiles_per_batch=tpb, scan_pad=scan_pad)
    npool, nconv, nh = new_states
    n_in = 4 + len(wts)
    any_spec = pl.BlockSpec(memory_space=pl.ANY)
    return pl.pallas_call(
        kern,
        out_shape=(jax.ShapeDtypeStruct((rows, D_MODEL), F32),
                   jax.ShapeDtypeStruct(npool.shape, F32),
                   jax.ShapeDtypeStruct(nconv.shape, F32),
                   jax.ShapeDtypeStruct(nh.shape, F32)),
        grid=(rows // tm,),
        in_specs=[pl.BlockSpec((tm, D_MODEL), lambda i: (i, 0))]
        + _mod_specs(layer, 1, False, n_sample, n_prompt)
        + [_resident(w.shape, layer) for w in wts]
        + [any_spec, any_spec, any_spec],
        out_specs=(
            pl.BlockSpec((tm, D_MODEL), lambda i: (i, 0)),
            pl.BlockSpec((None, None, POOL_BUF, POOL_WIDTH), lambda i: (layer, i // tpb, 0, 0)),
            pl.BlockSpec((None, None, CONV_WIDTH - 1, LRU_WIDTH), lambda i: (layer, i // tpb, 0, 0)),
            pl.BlockSpec((None, n_prompt, LRU_WIDTH), lambda i: (layer, 0, 0)),
        ),
        scratch_shapes=[
            pltpu.VMEM((POOL_HIST + tm, POOL_WIDTH), F32),
            pltpu.VMEM((CONV_HIST + tm, LRU_WIDTH), F32),
            pltpu.VMEM((1, LRU_WIDTH), F32),
            pltpu.VMEM((2, scan_pad + tm, LRU_WIDTH), F32),
            pltpu.VMEM((2, scan_pad + tm, LRU_WIDTH), F32),
        ],
        input_output_aliases={n_in: 1, n_in + 1: 2, n_in + 2: 3},
        compiler_params=pltpu.CompilerParams(
            dimension_semantics=("arbitrary",),
            vmem_limit_bytes=48 * MIB),
        name="mixer_p",
    )(x2d, mod, mod, mod, *wts, npool, nconv, nh)


def _mixer_sample_kernel(x_ref, sh_ref, sc_ref, g_ref, pst_ref, cst_ref, h0_ref,
                         win_ref, wpool_ref, pscale_ref, cw_ref, cb_ref,
                         wg_ref, ba_ref, bx_ref, lam_ref, wout_ref, lng_ref, lnb_ref,
                         npool_in, nconv_in, nh_in,
                         o_ref, npool_ref, nconv_ref, nh_ref):
    del npool_in, nconv_in, nh_in
    x = x_ref[...]
    sh, sc, g = sh_ref[...], sc_ref[...], g_ref[...]
    h = (x * (1.0 + sc) + sh).astype(BF16)
    proj = jnp.dot(h, win_ref[...], preferred_element_type=F32)
    u_pool = proj[:, :POOL_WIDTH]
    u_lru = proj[:, POOL_WIDTH:POOL_WIDTH + LRU_WIDTH]
    u_gate = proj[:, POOL_WIDTH + LRU_WIDTH:]

    pooled = []
    for gi, w in enumerate(POOL_WINDOWS):
        cols = slice(gi * POOL_GROUP_DIM, (gi + 1) * POOL_GROUP_DIM)
        s = u_pool[:, cols]
        for back in range(1, w):
            s = s + pst_ref[:, POOL_BUF - back, cols]
        cnt = float(min(PAST_LEN + 1, w))
        pooled.append(s / cnt - u_pool[:, cols])
    pooled = jnp.concatenate(pooled, axis=-1).astype(BF16)
    y_pool = jnp.dot(pooled, wpool_ref[...], preferred_element_type=F32) * pscale_ref[...]
    npool_ref[:, 0:POOL_BUF - 1, :] = pst_ref[:, 1:POOL_BUF, :]
    npool_ref[:, POOL_BUF - 1, :] = u_pool

    xc = None
    for k in range(CONV_WIDTH - 1):
        term = cst_ref[:, k, :] * cw_ref[k:k + 1, :]
        xc = term if xc is None else xc + term
    xc = cb_ref[...] + (xc + u_lru * cw_ref[CONV_WIDTH - 1:CONV_WIDTH, :])
    nconv_ref[:, 0:CONV_WIDTH - 2, :] = cst_ref[:, 1:CONV_WIDTH - 1, :]
    nconv_ref[:, CONV_WIDTH - 2, :] = u_lru

    a, b = _gates(xc, wg_ref, ba_ref, bx_ref, lam_ref)
    hs = a * h0_ref[...] + b
    nh_ref[...] = hs
    y_lru = hs * jax.nn.gelu(u_gate)

    o_ref[...] = _mix_out(x, g, y_pool, y_lru, wout_ref, lng_ref, lnb_ref)


def _mixer_sample_call(x2d, mod, state_pool, state_conv, state_h, wts, new_states, *, layer, n_sample, n_prompt):
    rows = x2d.shape[0]
    npool, nconv, nh = new_states
    n_in = 7 + len(wts)
    any_spec = pl.BlockSpec(memory_space=pl.ANY)
    return pl.pallas_call(
        _mixer_sample_kernel,
        out_shape=(jax.ShapeDtypeStruct((rows, D_MODEL), F32),
                   jax.ShapeDtypeStruct(npool.shape, F32),
                   jax.ShapeDtypeStruct(nconv.shape, F32),
                   jax.ShapeDtypeStruct(nh.shape, F32)),
        grid=(1,),
        in_specs=[pl.BlockSpec((rows, D_MODEL), lambda i: (0, 0))]
        + _mod_specs(layer, 1, True, n_sample, n_prompt)
        + [_layer_block(state_pool.shape, layer), _layer_block(state_conv.shape, layer),
           _layer_block(state_h.shape, layer)]
        + [_resident(w.shape, layer) for w in wts]
        + [any_spec, any_spec, any_spec],
        out_specs=(pl.BlockSpec((rows, D_MODEL), lambda i: (0, 0)),
                   _layer_block(npool.shape, layer), _layer_block(nconv.shape, layer), _layer_block(nh.shape, layer)),
        input_output_aliases={n_in: 1, n_in + 1: 2, n_in + 2: 3},
        compiler_params=pltpu.CompilerParams(
            dimension_semantics=("arbitrary",),
            vmem_limit_bytes=48 * MIB),
        name="mixer_s",
    )(x2d, mod, mod, mod, state_pool, state_conv, state_h, *wts, npool, nconv, nh)


def _block_diag(w):
    depth, groups, c, _ = w.shape
    eye = jnp.eye(groups, dtype=w.dtype)
    return jnp.einsum('lgij,gh->lgihj', w, eye).reshape(depth, groups * c, groups * c)


def kernel(x_prompt, x_sample, state_pool, state_conv, state_rglru, c_prompt, c_sample, w_ada, b_ada, ffn1_wgu, ffn1_wdown, w_in, w_out, w_pool, pool_scale, conv_w, conv_b, w_rg_a, b_rg_a, w_rg_x, b_rg_x, lru_lambda, ffn2_wgu, ffn2_wdown, ln_g, ln_b):
    n_prompt, seq, _ = x_prompt.shape
    n_sample = x_sample.shape[0]
    assert x_sample.shape[1] == 1 and n_sample % n_prompt == 0 and n_prompt == SUBLANES

    wgu1, wgu2 = ffn1_wgu.astype(BF16), ffn2_wgu.astype(BF16)
    wd1, wd2 = ffn1_wdown.astype(BF16), ffn2_wdown.astype(BF16)
    row = lambda v: v.reshape(DEPTH, 1, v.shape[-1])
    mixer_w = (
        w_in.astype(BF16), _block_diag(w_pool).astype(BF16), row(pool_scale), conv_w, row(conv_b),
        jnp.concatenate([_block_diag(w_rg_a), _block_diag(w_rg_x)], axis=-1).astype(BF16),
        row(b_rg_a), row(b_rg_x), row(lru_lambda), w_out.astype(BF16), ln_g, ln_b,
    )

    mod = _ada_call(jnp.concatenate([c_sample, c_prompt], axis=0), w_ada, b_ada)

    xp = x_prompt.reshape(n_prompt * seq, D_MODEL)
    xs = x_sample.reshape(n_sample, D_MODEL)
    new_p = (jnp.zeros((DEPTH, n_prompt, POOL_BUF, POOL_WIDTH), F32),
             jnp.zeros((DEPTH, n_prompt, CONV_WIDTH - 1, LRU_WIDTH), F32),
             jnp.zeros((DEPTH, n_prompt, LRU_WIDTH), F32))
    new_s = (jnp.zeros(state_pool.shape, F32), jnp.zeros(state_conv.shape, F32), jnp.zeros(state_rglru.shape, F32))

    common = dict(n_sample=n_sample, n_prompt=n_prompt)
    for l in range(DEPTH):
        ffn_p = functools.partial(_ffn_call, layer=l, per_row=False, tm=FFN_TILE, seq=seq, **common)
        ffn_s = functools.partial(_ffn_call, layer=l, per_row=True, tm=n_sample, seq=1, **common)
        xp = ffn_p(xp, mod, wgu1, wd1, ln_g, ln_b, sub=0, ln_idx=0)
        xs = ffn_s(xs, mod, wgu1, wd1, ln_g, ln_b, sub=0, ln_idx=0)
        xp, *new_p = _mixer_prompt_call(xp, mod, mixer_w, new_p, layer=l, tm=MIXER_TILE, seq=seq, **common)
        xs, *new_s = _mixer_sample_call(xs, mod, state_pool, state_conv, state_rglru, mixer_w, new_s,
                                        layer=l, **common)
        xp = ffn_p(xp, mod, wgu2, wd2, ln_g, ln_b, sub=2, ln_idx=2)
        xs = ffn_s(xs, mod, wgu2, wd2, ln_g, ln_b, sub=2, ln_idx=2)

    return (xp.reshape(n_prompt, seq, D_MODEL), xs.reshape(n_sample, 1, D_MODEL), *new_p, *new_s)
```

```python
import functools

import jax
import jax.numpy as jnp
from jax import lax
from jax.experimental import pallas as pl
from jax.experimental.pallas import tpu as pltpu

D_MODEL = 1024
DEPTH = 4
PAST_LEN = 16384
POOL_WIDTH = D_MODEL // 2
LRU_WIDTH = D_MODEL - POOL_WIDTH
IN_WIDTH = POOL_WIDTH + 2 * LRU_WIDTH
POOL_WINDOWS = (2, 4, 8, 16)
POOL_GROUP_DIM = POOL_WIDTH // len(POOL_WINDOWS)
POOL_BUF = max(POOL_WINDOWS) - 1
N_LRU_HEADS = 8
LRU_HEAD_DIM = LRU_WIDTH // N_LRU_HEADS
CONV_WIDTH = 4
LRU_C = 8.0
D_FF = 128 * int(round(8 * D_MODEL / 3 / 128))
N_MOD = 9
ALPHA = (2 * DEPTH) ** 0.25
LN_EPS = 1e-5

F32 = jnp.float32
BF16 = jnp.bfloat16

SUBLANES = 8
LANES = 128
MIB = 1024 * 1024

FFN_TILE = 512
FFN_SUBTILES = 2
MIXER_STEPS = 64


def _layer_norm(y, g, b):
    mu = jnp.mean(y, axis=-1, keepdims=True)
    d = y - mu
    var = jnp.mean(d * d, axis=-1, keepdims=True)
    return d * lax.rsqrt(var + LN_EPS) * g + b


def _silu(x):
    return x * jax.nn.sigmoid(x)


def _softplus(x):
    return jnp.maximum(x, 0.0) + jnp.log1p(jnp.exp(-jnp.abs(x)))


def _per_batch(fn, x, *ms):
    rows, d = x.shape
    nb = ms[0].shape[0]
    if nb == rows:
        return fn(x, *ms)
    return fn(x.reshape(rows // nb, nb, d), *[m[None] for m in ms]).reshape(rows, d)


def _modulate(x, sh, sc):
    return _per_batch(lambda x_, sh_, sc_: x_ * (1.0 + sc_) + sh_, x, sh, sc)


def _gated_residual(x, g, y, gate_scale):
    return _per_batch(lambda y_, g_: (gate_scale * (1.0 + g_)) * y_, y, g) + ALPHA * x


def _ada_kernel(c_ref, w_ref, b_ref, o_ref):
    sc = _silu(c_ref[...]).astype(BF16)
    w = w_ref[...].astype(BF16)
    o_ref[...] = jnp.dot(sc, w, preferred_element_type=F32) + b_ref[...]


def _ada_call(c_all, w_ada, b_ada):
    n_rows = c_all.shape[0]
    n_out = w_ada.shape[-1]
    tn = n_out // 8
    return pl.pallas_call(
        _ada_kernel,
        out_shape=jax.ShapeDtypeStruct((DEPTH, n_rows, n_out), F32),
        grid=(DEPTH, n_out // tn),
        in_specs=[
            pl.BlockSpec((n_rows, D_MODEL), lambda l, j: (0, 0)),
            pl.BlockSpec((None, D_MODEL, tn), lambda l, j: (l, 0, j)),
            pl.BlockSpec((None, 1, tn), lambda l, j: (l, 0, j)),
        ],
        out_specs=pl.BlockSpec((None, n_rows, tn), lambda l, j: (l, 0, j)),
        compiler_params=pltpu.CompilerParams(
            dimension_semantics=("arbitrary", "arbitrary"),
            vmem_limit_bytes=32 * MIB),
        name="ada",
    )(c_all, w_ada, b_ada.reshape(DEPTH, 1, n_out))


def _ffn_kernel(x_ref, sh_ref, sc_ref, g_ref, wgu_ref, wd_ref, lng_ref, lnb_ref, o_ref, *, ln_idx, subtiles):
    sh, sc, g = sh_ref[...], sc_ref[...], g_ref[...]
    rows = x_ref.shape[0] // subtiles
    for s_i in range(subtiles):
        rs = slice(s_i * rows, (s_i + 1) * rows)
        x = x_ref[rs, :]
        h = _modulate(x, sh, sc).astype(BF16)
        gu = jnp.dot(h, wgu_ref[...], preferred_element_type=F32)
        act = (_silu(gu[:, :D_FF]) * gu[:, D_FF:]).astype(BF16)
        acc = jnp.dot(act, wd_ref[...], preferred_element_type=F32)
        y = _gated_residual(x, g, acc, 0.5)
        o_ref[rs, :] = _layer_norm(y, lng_ref[ln_idx:ln_idx + 1, :], lnb_ref[ln_idx:ln_idx + 1, :])


def _mod_specs(layer, sub, is_sample, n_sample, n_prompt):
    rows, block = (n_sample, 0) if is_sample else (n_prompt, n_sample // n_prompt)
    return [pl.BlockSpec((None, rows, D_MODEL), functools.partial(lambda i, k: (layer, block, k), k=3 * sub + t))
            for t in range(3)]


def _resident(shape, layer):
    nd = len(shape)
    return pl.BlockSpec((None,) + tuple(shape[1:]), lambda i: (layer,) + (0,) * (nd - 1),
                        pipeline_mode=pl.Buffered(1))


def _layer_block(shape, layer):
    nd = len(shape)
    return pl.BlockSpec((None,) + tuple(shape[1:]), lambda i: (layer,) + (0,) * (nd - 1))


def _ffn_call(x2d, mod, wgu, wd, ln_g, ln_b, *, layer, sub, ln_idx, is_sample, tm, n_sample, n_prompt):
    rows = x2d.shape[0]
    kern = functools.partial(_ffn_kernel, ln_idx=ln_idx, subtiles=1 if is_sample else FFN_SUBTILES)
    return pl.pallas_call(
        kern,
        out_shape=jax.ShapeDtypeStruct((rows, D_MODEL), F32),
        grid=(rows // tm,),
        in_specs=[pl.BlockSpec((tm, D_MODEL), lambda i: (i, 0))]
        + _mod_specs(layer, sub, is_sample, n_sample, n_prompt)
        + [_resident(wgu.shape, layer), _resident(wd.shape, layer),
           _resident(ln_g.shape, layer), _resident(ln_b.shape, layer)],
        out_specs=pl.BlockSpec((tm, D_MODEL), lambda i: (i, 0)),
        compiler_params=pltpu.CompilerParams(
            dimension_semantics=("arbitrary",),
            vmem_limit_bytes=52 * MIB),
        name=f"ffn{sub}_{'s' if is_sample else 'p'}",
    )(x2d, mod, mod, mod, wgu, wd, ln_g, ln_b)


def _gates(xc, wg_ref, ba_ref, bx_ref, lam_ref):
    gates = jnp.dot(xc.astype(BF16), wg_ref[...], preferred_element_type=F32)
    r = jax.nn.sigmoid(gates[:, :LRU_WIDTH] + ba_ref[...])
    i = jax.nn.sigmoid(gates[:, LRU_WIDTH:] + bx_ref[...])
    log_a = (-LRU_C * r) * _softplus(-lam_ref[...])
    a = jnp.exp(log_a)
    t = jnp.tanh(log_a)
    mult = jnp.sqrt((-2.0 * t) / (1.0 - t))
    return a, mult * (i * xc)


def _mix_out(x, g, y_pool, y_lru, wout_ref, lng_ref, lnb_ref):
    mix = jnp.dot(y_pool.astype(BF16), wout_ref[:POOL_WIDTH, :], preferred_element_type=F32)
    mix = mix + jnp.dot(y_lru.astype(BF16), wout_ref[POOL_WIDTH:, :], preferred_element_type=F32)
    y = _gated_residual(x, g, mix, 1.0)
    return _layer_norm(y, lng_ref[1:2, :], lnb_ref[1:2, :])


def _mixer_prompt_kernel(x_ref, sh_ref, sc_ref, g_ref, win_ref, wpool_ref, pscale_ref, cw_ref, cb_ref,
                         wg_ref, ba_ref, bx_ref, lam_ref, wout_ref, lng_ref, lnb_ref,
                         npool_in, nconv_in, nh_in,
                         o_ref, npool_ref, nconv_ref, nh_ref,
                         zbuf, cbuf, hst, hsbuf, *, steps, nb):
    del npool_in, nconv_in, nh_in
    pid = pl.program_id(0)
    tm = steps * nb
    pool_hist = POOL_BUF * nb
    conv_hist = (CONV_WIDTH - 1) * nb

    @pl.when(pid == 0)
    def _new_sequences():
        zbuf[0:pool_hist, :] = jnp.zeros((pool_hist, POOL_WIDTH), F32)
        cbuf[0:conv_hist, :] = jnp.zeros((conv_hist, LRU_WIDTH), F32)
        hst[...] = jnp.zeros_like(hst)

    x = x_ref[...]
    h = _modulate(x, sh_ref[...], sc_ref[...]).astype(BF16)
    proj = jnp.dot(h, win_ref[...], preferred_element_type=F32)
    u_pool = proj[:, :POOL_WIDTH]
    u_lru = proj[:, POOL_WIDTH:POOL_WIDTH + LRU_WIDTH]
    u_gate = proj[:, POOL_WIDTH + LRU_WIDTH:]

    zbuf[pool_hist:pool_hist + tm, :] = u_pool
    t_idx = lax.shift_right_logical(lax.broadcasted_iota(jnp.int32, (tm, POOL_GROUP_DIM), 0), nb.bit_length() - 1)
    pos1 = pid * steps + t_idx + 1
    cur = zbuf[...]
    first, w = -POOL_BUF, 1
    pooled = []
    for gi, wnd in enumerate(POOL_WINDOWS):
        assert wnd == 2 * w
        n = cur.shape[0]
        cur = cur[w * nb:, :] + cur[:n - w * nb, :]
        first, w = first + w, wnd
        cols = slice(gi * POOL_GROUP_DIM, (gi + 1) * POOL_GROUP_DIM)
        s = cur[-first * nb:, :POOL_GROUP_DIM]
        pooled.append(s / jnp.minimum(pos1, wnd).astype(F32) - u_pool[:, cols])
        cur = cur[:, POOL_GROUP_DIM:]
    assert first == 0
    pooled = jnp.concatenate(pooled, axis=-1).astype(BF16)
    y_pool = jnp.dot(pooled, wpool_ref[...], preferred_element_type=F32) * pscale_ref[...]

    cbuf[conv_hist:conv_hist + tm, :] = u_lru
    xc = None
    for k in range(CONV_WIDTH):
        term = cbuf[k * nb:k * nb + tm, :] * cw_ref[k:k + 1, :]
        xc = term if xc is None else xc + term
    xc = cb_ref[...] + xc

    a, b = _gates(xc, wg_ref, ba_ref, bx_ref, lam_ref)
    hcur = hst[...]
    for t in range(steps):
        hcur = a[t * nb:(t + 1) * nb, :] * hcur + b[t * nb:(t + 1) * nb, :]
        hsbuf[t * nb:(t + 1) * nb, :] = hcur
    hst[...] = hcur
    y_lru = hsbuf[...] * jax.nn.gelu(u_gate)

    o_ref[...] = _mix_out(x, g_ref[...], y_pool, y_lru, wout_ref, lng_ref, lnb_ref)

    @pl.when(pid == pl.num_programs(0) - 1)
    def _final_state():
        for r in range(POOL_BUF):
            npool_ref[:, r, :] = zbuf[tm + r * nb:tm + (r + 1) * nb, :]
        for r in range(CONV_WIDTH - 1):
            nconv_ref[:, r, :] = cbuf[tm + r * nb:tm + (r + 1) * nb, :]
        nh_ref[...] = hcur

    zbuf[0:pool_hist, :] = zbuf[tm:tm + pool_hist, :]
    cbuf[0:conv_hist, :] = cbuf[tm:tm + conv_hist, :]


def _mixer_prompt_call(x2d, mod, wts, new_states, *, layer, steps, n_sample, n_prompt):
    rows = x2d.shape[0]
    tm = steps * n_prompt
    kern = functools.partial(_mixer_prompt_kernel, steps=steps, nb=n_prompt)
    npool, nconv, nh = new_states
    n_in = 4 + len(wts)
    any_spec = pl.BlockSpec(memory_space=pl.ANY)
    return pl.pallas_call(
        kern,
        out_shape=(jax.ShapeDtypeStruct((rows, D_MODEL), F32),
                   jax.ShapeDtypeStruct(npool.shape, F32),
                   jax.ShapeDtypeStruct(nconv.shape, F32),
                   jax.ShapeDtypeStruct(nh.shape, F32)),
        grid=(rows // tm,),
        in_specs=[pl.BlockSpec((tm, D_MODEL), lambda i: (i, 0))]
        + _mod_specs(layer, 1, False, n_sample, n_prompt)
        + [_resident(w.shape, layer) for w in wts]
        + [any_spec, any_spec, any_spec],
        out_specs=(pl.BlockSpec((tm, D_MODEL), lambda i: (i, 0)),
                   _layer_block(npool.shape, layer), _layer_block(nconv.shape, layer), _layer_block(nh.shape, layer)),
        scratch_shapes=[
            pltpu.VMEM((POOL_BUF * n_prompt + tm, POOL_WIDTH), F32),
            pltpu.VMEM(((CONV_WIDTH - 1) * n_prompt + tm, LRU_WIDTH), F32),
            pltpu.VMEM((n_prompt, LRU_WIDTH), F32),
            pltpu.VMEM((tm, LRU_WIDTH), F32),
        ],
        input_output_aliases={n_in: 1, n_in + 1: 2, n_in + 2: 3},
        compiler_params=pltpu.CompilerParams(
            dimension_semantics=("arbitrary",),
            vmem_limit_bytes=48 * MIB),
        name="mixer_p",
    )(x2d, mod, mod, mod, *wts, npool, nconv, nh)


def _mixer_sample_kernel(x_ref, sh_ref, sc_ref, g_ref, pst_ref, cst_ref, h0_ref,
                         win_ref, wpool_ref, pscale_ref, cw_ref, cb_ref,
                         wg_ref, ba_ref, bx_ref, lam_ref, wout_ref, lng_ref, lnb_ref,
                         npool_in, nconv_in, nh_in,
                         o_ref, npool_ref, nconv_ref, nh_ref):
    del npool_in, nconv_in, nh_in
    x = x_ref[...]
    h = _modulate(x, sh_ref[...], sc_ref[...]).astype(BF16)
    proj = jnp.dot(h, win_ref[...], preferred_element_type=F32)
    u_pool = proj[:, :POOL_WIDTH]
    u_lru = proj[:, POOL_WIDTH:POOL_WIDTH + LRU_WIDTH]
    u_gate = proj[:, POOL_WIDTH + LRU_WIDTH:]

    pooled = []
    for gi, w in enumerate(POOL_WINDOWS):
        cols = slice(gi * POOL_GROUP_DIM, (gi + 1) * POOL_GROUP_DIM)
        s = u_pool[:, cols]
        for back in range(1, w):
            s = s + pst_ref[:, POOL_BUF - back, cols]
        cnt = float(min(PAST_LEN + 1, w))
        pooled.append(s / cnt - u_pool[:, cols])
    pooled = jnp.concatenate(pooled, axis=-1).astype(BF16)
    y_pool = jnp.dot(pooled, wpool_ref[...], preferred_element_type=F32) * pscale_ref[...]
    npool_ref[:, 0:POOL_BUF - 1, :] = pst_ref[:, 1:POOL_BUF, :]
    npool_ref[:, POOL_BUF - 1, :] = u_pool

    xc = None
    for k in range(CONV_WIDTH - 1):
        term = cst_ref[:, k, :] * cw_ref[k:k + 1, :]
        xc = term if xc is None else xc + term
    xc = cb_ref[...] + (xc + u_lru * cw_ref[CONV_WIDTH - 1:CONV_WIDTH, :])
    nconv_ref[:, 0:CONV_WIDTH - 2, :] = cst_ref[:, 1:CONV_WIDTH - 1, :]
    nconv_ref[:, CONV_WIDTH - 2, :] = u_lru

    a, b = _gates(xc, wg_ref, ba_ref, bx_ref, lam_ref)
    hs = a * h0_ref[...] + b
    nh_ref[...] = hs
    y_lru = hs * jax.nn.gelu(u_gate)

    o_ref[...] = _mix_out(x, g_ref[...], y_pool, y_lru, wout_ref, lng_ref, lnb_ref)


def _mixer_sample_call(x2d, mod, state_pool, state_conv, state_h, wts, new_states, *, layer, n_sample, n_prompt):
    rows = x2d.shape[0]
    npool, nconv, nh = new_states
    n_in = 7 + len(wts)
    any_spec = pl.BlockSpec(memory_space=pl.ANY)
    return pl.pallas_call(
        _mixer_sample_kernel,
        out_shape=(jax.ShapeDtypeStruct((rows, D_MODEL), F32),
                   jax.ShapeDtypeStruct(npool.shape, F32),
                   jax.ShapeDtypeStruct(nconv.shape, F32),
                   jax.ShapeDtypeStruct(nh.shape, F32)),
        grid=(1,),
        in_specs=[pl.BlockSpec((rows, D_MODEL), lambda i: (0, 0))]
        + _mod_specs(layer, 1, True, n_sample, n_prompt)
        + [_layer_block(state_pool.shape, layer), _layer_block(state_conv.shape, layer),
           _layer_block(state_h.shape, layer)]
        + [_resident(w.shape, layer) for w in wts]
        + [any_spec, any_spec, any_spec],
        out_specs=(pl.BlockSpec((rows, D_MODEL), lambda i: (0, 0)),
                   _layer_block(npool.shape, layer), _layer_block(nconv.shape, layer), _layer_block(nh.shape, layer)),
        input_output_aliases={n_in: 1, n_in + 1: 2, n_in + 2: 3},
        compiler_params=pltpu.CompilerParams(
            dimension_semantics=("arbitrary",),
            vmem_limit_bytes=48 * MIB),
        name="mixer_s",
    )(x2d, mod, mod, mod, state_pool, state_conv, state_h, *wts, npool, nconv, nh)


def _block_diag(w):
    depth, groups, c, _ = w.shape
    eye = jnp.eye(groups, dtype=w.dtype)
    return jnp.einsum('lgij,gh->lgihj', w, eye).reshape(depth, groups * c, groups * c)


def kernel(x_prompt, x_sample, state_pool, state_conv, state_rglru, c_prompt, c_sample, w_ada, b_ada, ffn1_wgu, ffn1_wdown, w_in, w_out, w_pool, pool_scale, conv_w, conv_b, w_rg_a, b_rg_a, w_rg_x, b_rg_x, lru_lambda, ffn2_wgu, ffn2_wdown, ln_g, ln_b):
    n_prompt, seq, _ = x_prompt.shape
    n_sample = x_sample.shape[0]
    assert x_sample.shape[1] == 1 and n_sample % n_prompt == 0 and n_prompt == SUBLANES
    assert seq % MIXER_STEPS == 0 and (seq * n_prompt) % FFN_TILE == 0

    wgu1, wgu2 = ffn1_wgu.astype(BF16), ffn2_wgu.astype(BF16)
    wd1, wd2 = ffn1_wdown.astype(BF16), ffn2_wdown.astype(BF16)
    row = lambda v: v.reshape(DEPTH, 1, v.shape[-1])
    mixer_w = (
        w_in.astype(BF16), _block_diag(w_pool).astype(BF16), row(pool_scale), conv_w, row(conv_b),
        jnp.concatenate([_block_diag(w_rg_a), _block_diag(w_rg_x)], axis=-1).astype(BF16),
        row(b_rg_a), row(b_rg_x), row(lru_lambda), w_out.astype(BF16), ln_g, ln_b,
    )

    mod = _ada_call(jnp.concatenate([c_sample, c_prompt], axis=0), w_ada, b_ada)

    xp = jnp.swapaxes(x_prompt, 0, 1).reshape(seq * n_prompt, D_MODEL)
    xs = x_sample.reshape(n_sample, D_MODEL)
    new_p = (jnp.zeros((DEPTH, n_prompt, POOL_BUF, POOL_WIDTH), F32),
             jnp.zeros((DEPTH, n_prompt, CONV_WIDTH - 1, LRU_WIDTH), F32),
             jnp.zeros((DEPTH, n_prompt, LRU_WIDTH), F32))
    new_s = (jnp.zeros(state_pool.shape, F32), jnp.zeros(state_conv.shape, F32), jnp.zeros(state_rglru.shape, F32))

    common = dict(n_sample=n_sample, n_prompt=n_prompt)
    for l in range(DEPTH):
        ffn_p = functools.partial(_ffn_call, layer=l, is_sample=False, tm=FFN_TILE, **common)
        ffn_s = functools.partial(_ffn_call, layer=l, is_sample=True, tm=n_sample, **common)
        xp = ffn_p(xp, mod, wgu1, wd1, ln_g, ln_b, sub=0, ln_idx=0)
        xs = ffn_s(xs, mod, wgu1, wd1, ln_g, ln_b, sub=0, ln_idx=0)
        xp, *new_p = _mixer_prompt_call(xp, mod, mixer_w, new_p, layer=l, steps=MIXER_STEPS, **common)
        xs, *new_s = _mixer_sample_call(xs, mod, state_pool, state_conv, state_rglru, mixer_w, new_s,
                                        layer=l, **common)
        xp = ffn_p(xp, mod, wgu2, wd2, ln_g, ln_b, sub=2, ln_idx=2)
        xs = ffn_s(xs, mod, wgu2, wd2, ln_g, ln_b, sub=2, ln_idx=2)

    y_prompt = jnp.swapaxes(xp.reshape(seq, n_prompt, D_MODEL), 0, 1)
    return (y_prompt, xs.reshape(n_sample, 1, D_MODEL), *new_p, *new_s)
```

```python
import functools

import jax
import jax.numpy as jnp
from jax import lax
from jax.experimental import pallas as pl
from jax.experimental.pallas import tpu as pltpu

D_MODEL = 1024
DEPTH = 4
PAST_LEN = 16384
POOL_WIDTH = D_MODEL // 2
LRU_WIDTH = D_MODEL - POOL_WIDTH
IN_WIDTH = POOL_WIDTH + 2 * LRU_WIDTH
POOL_WINDOWS = (2, 4, 8, 16)
POOL_GROUP_DIM = POOL_WIDTH // len(POOL_WINDOWS)
POOL_BUF = max(POOL_WINDOWS) - 1
N_LRU_HEADS = 8
LRU_HEAD_DIM = LRU_WIDTH // N_LRU_HEADS
CONV_WIDTH = 4
LRU_C = 8.0
D_FF = 128 * int(round(8 * D_MODEL / 3 / 128))
N_MOD = 9
ALPHA = (2 * DEPTH) ** 0.25
LN_EPS = 1e-5

F32 = jnp.float32
BF16 = jnp.bfloat16

SUBLANES = 8
LANES = 128
MIB = 1024 * 1024

FFN_TILE = 512
FFN_SUBTILES = 2
MIXER_STEPS = 64
MIXER_SUBTILES = 2


def _layer_norm(y, g, b):
    mu = jnp.mean(y, axis=-1, keepdims=True)
    d = y - mu
    var = jnp.mean(d * d, axis=-1, keepdims=True)
    return d * lax.rsqrt(var + LN_EPS) * g + b


def _silu(x):
    return x * jax.nn.sigmoid(x)


def _softplus(x):
    return jnp.maximum(x, 0.0) + jnp.log1p(jnp.exp(-jnp.abs(x)))


def _per_batch(fn, x, *ms):
    rows, d = x.shape
    nb = ms[0].shape[0]
    if nb == rows:
        return fn(x, *ms)
    return fn(x.reshape(rows // nb, nb, d), *[m[None] for m in ms]).reshape(rows, d)


def _modulate(x, sh, sc):
    return _per_batch(lambda x_, sh_, sc_: x_ * (1.0 + sc_) + sh_, x, sh, sc)


def _gated_residual(x, g, y, gate_scale):
    return _per_batch(lambda y_, g_: (gate_scale * (1.0 + g_)) * y_, y, g) + ALPHA * x


def _ada_kernel(c_ref, w_ref, b_ref, o_ref):
    sc = _silu(c_ref[...]).astype(BF16)
    w = w_ref[...].astype(BF16)
    o_ref[...] = jnp.dot(sc, w, preferred_element_type=F32) + b_ref[...]


def _ada_call(c_all, w_ada, b_ada):
    n_rows = c_all.shape[0]
    n_out = w_ada.shape[-1]
    tn = n_out // 8
    return pl.pallas_call(
        _ada_kernel,
        out_shape=jax.ShapeDtypeStruct((DEPTH, n_rows, n_out), F32),
        grid=(DEPTH, n_out // tn),
        in_specs=[
            pl.BlockSpec((n_rows, D_MODEL), lambda l, j: (0, 0)),
            pl.BlockSpec((None, D_MODEL, tn), lambda l, j: (l, 0, j)),
            pl.BlockSpec((None, 1, tn), lambda l, j: (l, 0, j)),
        ],
        out_specs=pl.BlockSpec((None, n_rows, tn), lambda l, j: (l, 0, j)),
        compiler_params=pltpu.CompilerParams(
            dimension_semantics=("arbitrary", "arbitrary"),
            vmem_limit_bytes=32 * MIB),
        name="ada",
    )(c_all, w_ada, b_ada.reshape(DEPTH, 1, n_out))


def _ffn_kernel(x_ref, sh_ref, sc_ref, g_ref, wgu_ref, wd_ref, lng_ref, lnb_ref, o_ref,
                *, ln_idx, subtiles, batch_major_in, batch_major_out):
    sh, sc, g = sh_ref[...], sc_ref[...], g_ref[...]
    nb = sh.shape[0]
    rows = (x_ref.shape[0] * x_ref.shape[1] if batch_major_in else x_ref.shape[0]) // subtiles
    ts = rows // nb
    for s_i in range(subtiles):
        if batch_major_in:
            x = pltpu.einshape("btd->tbd", x_ref[:, s_i * ts:(s_i + 1) * ts, :]).reshape(rows, D_MODEL)
        else:
            x = x_ref[s_i * rows:(s_i + 1) * rows, :]
        h = _modulate(x, sh, sc).astype(BF16)
        gu = jnp.dot(h, wgu_ref[...], preferred_element_type=F32)
        act = (_silu(gu[:, :D_FF]) * gu[:, D_FF:]).astype(BF16)
        acc = jnp.dot(act, wd_ref[...], preferred_element_type=F32)
        y = _gated_residual(x, g, acc, 0.5)
        y = _layer_norm(y, lng_ref[ln_idx:ln_idx + 1, :], lnb_ref[ln_idx:ln_idx + 1, :])
        if batch_major_out:
            o_ref[:, s_i * ts:(s_i + 1) * ts, :] = pltpu.einshape("tbd->btd", y.reshape(ts, nb, D_MODEL))
        else:
            o_ref[s_i * rows:(s_i + 1) * rows, :] = y


def _mod_specs(layer, sub, is_sample, n_sample, n_prompt):
    rows, block = (n_sample, 0) if is_sample else (n_prompt, n_sample // n_prompt)
    return [pl.BlockSpec((None, rows, D_MODEL), functools.partial(lambda i, k: (layer, block, k), k=3 * sub + t))
            for t in range(3)]


def _resident(shape, layer):
    nd = len(shape)
    return pl.BlockSpec((None,) + tuple(shape[1:]), lambda i: (layer,) + (0,) * (nd - 1),
                        pipeline_mode=pl.Buffered(1))


def _layer_block(shape, layer):
    nd = len(shape)
    return pl.BlockSpec((None,) + tuple(shape[1:]), lambda i: (layer,) + (0,) * (nd - 1))


def _ffn_call(x, mod, wgu, wd, ln_g, ln_b, *, layer, sub, ln_idx, is_sample, tm, n_sample, n_prompt,
              batch_major_in=False, batch_major_out=False):
    rows = x.shape[0] * x.shape[1] if batch_major_in else x.shape[0]
    steps = tm // n_prompt
    major_spec = pl.BlockSpec((n_prompt, steps, D_MODEL), lambda i: (0, i, 0))
    rows_spec = pl.BlockSpec((tm, D_MODEL), lambda i: (i, 0))
    out_shape = (n_prompt, rows // n_prompt, D_MODEL) if batch_major_out else (rows, D_MODEL)
    kern = functools.partial(_ffn_kernel, ln_idx=ln_idx, subtiles=1 if is_sample else FFN_SUBTILES,
                             batch_major_in=batch_major_in, batch_major_out=batch_major_out)
    return pl.pallas_call(
        kern,
        out_shape=jax.ShapeDtypeStruct(out_shape, F32),
        grid=(rows // tm,),
        in_specs=[major_spec if batch_major_in else rows_spec]
        + _mod_specs(layer, sub, is_sample, n_sample, n_prompt)
        + [_resident(wgu.shape, layer), _resident(wd.shape, layer),
           _resident(ln_g.shape, layer), _resident(ln_b.shape, layer)],
        out_specs=major_spec if batch_major_out else rows_spec,
        compiler_params=pltpu.CompilerParams(
            dimension_semantics=("arbitrary",),
            vmem_limit_bytes=52 * MIB),
        name=f"ffn{sub}_{'s' if is_sample else 'p'}",
    )(x, mod, mod, mod, wgu, wd, ln_g, ln_b)


def _gates(xc, wg_ref, ba_ref, bx_ref, lam_ref):
    gates = jnp.dot(xc.astype(BF16), wg_ref[...], preferred_element_type=F32)
    r = jax.nn.sigmoid(gates[:, :LRU_WIDTH] + ba_ref[...])
    i = jax.nn.sigmoid(gates[:, LRU_WIDTH:] + bx_ref[...])
    log_a = (-LRU_C * r) * _softplus(-lam_ref[...])
    a = jnp.exp(log_a)
    t = jnp.tanh(log_a)
    mult = jnp.sqrt((-2.0 * t) / (1.0 - t))
    return a, mult * (i * xc)


def _mix_out(x, g, y_pool, y_lru, wout_ref, lng_ref, lnb_ref):
    mix = jnp.dot(y_pool.astype(BF16), wout_ref[:POOL_WIDTH, :], preferred_element_type=F32)
    mix = mix + jnp.dot(y_lru.astype(BF16), wout_ref[POOL_WIDTH:, :], preferred_element_type=F32)
    y = _gated_residual(x, g, mix, 1.0)
    return _layer_norm(y, lng_ref[1:2, :], lnb_ref[1:2, :])


def _mixer_prompt_kernel(x_ref, sh_ref, sc_ref, g_ref, win_ref, wpool_ref, pscale_ref, cw_ref, cb_ref,
                         wg_ref, ba_ref, bx_ref, lam_ref, wout_ref, lng_ref, lnb_ref,
                         npool_in, nconv_in, nh_in,
                         o_ref, npool_ref, nconv_ref, nh_ref,
                         zbuf, cbuf, hst, hsbuf, *, steps, nb, subtiles):
    del npool_in, nconv_in, nh_in
    pid = pl.program_id(0)
    tm = steps * nb
    pool_hist = POOL_BUF * nb
    conv_hist = (CONV_WIDTH - 1) * nb

    @pl.when(pid == 0)
    def _new_sequences():
        zbuf[0:pool_hist, :] = jnp.zeros((pool_hist, POOL_WIDTH), F32)
        cbuf[0:conv_hist, :] = jnp.zeros((conv_hist, LRU_WIDTH), F32)
        hst[...] = jnp.zeros_like(hst)

    sh, sc, g = sh_ref[...], sc_ref[...], g_ref[...]
    hm, hsteps = tm // subtiles, steps // subtiles
    t_idx = lax.shift_right_logical(lax.broadcasted_iota(jnp.int32, (hm, POOL_GROUP_DIM), 0), nb.bit_length() - 1)
    subs = range(subtiles)
    rows = [slice(s_i * hm, (s_i + 1) * hm) for s_i in subs]

    xs = [x_ref[rs, :] for rs in rows]
    projs = [jnp.dot(_modulate(x, sh, sc).astype(BF16), win_ref[...], preferred_element_type=F32) for x in xs]
    u_gates = [p[:, POOL_WIDTH + LRU_WIDTH:] for p in projs]

    y_pools, coeffs = [], []
    for s_i, proj in zip(subs, projs):
        r0 = s_i * hm
        u_pool = proj[:, :POOL_WIDTH]
        u_lru = proj[:, POOL_WIDTH:POOL_WIDTH + LRU_WIDTH]

        zbuf[pool_hist + r0:pool_hist + r0 + hm, :] = u_pool
        pos1 = (pid * steps + s_i * hsteps + 1) + t_idx
        cur = zbuf[r0:r0 + pool_hist + hm, :]
        first, w = -POOL_BUF, 1
        pooled = []
        for gi, wnd in enumerate(POOL_WINDOWS):
            assert wnd == 2 * w
            n = cur.shape[0]
            cur = cur[w * nb:, :] + cur[:n - w * nb, :]
            first, w = first + w, wnd
            cols = slice(gi * POOL_GROUP_DIM, (gi + 1) * POOL_GROUP_DIM)
            s = cur[-first * nb:, :POOL_GROUP_DIM]
            pooled.append(s / jnp.minimum(pos1, wnd).astype(F32) - u_pool[:, cols])
            cur = cur[:, POOL_GROUP_DIM:]
        assert first == 0
        pooled = jnp.concatenate(pooled, axis=-1).astype(BF16)
        y_pools.append(jnp.dot(pooled, wpool_ref[...], preferred_element_type=F32) * pscale_ref[...])

        cbuf[conv_hist + r0:conv_hist + r0 + hm, :] = u_lru
        xc = None
        for k in range(CONV_WIDTH):
            term = cbuf[r0 + k * nb:r0 + k * nb + hm, :] * cw_ref[k:k + 1, :]
            xc = term if xc is None else xc + term
        coeffs.append(_gates(cb_ref[...] + xc, wg_ref, ba_ref, bx_ref, lam_ref))

    hcur = hst[...]
    for s_i, (a, b) in zip(subs, coeffs):
        r0 = s_i * hm
        for t in range(hsteps):
            hcur = a[t * nb:(t + 1) * nb, :] * hcur + b[t * nb:(t + 1) * nb, :]
            hsbuf[r0 + t * nb:r0 + (t + 1) * nb, :] = hcur
    hst[...] = hcur

    for rs, x, y_pool, u_gate in zip(rows, xs, y_pools, u_gates):
        y_lru = hsbuf[rs, :] * jax.nn.gelu(u_gate)
        o_ref[rs, :] = _mix_out(x, g, y_pool, y_lru, wout_ref, lng_ref, lnb_ref)

    @pl.when(pid == pl.num_programs(0) - 1)
    def _final_state():
        for r in range(POOL_BUF):
            npool_ref[:, r, :] = zbuf[tm + r * nb:tm + (r + 1) * nb, :]
        for r in range(CONV_WIDTH - 1):
            nconv_ref[:, r, :] = cbuf[tm + r * nb:tm + (r + 1) * nb, :]
        nh_ref[...] = hcur

    zbuf[0:pool_hist, :] = zbuf[tm:tm + pool_hist, :]
    cbuf[0:conv_hist, :] = cbuf[tm:tm + conv_hist, :]


def _mixer_prompt_call(x2d, mod, wts, new_states, *, layer, steps, n_sample, n_prompt):
    rows = x2d.shape[0]
    tm = steps * n_prompt
    kern = functools.partial(_mixer_prompt_kernel, steps=steps, nb=n_prompt, subtiles=MIXER_SUBTILES)
    npool, nconv, nh = new_states
    n_in = 4 + len(wts)
    any_spec = pl.BlockSpec(memory_space=pl.ANY)
    return pl.pallas_call(
        kern,
        out_shape=(jax.ShapeDtypeStruct((rows, D_MODEL), F32),
                   jax.ShapeDtypeStruct(npool.shape, F32),
                   jax.ShapeDtypeStruct(nconv.shape, F32),
                   jax.ShapeDtypeStruct(nh.shape, F32)),
        grid=(rows // tm,),
        in_specs=[pl.BlockSpec((tm, D_MODEL), lambda i: (i, 0))]
        + _mod_specs(layer, 1, False, n_sample, n_prompt)
        + [_resident(w.shape, layer) for w in wts]
        + [any_spec, any_spec, any_spec],
        out_specs=(pl.BlockSpec((tm, D_MODEL), lambda i: (i, 0)),
                   _layer_block(npool.shape, layer), _layer_block(nconv.shape, layer), _layer_block(nh.shape, layer)),
        scratch_shapes=[
            pltpu.VMEM((POOL_BUF * n_prompt + tm, POOL_WIDTH), F32),
            pltpu.VMEM(((CONV_WIDTH - 1) * n_prompt + tm, LRU_WIDTH), F32),
            pltpu.VMEM((n_prompt, LRU_WIDTH), F32),
            pltpu.VMEM((tm, LRU_WIDTH), F32),
        ],
        input_output_aliases={n_in: 1, n_in + 1: 2, n_in + 2: 3},
        compiler_params=pltpu.CompilerParams(
            dimension_semantics=("arbitrary",),
            vmem_limit_bytes=48 * MIB),
        name="mixer_p",
    )(x2d, mod, mod, mod, *wts, npool, nconv, nh)


def _mixer_sample_kernel(x_ref, sh_ref, sc_ref, g_ref, pst_ref, cst_ref, h0_ref,
                         win_ref, wpool_ref, pscale_ref, cw_ref, cb_ref,
                         wg_ref, ba_ref, bx_ref, lam_ref, wout_ref, lng_ref, lnb_ref,
                         npool_in, nconv_in, nh_in,
                         o_ref, npool_ref, nconv_ref, nh_ref):
    del npool_in, nconv_in, nh_in
    x = x_ref[...]
    h = _modulate(x, sh_ref[...], sc_ref[...]).astype(BF16)
    proj = jnp.dot(h, win_ref[...], preferred_element_type=F32)
    u_pool = proj[:, :POOL_WIDTH]
    u_lru = proj[:, POOL_WIDTH:POOL_WIDTH + LRU_WIDTH]
    u_gate = proj[:, POOL_WIDTH + LRU_WIDTH:]

    pooled = []
    for gi, w in enumerate(POOL_WINDOWS):
        cols = slice(gi * POOL_GROUP_DIM, (gi + 1) * POOL_GROUP_DIM)
        s = u_pool[:, cols]
        for back in range(1, w):
            s = s + pst_ref[:, POOL_BUF - back, cols]
        cnt = float(min(PAST_LEN + 1, w))
        pooled.append(s / cnt - u_pool[:, cols])
    pooled = jnp.concatenate(pooled, axis=-1).astype(BF16)
    y_pool = jnp.dot(pooled, wpool_ref[...], preferred_element_type=F32) * pscale_ref[...]
    npool_ref[:, 0:POOL_BUF - 1, :] = pst_ref[:, 1:POOL_BUF, :]
    npool_ref[:, POOL_BUF - 1, :] = u_pool

    xc = None
    for k in range(CONV_WIDTH - 1):
        term = cst_ref[:, k, :] * cw_ref[k:k + 1, :]
        xc = term if xc is None else xc + term
    xc = cb_ref[...] + (xc + u_lru * cw_ref[CONV_WIDTH - 1:CONV_WIDTH, :])
    nconv_ref[:, 0:CONV_WIDTH - 2, :] = cst_ref[:, 1:CONV_WIDTH - 1, :]
    nconv_ref[:, CONV_WIDTH - 2, :] = u_lru

    a, b = _gates(xc, wg_ref, ba_ref, bx_ref, lam_ref)
    hs = a * h0_ref[...] + b
    nh_ref[...] = hs
    y_lru = hs * jax.nn.gelu(u_gate)

    o_ref[...] = _mix_out(x, g_ref[...], y_pool, y_lru, wout_ref, lng_ref, lnb_ref)


def _mixer_sample_call(x2d, mod, state_pool, state_conv, state_h, wts, new_states, *, layer, n_sample, n_prompt):
    rows = x2d.shape[0]
    npool, nconv, nh = new_states
    n_in = 7 + len(wts)
    any_spec = pl.BlockSpec(memory_space=pl.ANY)
    return pl.pallas_call(
        _mixer_sample_kernel,
        out_shape=(jax.ShapeDtypeStruct((rows, D_MODEL), F32),
                   jax.ShapeDtypeStruct(npool.shape, F32),
                   jax.ShapeDtypeStruct(nconv.shape, F32),
                   jax.ShapeDtypeStruct(nh.shape, F32)),
        grid=(1,),
        in_specs=[pl.BlockSpec((rows, D_MODEL), lambda i: (0, 0))]
        + _mod_specs(layer, 1, True, n_sample, n_prompt)
        + [_layer_block(state_pool.shape, layer), _layer_block(state_conv.shape, layer),
           _layer_block(state_h.shape, layer)]
        + [_resident(w.shape, layer) for w in wts]
        + [any_spec, any_spec, any_spec],
        out_specs=(pl.BlockSpec((rows, D_MODEL), lambda i: (0, 0)),
                   _layer_block(npool.shape, layer), _layer_block(nconv.shape, layer), _layer_block(nh.shape, layer)),
        input_output_aliases={n_in: 1, n_in + 1: 2, n_in + 2: 3},
        compiler_params=pltpu.CompilerParams(
            dimension_semantics=("arbitrary",),
            vmem_limit_bytes=48 * MIB),
        name="mixer_s",
    )(x2d, mod, mod, mod, state_pool, state_conv, state_h, *wts, npool, nconv, nh)


def _block_diag(w):
    depth, groups, c, _ = w.shape
    eye = jnp.eye(groups, dtype=w.dtype)
    return jnp.einsum('lgij,gh->lgihj', w, eye).reshape(depth, groups * c, groups * c)


def kernel(x_prompt, x_sample, state_pool, state_conv, state_rglru, c_prompt, c_sample, w_ada, b_ada, ffn1_wgu, ffn1_wdown, w_in, w_out, w_pool, pool_scale, conv_w, conv_b, w_rg_a, b_rg_a, w_rg_x, b_rg_x, lru_lambda, ffn2_wgu, ffn2_wdown, ln_g, ln_b):
    n_prompt, seq, _ = x_prompt.shape
    n_sample = x_sample.shape[0]
    assert x_sample.shape[1] == 1 and n_sample % n_prompt == 0 and n_prompt == SUBLANES
    assert seq % MIXER_STEPS == 0 and (seq * n_prompt) % FFN_TILE == 0

    wgu1, wgu2 = ffn1_wgu.astype(BF16), ffn2_wgu.astype(BF16)
    wd1, wd2 = ffn1_wdown.astype(BF16), ffn2_wdown.astype(BF16)
    row = lambda v: v.reshape(DEPTH, 1, v.shape[-1])
    mixer_w = (
        w_in.astype(BF16), _block_diag(w_pool).astype(BF16), row(pool_scale), conv_w, row(conv_b),
        jnp.concatenate([_block_diag(w_rg_a), _block_diag(w_rg_x)], axis=-1).astype(BF16),
        row(b_rg_a), row(b_rg_x), row(lru_lambda), w_out.astype(BF16), ln_g, ln_b,
    )

    mod = _ada_call(jnp.concatenate([c_sample, c_prompt], axis=0), w_ada, b_ada)

    xp = x_prompt
    xs = x_sample.reshape(n_sample, D_MODEL)
    new_p = (jnp.zeros((DEPTH, n_prompt, POOL_BUF, POOL_WIDTH), F32),
             jnp.zeros((DEPTH, n_prompt, CONV_WIDTH - 1, LRU_WIDTH), F32),
             jnp.zeros((DEPTH, n_prompt, LRU_WIDTH), F32))
    new_s = (jnp.zeros(state_pool.shape, F32), jnp.zeros(state_conv.shape, F32), jnp.zeros(state_rglru.shape, F32))

    common = dict(n_sample=n_sample, n_prompt=n_prompt)
    for l in range(DEPTH):
        ffn_p = functools.partial(_ffn_call, layer=l, is_sample=False, tm=FFN_TILE, **common)
        ffn_s = functools.partial(_ffn_call, layer=l, is_sample=True, tm=n_sample, **common)
        xp = ffn_p(xp, mod, wgu1, wd1, ln_g, ln_b, sub=0, ln_idx=0, batch_major_in=(l == 0))
        xs = ffn_s(xs, mod, wgu1, wd1, ln_g, ln_b, sub=0, ln_idx=0)
        xp, *new_p = _mixer_prompt_call(xp, mod, mixer_w, new_p, layer=l, steps=MIXER_STEPS, **common)
        xs, *new_s = _mixer_sample_call(xs, mod, state_pool, state_conv, state_rglru, mixer_w, new_s,
                                        layer=l, **common)
        xp = ffn_p(xp, mod, wgu2, wd2, ln_g, ln_b, sub=2, ln_idx=2, batch_major_out=(l == DEPTH - 1))
        xs = ffn_s(xs, mod, wgu2, wd2, ln_g, ln_b, sub=2, ln_idx=2)

    return (xp, xs.reshape(n_sample, 1, D_MODEL), *new_p, *new_s)
```

```python
import functools

import jax
import jax.numpy as jnp
from jax import lax
from jax.experimental import pallas as pl
from jax.experimental.pallas import tpu as pltpu

D_MODEL = 1024
DEPTH = 4
PAST_LEN = 16384
POOL_WIDTH = D_MODEL // 2
LRU_WIDTH = D_MODEL - POOL_WIDTH
IN_WIDTH = POOL_WIDTH + 2 * LRU_WIDTH
POOL_WINDOWS = (2, 4, 8, 16)
POOL_GROUP_DIM = POOL_WIDTH // len(POOL_WINDOWS)
POOL_BUF = max(POOL_WINDOWS) - 1
N_LRU_HEADS = 8
LRU_HEAD_DIM = LRU_WIDTH // N_LRU_HEADS
CONV_WIDTH = 4
LRU_C = 8.0
D_FF = 128 * int(round(8 * D_MODEL / 3 / 128))
N_MOD = 9
ALPHA = (2 * DEPTH) ** 0.25
LN_EPS = 1e-5

F32 = jnp.float32
BF16 = jnp.bfloat16

SUBLANES = 8
LANES = 128
MIB = 1024 * 1024

FFN_TILE = 1024
FFN_SUBTILES = 4
MIXER_STEPS = 128
MIXER_SUBTILES = 4


def _layer_norm(y, g, b):
    mu = jnp.mean(y, axis=-1, keepdims=True)
    d = y - mu
    var = jnp.mean(d * d, axis=-1, keepdims=True)
    return d * lax.rsqrt(var + LN_EPS) * g + b


def _silu(x):
    return x * jax.nn.sigmoid(x)


def _softplus(x):
    return jnp.maximum(x, 0.0) + jnp.log1p(jnp.exp(-jnp.abs(x)))


def _per_batch(fn, x, *ms):
    rows, d = x.shape
    nb = ms[0].shape[0]
    if nb == rows:
        return fn(x, *ms)
    return fn(x.reshape(rows // nb, nb, d), *[m[None] for m in ms]).reshape(rows, d)


def _modulate(x, sh, sc):
    return _per_batch(lambda x_, sh_, sc_: x_ * (1.0 + sc_) + sh_, x, sh, sc)


def _gated_residual(x, g, y, gate_scale):
    return _per_batch(lambda y_, g_: (gate_scale * (1.0 + g_)) * y_, y, g) + ALPHA * x


def _ada_kernel(c_ref, w_ref, b_ref, o_ref):
    sc = _silu(c_ref[...]).astype(BF16)
    w = w_ref[...].astype(BF16)
    o_ref[...] = jnp.dot(sc, w, preferred_element_type=F32) + b_ref[...]


def _ada_call(c_all, w_ada, b_ada):
    n_rows = c_all.shape[0]
    n_out = w_ada.shape[-1]
    tn = n_out // 8
    return pl.pallas_call(
        _ada_kernel,
        out_shape=jax.ShapeDtypeStruct((DEPTH, n_rows, n_out), F32),
        grid=(DEPTH, n_out // tn),
        in_specs=[
            pl.BlockSpec((n_rows, D_MODEL), lambda l, j: (0, 0)),
            pl.BlockSpec((None, D_MODEL, tn), lambda l, j: (l, 0, j)),
            pl.BlockSpec((None, 1, tn), lambda l, j: (l, 0, j)),
        ],
        out_specs=pl.BlockSpec((None, n_rows, tn), lambda l, j: (l, 0, j)),
        compiler_params=pltpu.CompilerParams(
            dimension_semantics=("arbitrary", "arbitrary"),
            vmem_limit_bytes=32 * MIB),
        name="ada",
    )(c_all, w_ada, b_ada.reshape(DEPTH, 1, n_out))


def _ffn_kernel(x_ref, sh_ref, sc_ref, g_ref, wgu_ref, wd_ref, lng_ref, lnb_ref, o_ref, *lbuf,
                ln_idx, subtiles, batch_major_in, batch_major_out):
    sh, sc, g = sh_ref[...], sc_ref[...], g_ref[...]
    nb = sh.shape[0]
    rows = (x_ref.shape[0] * x_ref.shape[1] if batch_major_in else x_ref.shape[0]) // subtiles
    ts = rows // nb
    for s_i in range(subtiles):
        steps = slice(s_i * ts, (s_i + 1) * ts)
        if batch_major_in:
            for b in range(nb):
                lbuf[0][steps, b, :] = x_ref[b, steps, :]
            x = lbuf[0][steps, :, :].reshape(rows, D_MODEL)
        else:
            x = x_ref[s_i * rows:(s_i + 1) * rows, :]
        h = _modulate(x, sh, sc).astype(BF16)
        gu = jnp.dot(h, wgu_ref[...], preferred_element_type=F32)
        act = (_silu(gu[:, :D_FF]) * gu[:, D_FF:]).astype(BF16)
        acc = jnp.dot(act, wd_ref[...], preferred_element_type=F32)
        y = _gated_residual(x, g, acc, 0.5)
        y = _layer_norm(y, lng_ref[ln_idx:ln_idx + 1, :], lnb_ref[ln_idx:ln_idx + 1, :])
        if batch_major_out:
            lbuf[0][steps, :, :] = y.reshape(ts, nb, D_MODEL)
            for b in range(nb):
                o_ref[b, steps, :] = lbuf[0][steps, b, :]
        else:
            o_ref[s_i * rows:(s_i + 1) * rows, :] = y


def _mod_specs(layer, sub, is_sample, n_sample, n_prompt):
    rows, block = (n_sample, 0) if is_sample else (n_prompt, n_sample // n_prompt)
    return [pl.BlockSpec((None, rows, D_MODEL), functools.partial(lambda i, k: (layer, block, k), k=3 * sub + t))
            for t in range(3)]


def _resident(shape, layer):
    nd = len(shape)
    return pl.BlockSpec((None,) + tuple(shape[1:]), lambda i: (layer,) + (0,) * (nd - 1),
                        pipeline_mode=pl.Buffered(1))


def _layer_block(shape, layer):
    nd = len(shape)
    return pl.BlockSpec((None,) + tuple(shape[1:]), lambda i: (layer,) + (0,) * (nd - 1))


def _ffn_call(x, mod, wgu, wd, ln_g, ln_b, *, layer, sub, ln_idx, is_sample, tm, n_sample, n_prompt,
              batch_major_in=False, batch_major_out=False):
    rows = x.shape[0] * x.shape[1] if batch_major_in else x.shape[0]
    steps = tm // n_prompt
    major_spec = pl.BlockSpec((n_prompt, steps, D_MODEL), lambda i: (0, i, 0))
    rows_spec = pl.BlockSpec((tm, D_MODEL), lambda i: (i, 0))
    out_shape = (n_prompt, rows // n_prompt, D_MODEL) if batch_major_out else (rows, D_MODEL)
    kern = functools.partial(_ffn_kernel, ln_idx=ln_idx, subtiles=1 if is_sample else FFN_SUBTILES,
                             batch_major_in=batch_major_in, batch_major_out=batch_major_out)
    return pl.pallas_call(
        kern,
        out_shape=jax.ShapeDtypeStruct(out_shape, F32),
        grid=(rows // tm,),
        in_specs=[major_spec if batch_major_in else rows_spec]
        + _mod_specs(layer, sub, is_sample, n_sample, n_prompt)
        + [_resident(wgu.shape, layer), _resident(wd.shape, layer),
           _resident(ln_g.shape, layer), _resident(ln_b.shape, layer)],
        out_specs=major_spec if batch_major_out else rows_spec,
        scratch_shapes=[pltpu.VMEM((steps, n_prompt, D_MODEL), F32)] if batch_major_in or batch_major_out else [],
        compiler_params=pltpu.CompilerParams(
            dimension_semantics=("arbitrary",),
            vmem_limit_bytes=56 * MIB),
        name=f"ffn{sub}_{'s' if is_sample else 'p'}",
    )(x, mod, mod, mod, wgu, wd, ln_g, ln_b)


def _gates(xc, wg_ref, ba_ref, bx_ref, lam_ref):
    gates = jnp.dot(xc.astype(BF16), wg_ref[...], preferred_element_type=F32)
    r = jax.nn.sigmoid(gates[:, :LRU_WIDTH] + ba_ref[...])
    i = jax.nn.sigmoid(gates[:, LRU_WIDTH:] + bx_ref[...])
    log_a = (-LRU_C * r) * _softplus(-lam_ref[...])
    a = jnp.exp(log_a)
    t = jnp.tanh(log_a)
    mult = jnp.sqrt((-2.0 * t) / (1.0 - t))
    return a, mult * (i * xc)


def _mix_out(x, g, y_pool, y_lru, wout_ref, lng_ref, lnb_ref):
    mix = jnp.dot(y_pool.astype(BF16), wout_ref[:POOL_WIDTH, :], preferred_element_type=F32)
    mix = mix + jnp.dot(y_lru.astype(BF16), wout_ref[POOL_WIDTH:, :], preferred_element_type=F32)
    y = _gated_residual(x, g, mix, 1.0)
    return _layer_norm(y, lng_ref[1:2, :], lnb_ref[1:2, :])


def _mixer_prompt_kernel(x_ref, sh_ref, sc_ref, g_ref, win_ref, wpool_ref, pscale_ref, cw_ref, cb_ref,
                         wg_ref, ba_ref, bx_ref, lam_ref, wout_ref, lng_ref, lnb_ref,
                         npool_in, nconv_in, nh_in,
                         o_ref, npool_ref, nconv_ref, nh_ref,
                         zbuf, cbuf, hst, hsbuf, *, steps, nb, subtiles):
    del npool_in, nconv_in, nh_in
    pid = pl.program_id(0)
    tm = steps * nb
    pool_hist = POOL_BUF * nb
    conv_hist = (CONV_WIDTH - 1) * nb

    @pl.when(pid == 0)
    def _new_sequences():
        zbuf[0:pool_hist, :] = jnp.zeros((pool_hist, POOL_WIDTH), F32)
        cbuf[0:conv_hist, :] = jnp.zeros((conv_hist, LRU_WIDTH), F32)
        hst[...] = jnp.zeros_like(hst)

    sh, sc, g = sh_ref[...], sc_ref[...], g_ref[...]
    hm, hsteps = tm // subtiles, steps // subtiles
    t_idx = lax.shift_right_logical(lax.broadcasted_iota(jnp.int32, (hm, POOL_GROUP_DIM), 0), nb.bit_length() - 1)
    subs = range(subtiles)
    rows = [slice(s_i * hm, (s_i + 1) * hm) for s_i in subs]

    xs = [x_ref[rs, :] for rs in rows]
    projs = [jnp.dot(_modulate(x, sh, sc).astype(BF16), win_ref[...], preferred_element_type=F32) for x in xs]
    u_gates = [p[:, POOL_WIDTH + LRU_WIDTH:] for p in projs]

    y_pools, coeffs = [], []
    for s_i, proj in zip(subs, projs):
        r0 = s_i * hm
        u_pool = proj[:, :POOL_WIDTH]
        u_lru = proj[:, POOL_WIDTH:POOL_WIDTH + LRU_WIDTH]

        zbuf[pool_hist + r0:pool_hist + r0 + hm, :] = u_pool
        pos1 = (pid * steps + s_i * hsteps + 1) + t_idx
        cur = zbuf[r0:r0 + pool_hist + hm, :]
        first, w = -POOL_BUF, 1
        pooled = []
        for gi, wnd in enumerate(POOL_WINDOWS):
            assert wnd == 2 * w
            n = cur.shape[0]
            cur = cur[w * nb:, :] + cur[:n - w * nb, :]
            first, w = first + w, wnd
            cols = slice(gi * POOL_GROUP_DIM, (gi + 1) * POOL_GROUP_DIM)
            s = cur[-first * nb:, :POOL_GROUP_DIM]
            pooled.append(s / jnp.minimum(pos1, wnd).astype(F32) - u_pool[:, cols])
            cur = cur[:, POOL_GROUP_DIM:]
        assert first == 0
        pooled = jnp.concatenate(pooled, axis=-1).astype(BF16)
        y_pools.append(jnp.dot(pooled, wpool_ref[...], preferred_element_type=F32) * pscale_ref[...])

        cbuf[conv_hist + r0:conv_hist + r0 + hm, :] = u_lru
        xc = None
        for k in range(CONV_WIDTH):
            term = cbuf[r0 + k * nb:r0 + k * nb + hm, :] * cw_ref[k:k + 1, :]
            xc = term if xc is None else xc + term
        coeffs.append(_gates(cb_ref[...] + xc, wg_ref, ba_ref, bx_ref, lam_ref))

    hcur = hst[...]
    for s_i, (a, b) in zip(subs, coeffs):
        r0 = s_i * hm
        for t in range(hsteps):
            hcur = a[t * nb:(t + 1) * nb, :] * hcur + b[t * nb:(t + 1) * nb, :]
            hsbuf[r0 + t * nb:r0 + (t + 1) * nb, :] = hcur
    hst[...] = hcur

    for rs, x, y_pool, u_gate in zip(rows, xs, y_pools, u_gates):
        y_lru = hsbuf[rs, :] * jax.nn.gelu(u_gate)
        o_ref[rs, :] = _mix_out(x, g, y_pool, y_lru, wout_ref, lng_ref, lnb_ref)

    @pl.when(pid == pl.num_programs(0) - 1)
    def _final_state():
        for r in range(POOL_BUF):
            npool_ref[:, r, :] = zbuf[tm + r * nb:tm + (r + 1) * nb, :]
        for r in range(CONV_WIDTH - 1):
            nconv_ref[:, r, :] = cbuf[tm + r * nb:tm + (r + 1) * nb, :]
        nh_ref[...] = hcur

    zbuf[0:pool_hist, :] = zbuf[tm:tm + pool_hist, :]
    cbuf[0:conv_hist, :] = cbuf[tm:tm + conv_hist, :]


def _mixer_prompt_call(x2d, mod, wts, new_states, *, layer, steps, n_sample, n_prompt):
    rows = x2d.shape[0]
    tm = steps * n_prompt
    kern = functools.partial(_mixer_prompt_kernel, steps=steps, nb=n_prompt, subtiles=MIXER_SUBTILES)
    npool, nconv, nh = new_states
    n_in = 4 + len(wts)
    any_spec = pl.BlockSpec(memory_space=pl.ANY)
    return pl.pallas_call(
        kern,
        out_shape=(jax.ShapeDtypeStruct((rows, D_MODEL), F32),
                   jax.ShapeDtypeStruct(npool.shape, F32),
                   jax.ShapeDtypeStruct(nconv.shape, F32),
                   jax.ShapeDtypeStruct(nh.shape, F32)),
        grid=(rows // tm,),
        in_specs=[pl.BlockSpec((tm, D_MODEL), lambda i: (i, 0))]
        + _mod_specs(layer, 1, False, n_sample, n_prompt)
        + [_resident(w.shape, layer) for w in wts]
        + [any_spec, any_spec, any_spec],
        out_specs=(pl.BlockSpec((tm, D_MODEL), lambda i: (i, 0)),
                   _layer_block(npool.shape, layer), _layer_block(nconv.shape, layer), _layer_block(nh.shape, layer)),
        scratch_shapes=[
            pltpu.VMEM((POOL_BUF * n_prompt + tm, POOL_WIDTH), F32),
            pltpu.VMEM(((CONV_WIDTH - 1) * n_prompt + tm, LRU_WIDTH), F32),
            pltpu.VMEM((n_prompt, LRU_WIDTH), F32),
            pltpu.VMEM((tm, LRU_WIDTH), F32),
        ],
        input_output_aliases={n_in: 1, n_in + 1: 2, n_in + 2: 3},
        compiler_params=pltpu.CompilerParams(
            dimension_semantics=("arbitrary",),
            vmem_limit_bytes=48 * MIB),
        name="mixer_p",
    )(x2d, mod, mod, mod, *wts, npool, nconv, nh)


def _mixer_sample_kernel(x_ref, sh_ref, sc_ref, g_ref, pst_ref, cst_ref, h0_ref,
                         win_ref, wpool_ref, pscale_ref, cw_ref, cb_ref,
                         wg_ref, ba_ref, bx_ref, lam_ref, wout_ref, lng_ref, lnb_ref,
                         npool_in, nconv_in, nh_in,
                         o_ref, npool_ref, nconv_ref, nh_ref):
    del npool_in, nconv_in, nh_in
    x = x_ref[...]
    h = _modulate(x, sh_ref[...], sc_ref[...]).astype(BF16)
    proj = jnp.dot(h, win_ref[...], preferred_element_type=F32)
    u_pool = proj[:, :POOL_WIDTH]
    u_lru = proj[:, POOL_WIDTH:POOL_WIDTH + LRU_WIDTH]
    u_gate = proj[:, POOL_WIDTH + LRU_WIDTH:]

    pooled = []
    for gi, w in enumerate(POOL_WINDOWS):
        cols = slice(gi * POOL_GROUP_DIM, (gi + 1) * POOL_GROUP_DIM)
        s = u_pool[:, cols]
        for back in range(1, w):
            s = s + pst_ref[:, POOL_BUF - back, cols]
        cnt = float(min(PAST_LEN + 1, w))
        pooled.append(s / cnt - u_pool[:, cols])
    pooled = jnp.concatenate(pooled, axis=-1).astype(BF16)
    y_pool = jnp.dot(pooled, wpool_ref[...], preferred_element_type=F32) * pscale_ref[...]
    npool_ref[:, 0:POOL_BUF - 1, :] = pst_ref[:, 1:POOL_BUF, :]
    npool_ref[:, POOL_BUF - 1, :] = u_pool

    xc = None
    for k in range(CONV_WIDTH - 1):
        term = cst_ref[:, k, :] * cw_ref[k:k + 1, :]
        xc = term if xc is None else xc + term
    xc = cb_ref[...] + (xc + u_lru * cw_ref[CONV_WIDTH - 1:CONV_WIDTH, :])
    nconv_ref[:, 0:CONV_WIDTH - 2, :] = cst_ref[:, 1:CONV_WIDTH - 1, :]
    nconv_ref[:, CONV_WIDTH - 2, :] = u_lru

    a, b = _gates(xc, wg_ref, ba_ref, bx_ref, lam_ref)
    hs = a * h0_ref[...] + b
    nh_ref[...] = hs
    y_lru = hs * jax.nn.gelu(u_gate)

    o_ref[...] = _mix_out(x, g_ref[...], y_pool, y_lru, wout_ref, lng_ref, lnb_ref)


def _mixer_sample_call(x2d, mod, state_pool, state_conv, state_h, wts, new_states, *, layer, n_sample, n_prompt):
    rows = x2d.shape[0]
    npool, nconv, nh = new_states
    n_in = 7 + len(wts)
    any_spec = pl.BlockSpec(memory_space=pl.ANY)
    return pl.pallas_call(
        _mixer_sample_kernel,
        out_shape=(jax.ShapeDtypeStruct((rows, D_MODEL), F32),
                   jax.ShapeDtypeStruct(npool.shape, F32),
                   jax.ShapeDtypeStruct(nconv.shape, F32),
                   jax.ShapeDtypeStruct(nh.shape, F32)),
        grid=(1,),
        in_specs=[pl.BlockSpec((rows, D_MODEL), lambda i: (0, 0))]
        + _mod_specs(layer, 1, True, n_sample, n_prompt)
        + [_layer_block(state_pool.shape, layer), _layer_block(state_conv.shape, layer),
           _layer_block(state_h.shape, layer)]
        + [_resident(w.shape, layer) for w in wts]
        + [any_spec, any_spec, any_spec],
        out_specs=(pl.BlockSpec((rows, D_MODEL), lambda i: (0, 0)),
                   _layer_block(npool.shape, layer), _layer_block(nconv.shape, layer), _layer_block(nh.shape, layer)),
        input_output_aliases={n_in: 1, n_in + 1: 2, n_in + 2: 3},
        compiler_params=pltpu.CompilerParams(
            dimension_semantics=("arbitrary",),
            vmem_limit_bytes=48 * MIB),
        name="mixer_s",
    )(x2d, mod, mod, mod, state_pool, state_conv, state_h, *wts, npool, nconv, nh)


def _block_diag(w):
    depth, groups, c, _ = w.shape
    eye = jnp.eye(groups, dtype=w.dtype)
    return jnp.einsum('lgij,gh->lgihj', w, eye).reshape(depth, groups * c, groups * c)


def kernel(x_prompt, x_sample, state_pool, state_conv, state_rglru, c_prompt, c_sample, w_ada, b_ada, ffn1_wgu, ffn1_wdown, w_in, w_out, w_pool, pool_scale, conv_w, conv_b, w_rg_a, b_rg_a, w_rg_x, b_rg_x, lru_lambda, ffn2_wgu, ffn2_wdown, ln_g, ln_b):
    n_prompt, seq, _ = x_prompt.shape
    n_sample = x_sample.shape[0]
    assert x_sample.shape[1] == 1 and n_sample % n_prompt == 0 and n_prompt == SUBLANES
    assert seq % MIXER_STEPS == 0 and (seq * n_prompt) % FFN_TILE == 0

    wgu1, wgu2 = ffn1_wgu.astype(BF16), ffn2_wgu.astype(BF16)
    wd1, wd2 = ffn1_wdown.astype(BF16), ffn2_wdown.astype(BF16)
    row = lambda v: v.reshape(DEPTH, 1, v.shape[-1])
    mixer_w = (
        w_in.astype(BF16), _block_diag(w_pool).astype(BF16), row(pool_scale), conv_w, row(conv_b),
        jnp.concatenate([_block_diag(w_rg_a), _block_diag(w_rg_x)], axis=-1).astype(BF16),
        row(b_rg_a), row(b_rg_x), row(lru_lambda), w_out.astype(BF16), ln_g, ln_b,
    )

    mod = _ada_call(jnp.concatenate([c_sample, c_prompt], axis=0), w_ada, b_ada)

    xp = x_prompt
    xs = x_sample.reshape(n_sample, D_MODEL)
    new_p = (jnp.zeros((DEPTH, n_prompt, POOL_BUF, POOL_WIDTH), F32),
             jnp.zeros((DEPTH, n_prompt, CONV_WIDTH - 1, LRU_WIDTH), F32),
             jnp.zeros((DEPTH, n_prompt, LRU_WIDTH), F32))
    new_s = (jnp.zeros(state_pool.shape, F32), jnp.zeros(state_conv.shape, F32), jnp.zeros(state_rglru.shape, F32))

    common = dict(n_sample=n_sample, n_prompt=n_prompt)
    for l in range(DEPTH):
        ffn_p = functools.partial(_ffn_call, layer=l, is_sample=False, tm=FFN_TILE, **common)
        ffn_s = functools.partial(_ffn_call, layer=l, is_sample=True, tm=n_sample, **common)
        xp = ffn_p(xp, mod, wgu1, wd1, ln_g, ln_b, sub=0, ln_idx=0, batch_major_in=(l == 0))
        xs = ffn_s(xs, mod, wgu1, wd1, ln_g, ln_b, sub=0, ln_idx=0)
        xp, *new_p = _mixer_prompt_call(xp, mod, mixer_w, new_p, layer=l, steps=MIXER_STEPS, **common)
        xs, *new_s = _mixer_sample_call(xs, mod, state_pool, state_conv, state_rglru, mixer_w, new_s,
                                        layer=l, **common)
        xp = ffn_p(xp, mod, wgu2, wd2, ln_g, ln_b, sub=2, ln_idx=2, batch_major_out=(l == DEPTH - 1))
        xs = ffn_s(xs, mod, wgu2, wd2, ln_g, ln_b, sub=2, ln_idx=2)

    return (xp, xs.reshape(n_sample, 1, D_MODEL), *new_p, *new_s)
```

```python
import functools

import jax
import jax.numpy as jnp
from jax import lax
from jax.experimental import pallas as pl
from jax.experimental.pallas import tpu as pltpu

D_MODEL = 1024
DEPTH = 4
PAST_LEN = 16384
POOL_WIDTH = D_MODEL // 2
LRU_WIDTH = D_MODEL - POOL_WIDTH
IN_WIDTH = POOL_WIDTH + 2 * LRU_WIDTH
POOL_WINDOWS = (2, 4, 8, 16)
POOL_GROUP_DIM = POOL_WIDTH // len(POOL_WINDOWS)
POOL_BUF = max(POOL_WINDOWS) - 1
N_LRU_HEADS = 8
LRU_HEAD_DIM = LRU_WIDTH // N_LRU_HEADS
CONV_WIDTH = 4
LRU_C = 8.0
D_FF = 128 * int(round(8 * D_MODEL / 3 / 128))
N_MOD = 9
ALPHA = (2 * DEPTH) ** 0.25
LN_EPS = 1e-5

F32 = jnp.float32
BF16 = jnp.bfloat16

SUBLANES = 8
LANES = 128
BF16_SUBLANES = 16
MIB = 1024 * 1024

FFN_TILE = 512
FFN_SUBTILES = 2
MIXER_STEPS = 64
MIXER_SUBTILES = 2


def _layer_norm(y, g, b):
    mu = jnp.mean(y, axis=-1, keepdims=True)
    d = y - mu
    var = jnp.mean(d * d, axis=-1, keepdims=True)
    return d * lax.rsqrt(var + LN_EPS) * g + b


def _silu(x):
    return x * jax.nn.sigmoid(x)


def _softplus(x):
    return jnp.maximum(x, 0.0) + jnp.log1p(jnp.exp(-jnp.abs(x)))


def _per_batch(fn, x, *ms):
    rows, d = x.shape
    nb = ms[0].shape[0]
    if nb == rows:
        return fn(x, *ms)
    return fn(x.reshape(rows // nb, nb, d), *[m[None] for m in ms]).reshape(rows, d)


def _modulate(x, sh, sc):
    return _per_batch(lambda x_, sh_, sc_: x_ * (1.0 + sc_) + sh_, x, sh, sc)


def _gated_residual(x, g, y, gate_scale):
    return _per_batch(lambda y_, g_: (gate_scale * (1.0 + g_)) * y_, y, g) + ALPHA * x


def _mod_specs(layer, sub, is_sample, n_sample, n_prompt):
    rows, block = (n_sample, 0) if is_sample else (n_prompt, n_sample // n_prompt)
    return [pl.BlockSpec((None, rows, D_MODEL), functools.partial(lambda i, k: (layer, block, k), k=3 * sub + t))
            for t in range(3)]


def _resident(w, layer):
    if w.ndim == 2:
        return pl.BlockSpec(w.shape, lambda i: (0, 0), pipeline_mode=pl.Buffered(1))
    nd = w.ndim
    return pl.BlockSpec((None,) + tuple(w.shape[1:]), lambda i: (layer,) + (0,) * (nd - 1),
                        pipeline_mode=pl.Buffered(1))


def _layer_block(shape, layer):
    nd = len(shape)
    return pl.BlockSpec((None,) + tuple(shape[1:]), lambda i: (layer,) + (0,) * (nd - 1))


def _round_specs(w, layer, n_steps):
    rows, cols = w.shape[1:]
    n_chunks = next(n for n in range(n_steps, 0, -1) if rows % (n * BF16_SUBLANES) == 0)
    chunk = rows // n_chunks
    idx = lambda i: jnp.minimum(i, n_chunks - 1)
    return (pl.BlockSpec((None, chunk, cols), lambda i: (layer, idx(i), 0)),
            pl.BlockSpec((chunk, cols), lambda i: (idx(i), 0)),
            jax.ShapeDtypeStruct((rows, cols), BF16))


def _round_chunks(src_refs, dst_refs):
    for src, dst in zip(src_refs, dst_refs):
        dst[...] = src[...].astype(BF16)


def _ada_kernel(c_ref, w_ref, b_ref, o_ref):
    sc = _silu(c_ref[...]).astype(BF16)
    w = w_ref[...].astype(BF16)
    o_ref[...] = jnp.dot(sc, w, preferred_element_type=F32) + b_ref[...]


def _ada_call(c_all, w_ada, b_ada):
    n_rows = c_all.shape[0]
    n_out = w_ada.shape[-1]
    tn = n_out // 8
    return pl.pallas_call(
        _ada_kernel,
        out_shape=jax.ShapeDtypeStruct((DEPTH, n_rows, n_out), F32),
        grid=(DEPTH, n_out // tn),
        in_specs=[
            pl.BlockSpec((n_rows, D_MODEL), lambda l, j: (0, 0)),
            pl.BlockSpec((None, D_MODEL, tn), lambda l, j: (l, 0, j)),
            pl.BlockSpec((None, 1, tn), lambda l, j: (l, 0, j)),
        ],
        out_specs=pl.BlockSpec((None, n_rows, tn), lambda l, j: (l, 0, j)),
        compiler_params=pltpu.CompilerParams(
            dimension_semantics=("arbitrary", "arbitrary"),
            vmem_limit_bytes=32 * MIB),
        name="ada",
    )(c_all, w_ada, b_ada.reshape(DEPTH, 1, n_out))


def _ffn_kernel(*refs, ln_idx, subtiles, batch_major_in, batch_major_out, n_round):
    x_ref, sh_ref, sc_ref, g_ref, wgu_ref, wd_ref, lng_ref, lnb_ref = refs[:8]
    o_ref = refs[8 + n_round]
    _round_chunks(refs[8:8 + n_round], refs[9 + n_round:9 + 2 * n_round])
    lbuf = refs[9 + 2 * n_round:]
    sh, sc, g = sh_ref[...], sc_ref[...], g_ref[...]
    nb = sh.shape[0]
    rows = (x_ref.shape[0] * x_ref.shape[1] if batch_major_in else x_ref.shape[0]) // subtiles
    ts = rows // nb
    for s_i in range(subtiles):
        steps = slice(s_i * ts, (s_i + 1) * ts)
        if batch_major_in:
            for b in range(nb):
                lbuf[0][steps, b, :] = x_ref[b, steps, :]
            x = lbuf[0][steps, :, :].reshape(rows, D_MODEL)
        else:
            x = x_ref[s_i * rows:(s_i + 1) * rows, :]
        h = _modulate(x, sh, sc).astype(BF16)
        gu = jnp.dot(h, wgu_ref[...], preferred_element_type=F32)
        act = (_silu(gu[:, :D_FF]) * gu[:, D_FF:]).astype(BF16)
        acc = jnp.dot(act, wd_ref[...], preferred_element_type=F32)
        y = _gated_residual(x, g, acc, 0.5)
        y = _layer_norm(y, lng_ref[ln_idx:ln_idx + 1, :], lnb_ref[ln_idx:ln_idx + 1, :])
        if batch_major_out:
            lbuf[0][steps, :, :] = y.reshape(ts, nb, D_MODEL)
            for b in range(nb):
                o_ref[b, steps, :] = lbuf[0][steps, b, :]
        else:
            o_ref[s_i * rows:(s_i + 1) * rows, :] = y


def _ffn_call(x, mod, wgu, wd, ln_g, ln_b, *, layer, sub, ln_idx, is_sample, tm, n_sample, n_prompt,
              batch_major_in=False, batch_major_out=False, round_next=()):
    rows = x.shape[0] * x.shape[1] if batch_major_in else x.shape[0]
    steps = tm // n_prompt
    n_grid = rows // tm
    major_spec = pl.BlockSpec((n_prompt, steps, D_MODEL), lambda i: (0, i, 0))
    rows_spec = pl.BlockSpec((tm, D_MODEL), lambda i: (i, 0))
    out_shape = (n_prompt, rows // n_prompt, D_MODEL) if batch_major_out else (rows, D_MODEL)
    rnd = [_round_specs(w, l, n_grid) for w, l in round_next]
    kern = functools.partial(_ffn_kernel, ln_idx=ln_idx, subtiles=1 if is_sample else FFN_SUBTILES,
                             batch_major_in=batch_major_in, batch_major_out=batch_major_out, n_round=len(rnd))
    outs = pl.pallas_call(
        kern,
        out_shape=[jax.ShapeDtypeStruct(out_shape, F32)] + [r[2] for r in rnd],
        grid=(n_grid,),
        in_specs=[major_spec if batch_major_in else rows_spec]
        + _mod_specs(layer, sub, is_sample, n_sample, n_prompt)
        + [_resident(wgu, layer), _resident(wd, layer), _resident(ln_g, layer), _resident(ln_b, layer)]
        + [r[0] for r in rnd],
        out_specs=[major_spec if batch_major_out else rows_spec] + [r[1] for r in rnd],
        scratch_shapes=[pltpu.VMEM((steps, n_prompt, D_MODEL), F32)] if batch_major_in or batch_major_out else [],
        compiler_params=pltpu.CompilerParams(
            dimension_semantics=("arbitrary",),
            vmem_limit_bytes=52 * MIB),
        name=f"ffn{sub}_{'s' if is_sample else 'p'}",
    )(x, mod, mod, mod, wgu, wd, ln_g, ln_b, *[w for w, _ in round_next])
    return outs if rnd else outs[0]


def _gates(xc, wg_ref, ba_ref, bx_ref, lam_ref):
    gates = jnp.dot(xc.astype(BF16), wg_ref[...], preferred_element_type=F32)
    r = jax.nn.sigmoid(gates[:, :LRU_WIDTH] + ba_ref[...])
    i = jax.nn.sigmoid(gates[:, LRU_WIDTH:] + bx_ref[...])
    log_a = (-LRU_C * r) * _softplus(-lam_ref[...])
    a = jnp.exp(log_a)
    t = jnp.tanh(log_a)
    mult = jnp.sqrt((-2.0 * t) / (1.0 - t))
    return a, mult * (i * xc)


def _mix_out(x, g, y_pool, y_lru, wout_ref, lng_ref, lnb_ref):
    mix = jnp.dot(y_pool.astype(BF16), wout_ref[:POOL_WIDTH, :], preferred_element_type=F32)
    mix = mix + jnp.dot(y_lru.astype(BF16), wout_ref[POOL_WIDTH:, :], preferred_element_type=F32)
    y = _gated_residual(x, g, mix, 1.0)
    return _layer_norm(y, lng_ref[1:2, :], lnb_ref[1:2, :])


N_MIXER_WEIGHTS = 12


def _mixer_prompt_kernel(*refs, steps, nb, subtiles, n_round):
    x_ref, sh_ref, sc_ref, g_ref = refs[:4]
    (win_ref, wpool_ref, pscale_ref, cw_ref, cb_ref, wg_ref, ba_ref, bx_ref, lam_ref,
     wout_ref, lng_ref, lnb_ref) = refs[4:4 + N_MIXER_WEIGHTS]
    n_in = 4 + N_MIXER_WEIGHTS + n_round + 3
    o_ref, npool_ref, nconv_ref, nh_ref = refs[n_in:n_in + 4]
    _round_chunks(refs[4 + N_MIXER_WEIGHTS:4 + N_MIXER_WEIGHTS + n_round], refs[n_in + 4:n_in + 4 + n_round])
    zbuf, cbuf, hst, hsbuf = refs[n_in + 4 + n_round:]

    pid = pl.program_id(0)
    tm = steps * nb
    pool_hist = POOL_BUF * nb
    conv_hist = (CONV_WIDTH - 1) * nb

    @pl.when(pid == 0)
    def _new_sequences():
        zbuf[0:pool_hist, :] = jnp.zeros((pool_hist, POOL_WIDTH), F32)
        cbuf[0:conv_hist, :] = jnp.zeros((conv_hist, LRU_WIDTH), F32)
        hst[...] = jnp.zeros_like(hst)

    sh, sc, g = sh_ref[...], sc_ref[...], g_ref[...]
    hm, hsteps = tm // subtiles, steps // subtiles
    t_idx = lax.shift_right_logical(lax.broadcasted_iota(jnp.int32, (hm, POOL_GROUP_DIM), 0), nb.bit_length() - 1)
    subs = range(subtiles)
    rows = [slice(s_i * hm, (s_i + 1) * hm) for s_i in subs]

    xs = [x_ref[rs, :] for rs in rows]
    projs = [jnp.dot(_modulate(x, sh, sc).astype(BF16), win_ref[...], preferred_element_type=F32) for x in xs]
    u_gates = [p[:, POOL_WIDTH + LRU_WIDTH:] for p in projs]

    y_pools, coeffs = [], []
    for s_i, proj in zip(subs, projs):
        r0 = s_i * hm
        u_pool = proj[:, :POOL_WIDTH]
        u_lru = proj[:, POOL_WIDTH:POOL_WIDTH + LRU_WIDTH]

        zbuf[pool_hist + r0:pool_hist + r0 + hm, :] = u_pool
        pos1 = (pid * steps + s_i * hsteps + 1) + t_idx
        cur = zbuf[r0:r0 + pool_hist + hm, :]
        first, w = -POOL_BUF, 1
        pooled = []
        for gi, wnd in enumerate(POOL_WINDOWS):
            assert wnd == 2 * w
            n = cur.shape[0]
            cur = cur[w * nb:, :] + cur[:n - w * nb, :]
            first, w = first + w, wnd
            cols = slice(gi * POOL_GROUP_DIM, (gi + 1) * POOL_GROUP_DIM)
            s = cur[-first * nb:, :POOL_GROUP_DIM]
            pooled.append(s / jnp.minimum(pos1, wnd).astype(F32) - u_pool[:, cols])
            cur = cur[:, POOL_GROUP_DIM:]
        assert first == 0
        pooled = jnp.concatenate(pooled, axis=-1).astype(BF16)
        y_pools.append(jnp.dot(pooled, wpool_ref[...], preferred_element_type=F32) * pscale_ref[...])

        cbuf[conv_hist + r0:conv_hist + r0 + hm, :] = u_lru
        xc = None
        for k in range(CONV_WIDTH):
            term = cbuf[r0 + k * nb:r0 + k * nb + hm, :] * cw_ref[k:k + 1, :]
            xc = term if xc is None else xc + term
        coeffs.append(_gates(cb_ref[...] + xc, wg_ref, ba_ref, bx_ref, lam_ref))

    hcur = hst[...]
    for s_i, (a, b) in zip(subs, coeffs):
        r0 = s_i * hm
        for t in range(hsteps):
            hcur = a[t * nb:(t + 1) * nb, :] * hcur + b[t * nb:(t + 1) * nb, :]
            hsbuf[r0 + t * nb:r0 + (t + 1) * nb, :] = hcur
    hst[...] = hcur

    for rs, x, y_pool, u_gate in zip(rows, xs, y_pools, u_gates):
        y_lru = hsbuf[rs, :] * jax.nn.gelu(u_gate)
        o_ref[rs, :] = _mix_out(x, g, y_pool, y_lru, wout_ref, lng_ref, lnb_ref)

    @pl.when(pid == pl.num_programs(0) - 1)
    def _final_state():
        npool_ref[...] = zbuf[tm:tm + pool_hist, :].reshape(POOL_BUF, nb, POOL_WIDTH)
        nconv_ref[...] = cbuf[tm:tm + conv_hist, :].reshape(CONV_WIDTH - 1, nb, LRU_WIDTH)
        nh_ref[...] = hcur

    zbuf[0:pool_hist, :] = zbuf[tm:tm + pool_hist, :]
    cbuf[0:conv_hist, :] = cbuf[tm:tm + conv_hist, :]


def _mixer_prompt_call(x2d, mod, wts, new_states, *, layer, steps, n_sample, n_prompt, round_next=()):
    assert len(wts) == N_MIXER_WEIGHTS
    rows = x2d.shape[0]
    tm = steps * n_prompt
    n_grid = rows // tm
    rnd = [_round_specs(w, l, n_grid) for w, l in round_next]
    kern = functools.partial(_mixer_prompt_kernel, steps=steps, nb=n_prompt, subtiles=MIXER_SUBTILES,
                             n_round=len(rnd))
    npool, nconv, nh = new_states
    n_in = 4 + len(wts) + len(rnd)
    any_spec = pl.BlockSpec(memory_space=pl.ANY)
    return pl.pallas_call(
        kern,
        out_shape=[jax.ShapeDtypeStruct((rows, D_MODEL), F32),
                   jax.ShapeDtypeStruct(npool.shape, F32),
                   jax.ShapeDtypeStruct(nconv.shape, F32),
                   jax.ShapeDtypeStruct(nh.shape, F32)] + [r[2] for r in rnd],
        grid=(n_grid,),
        in_specs=[pl.BlockSpec((tm, D_MODEL), lambda i: (i, 0))]
        + _mod_specs(layer, 1, False, n_sample, n_prompt)
        + [_resident(w, layer) for w in wts]
        + [r[0] for r in rnd]
        + [any_spec, any_spec, any_spec],
        out_specs=[pl.BlockSpec((tm, D_MODEL), lambda i: (i, 0)),
                   _layer_block(npool.shape, layer), _layer_block(nconv.shape, layer), _layer_block(nh.shape, layer)]
        + [r[1] for r in rnd],
        scratch_shapes=[
            pltpu.VMEM((POOL_BUF * n_prompt + tm, POOL_WIDTH), F32),
            pltpu.VMEM(((CONV_WIDTH - 1) * n_prompt + tm, LRU_WIDTH), F32),
            pltpu.VMEM((n_prompt, LRU_WIDTH), F32),
            pltpu.VMEM((tm, LRU_WIDTH), F32),
        ],
        input_output_aliases={n_in: 1, n_in + 1: 2, n_in + 2: 3},
        compiler_params=pltpu.CompilerParams(
            dimension_semantics=("arbitrary",),
            vmem_limit_bytes=48 * MIB),
        name="mixer_p",
    )(x2d, mod, mod, mod, *wts, *[w for w, _ in round_next], npool, nconv, nh)


def _mixer_sample_kernel(x_ref, sh_ref, sc_ref, g_ref, pst_ref, cst_ref, h0_ref,
                         win_ref, wpool_ref, pscale_ref, cw_ref, cb_ref,
                         wg_ref, ba_ref, bx_ref, lam_ref, wout_ref, lng_ref, lnb_ref,
                         npool_in, nconv_in, nh_in,
                         o_ref, npool_ref, nconv_ref, nh_ref):
    del npool_in, nconv_in, nh_in
    x = x_ref[...]
    h = _modulate(x, sh_ref[...], sc_ref[...]).astype(BF16)
    proj = jnp.dot(h, win_ref[...], preferred_element_type=F32)
    u_pool = proj[:, :POOL_WIDTH]
    u_lru = proj[:, POOL_WIDTH:POOL_WIDTH + LRU_WIDTH]
    u_gate = proj[:, POOL_WIDTH + LRU_WIDTH:]

    pooled = []
    for gi, w in enumerate(POOL_WINDOWS):
        cols = slice(gi * POOL_GROUP_DIM, (gi + 1) * POOL_GROUP_DIM)
        s = u_pool[:, cols]
        for back in range(1, w):
            s = s + pst_ref[POOL_BUF - back, :, cols]
        cnt = float(min(PAST_LEN + 1, w))
        pooled.append(s / cnt - u_pool[:, cols])
    pooled = jnp.concatenate(pooled, axis=-1).astype(BF16)
    y_pool = jnp.dot(pooled, wpool_ref[...], preferred_element_type=F32) * pscale_ref[...]
    npool_ref[0:POOL_BUF - 1] = pst_ref[1:POOL_BUF]
    npool_ref[POOL_BUF - 1] = u_pool

    xc = None
    for k in range(CONV_WIDTH - 1):
        term = cst_ref[k] * cw_ref[k:k + 1, :]
        xc = term if xc is None else xc + term
    xc = cb_ref[...] + (xc + u_lru * cw_ref[CONV_WIDTH - 1:CONV_WIDTH, :])
    nconv_ref[0:CONV_WIDTH - 2] = cst_ref[1:CONV_WIDTH - 1]
    nconv_ref[CONV_WIDTH - 2] = u_lru

    a, b = _gates(xc, wg_ref, ba_ref, bx_ref, lam_ref)
    hs = a * h0_ref[...] + b
    nh_ref[...] = hs
    y_lru = hs * jax.nn.gelu(u_gate)

    o_ref[...] = _mix_out(x, g_ref[...], y_pool, y_lru, wout_ref, lng_ref, lnb_ref)


def _mixer_sample_call(x2d, mod, state_pool, state_conv, state_h, wts, new_states, *, layer, n_sample, n_prompt):
    assert len(wts) == N_MIXER_WEIGHTS
    rows = x2d.shape[0]
    npool, nconv, nh = new_states
    n_in = 7 + len(wts)
    any_spec = pl.BlockSpec(memory_space=pl.ANY)
    return pl.pallas_call(
        _mixer_sample_kernel,
        out_shape=(jax.ShapeDtypeStruct((rows, D_MODEL), F32),
                   jax.ShapeDtypeStruct(npool.shape, F32),
                   jax.ShapeDtypeStruct(nconv.shape, F32),
                   jax.ShapeDtypeStruct(nh.shape, F32)),
        grid=(1,),
        in_specs=[pl.BlockSpec((rows, D_MODEL), lambda i: (0, 0))]
        + _mod_specs(layer, 1, True, n_sample, n_prompt)
        + [_layer_block(state_pool.shape, layer), _layer_block(state_conv.shape, layer),
           _layer_block(state_h.shape, layer)]
        + [_resident(w, layer) for w in wts]
        + [any_spec, any_spec, any_spec],
        out_specs=(pl.BlockSpec((rows, D_MODEL), lambda i: (0, 0)),
                   _layer_block(npool.shape, layer), _layer_block(nconv.shape, layer), _layer_block(nh.shape, layer)),
        input_output_aliases={n_in: 1, n_in + 1: 2, n_in + 2: 3},
        compiler_params=pltpu.CompilerParams(
            dimension_semantics=("arbitrary",),
            vmem_limit_bytes=48 * MIB),
        name="mixer_s",
    )(x2d, mod, mod, mod, state_pool, state_conv, state_h, *wts, npool, nconv, nh)


def _block_diag(w):
    depth, groups, c, _ = w.shape
    eye = jnp.eye(groups, dtype=w.dtype)
    return jnp.einsum('lgij,gh->lgihj', w, eye).reshape(depth, groups * c, groups * c)


def kernel(x_prompt, x_sample, state_pool, state_conv, state_rglru, c_prompt, c_sample, w_ada, b_ada, ffn1_wgu, ffn1_wdown, w_in, w_out, w_pool, pool_scale, conv_w, conv_b, w_rg_a, b_rg_a, w_rg_x, b_rg_x, lru_lambda, ffn2_wgu, ffn2_wdown, ln_g, ln_b):
    n_prompt, seq, _ = x_prompt.shape
    n_sample = x_sample.shape[0]
    assert x_sample.shape[1] == 1 and n_sample % n_prompt == 0 and n_prompt == SUBLANES
    assert seq % MIXER_STEPS == 0 and (seq * n_prompt) % FFN_TILE == 0

    row = lambda v: v.reshape(DEPTH, 1, v.shape[-1])
    wpool_bd = _block_diag(w_pool).astype(BF16)
    wgate_bd = jnp.concatenate([_block_diag(w_rg_a), _block_diag(w_rg_x)], axis=-1).astype(BF16)
    wgu_bf, wd_bf = ffn1_wgu[0].astype(BF16), ffn1_wdown[0].astype(BF16)

    mod = _ada_call(jnp.concatenate([c_sample, c_prompt], axis=0), w_ada, b_ada)

    xp = x_prompt
    xs = x_sample.reshape(n_sample, D_MODEL)
    hist_major = lambda a: jnp.swapaxes(a, 1, 2)
    spool_t, sconv_t = hist_major(state_pool), hist_major(state_conv)
    new_p = (jnp.zeros((DEPTH, POOL_BUF, n_prompt, POOL_WIDTH), F32),
             jnp.zeros((DEPTH, CONV_WIDTH - 1, n_prompt, LRU_WIDTH), F32),
             jnp.zeros((DEPTH, n_prompt, LRU_WIDTH), F32))
    new_s = (jnp.zeros(spool_t.shape, F32), jnp.zeros(sconv_t.shape, F32), jnp.zeros(state_rglru.shape, F32))

    common = dict(n_sample=n_sample, n_prompt=n_prompt)
    for l in range(DEPTH):
        ffn_p = functools.partial(_ffn_call, layer=l, is_sample=False, tm=FFN_TILE, **common)
        ffn_s = functools.partial(_ffn_call, layer=l, is_sample=True, tm=n_sample, **common)

        xp, win_bf, wout_bf = ffn_p(xp, mod, wgu_bf, wd_bf, ln_g, ln_b, sub=0, ln_idx=0, batch_major_in=(l == 0),
                                    round_next=((w_in, l), (w_out, l)))
        xs = ffn_s(xs, mod, wgu_bf, wd_bf, ln_g, ln_b, sub=0, ln_idx=0)

        mixer_w = (win_bf, wpool_bd, row(pool_scale), conv_w, row(conv_b), wgate_bd,
                   row(b_rg_a), row(b_rg_x), row(lru_lambda), wout_bf, ln_g, ln_b)
        xp, *new_p, wgu_bf, wd_bf = _mixer_prompt_call(xp, mod, mixer_w, new_p, layer=l, steps=MIXER_STEPS,
                                                       round_next=((ffn2_wgu, l), (ffn2_wdown, l)), **common)
        xs, *new_s = _mixer_sample_call(xs, mod, spool_t, sconv_t, state_rglru, mixer_w, new_s,
                                        layer=l, **common)

        last = l == DEPTH - 1
        outs = ffn_p(xp, mod, wgu_bf, wd_bf, ln_g, ln_b, sub=2, ln_idx=2, batch_major_out=last,
                     round_next=() if last else ((ffn1_wgu, l + 1), (ffn1_wdown, l + 1)))
        xs = ffn_s(xs, mod, wgu_bf, wd_bf, ln_g, ln_b, sub=2, ln_idx=2)
        if last:
            xp = outs
        else:
            xp, wgu_bf, wd_bf = outs

    return (xp, xs.reshape(n_sample, 1, D_MODEL),
            hist_major(new_p[0]), hist_major(new_p[1]), new_p[2],
            hist_major(new_s[0]), hist_major(new_s[1]), new_s[2])
```

```python
import functools

import jax
import jax.numpy as jnp
from jax import lax
from jax.experimental import pallas as pl
from jax.experimental.pallas import tpu as pltpu

D_MODEL = 1024
DEPTH = 4
PAST_LEN = 16384
POOL_WIDTH = D_MODEL // 2
LRU_WIDTH = D_MODEL - POOL_WIDTH
IN_WIDTH = POOL_WIDTH + 2 * LRU_WIDTH
POOL_WINDOWS = (2, 4, 8, 16)
POOL_GROUP_DIM = POOL_WIDTH // len(POOL_WINDOWS)
POOL_BUF = max(POOL_WINDOWS) - 1
N_LRU_HEADS = 8
LRU_HEAD_DIM = LRU_WIDTH // N_LRU_HEADS
CONV_WIDTH = 4
LRU_C = 8.0
D_FF = 128 * int(round(8 * D_MODEL / 3 / 128))
N_MOD = 9
ALPHA = (2 * DEPTH) ** 0.25
LN_EPS = 1e-5

F32 = jnp.float32
BF16 = jnp.bfloat16

SUBLANES = 8
LANES = 128
BF16_SUBLANES = 16
MIB = 1024 * 1024
GU_BLOCKS = D_FF // LANES

FFN_TILE = 512
FFN_SUBTILES = 2
MIXER_STEPS = 64
MIXER_SUBTILES = 2


def _layer_norm(y, g, b):
    mu = jnp.mean(y, axis=-1, keepdims=True)
    d = y - mu
    var = jnp.mean(d * d, axis=-1, keepdims=True)
    return d * lax.rsqrt(var + LN_EPS) * g + b


def _silu(x):
    return x * jax.nn.sigmoid(x)


def _softplus(x):
    return jnp.maximum(x, 0.0) + jnp.log1p(jnp.exp(-jnp.abs(x)))


def _per_batch(fn, x, *ms):
    rows, d = x.shape
    nb = ms[0].shape[0]
    if nb == rows:
        return fn(x, *ms)
    return fn(x.reshape(rows // nb, nb, d), *[m[None] for m in ms]).reshape(rows, d)


def _modulate(x, sh, sc):
    return _per_batch(lambda x_, sh_, sc_: x_ * (1.0 + sc_) + sh_, x, sh, sc)


def _gated_residual(x, g, y, gate_scale):
    return _per_batch(lambda y_, g_: (gate_scale * (1.0 + g_)) * y_, y, g) + ALPHA * x


def _mod_specs(layer, sub, is_sample, n_sample, n_prompt):
    rows, block = (n_sample, 0) if is_sample else (n_prompt, n_sample // n_prompt)
    return [pl.BlockSpec((None, rows, D_MODEL), functools.partial(lambda i, k: (layer, block, k), k=3 * sub + t))
            for t in range(3)]


def _resident(w, layer):
    if w.ndim == 2:
        return pl.BlockSpec(w.shape, lambda i: (0, 0), pipeline_mode=pl.Buffered(1))
    nd = w.ndim
    return pl.BlockSpec((None,) + tuple(w.shape[1:]), lambda i: (layer,) + (0,) * (nd - 1),
                        pipeline_mode=pl.Buffered(1))


def _layer_block(shape, layer):
    nd = len(shape)
    return pl.BlockSpec((None,) + tuple(shape[1:]), lambda i: (layer,) + (0,) * (nd - 1))


def _round_specs(w, layer, n_steps, step=lambda i: i):
    rows, cols = w.shape[1:]
    n_chunks = next(n for n in range(n_steps, 0, -1) if rows % (n * BF16_SUBLANES) == 0)
    chunk = rows // n_chunks
    idx = lambda *ids: jnp.minimum(step(*ids), n_chunks - 1)
    return (pl.BlockSpec((None, chunk, cols), lambda *ids: (layer, idx(*ids), 0)),
            pl.BlockSpec((chunk, cols), lambda *ids: (idx(*ids), 0)),
            jax.ShapeDtypeStruct((rows, cols), BF16))


def _round_chunks(src_refs, dst_refs, interleave):
    for src, dst, il in zip(src_refs, dst_refs, interleave):
        if il:
            for i in range(GU_BLOCKS):
                dst[:, 2 * i * LANES:(2 * i + 1) * LANES] = src[:, i * LANES:(i + 1) * LANES].astype(BF16)
                dst[:, (2 * i + 1) * LANES:(2 * i + 2) * LANES] = src[:, D_FF + i * LANES:D_FF + (i + 1) * LANES].astype(BF16)
        else:
            dst[...] = src[...].astype(BF16)


def _swiglu_act(gu):
    blocks = [_silu(gu[:, 2 * i * LANES:(2 * i + 1) * LANES]) * gu[:, (2 * i + 1) * LANES:(2 * i + 2) * LANES]
              for i in range(GU_BLOCKS)]
    return jnp.concatenate(blocks, axis=-1)


def _ada_kernel(*refs, interleave):
    n_round = len(interleave)
    c_ref, w_ref, b_ref = refs[:3]
    o_ref = refs[3 + n_round]
    _round_chunks(refs[3:3 + n_round], refs[4 + n_round:], interleave)
    sc = _silu(c_ref[...]).astype(BF16)
    w = w_ref[...].astype(BF16)
    o_ref[...] = jnp.dot(sc, w, preferred_element_type=F32) + b_ref[...]


def _ada_call(c_all, w_ada, b_ada, round_next=()):
    n_rows = c_all.shape[0]
    n_out = w_ada.shape[-1]
    n_col = 8
    tn = n_out // n_col
    rnd = [_round_specs(w, l, DEPTH * n_col, step=lambda l_, j: l_ * n_col + j) for w, l, _ in round_next]
    return pl.pallas_call(
        functools.partial(_ada_kernel, interleave=tuple(il for _, _, il in round_next)),
        out_shape=[jax.ShapeDtypeStruct((DEPTH, n_rows, n_out), F32)] + [r[2] for r in rnd],
        grid=(DEPTH, n_col),
        in_specs=[
            pl.BlockSpec((n_rows, D_MODEL), lambda l, j: (0, 0)),
            pl.BlockSpec((None, D_MODEL, tn), lambda l, j: (l, 0, j)),
            pl.BlockSpec((None, 1, tn), lambda l, j: (l, 0, j)),
        ] + [r[0] for r in rnd],
        out_specs=[pl.BlockSpec((None, n_rows, tn), lambda l, j: (l, 0, j))] + [r[1] for r in rnd],
        compiler_params=pltpu.CompilerParams(
            dimension_semantics=("arbitrary", "arbitrary"),
            vmem_limit_bytes=32 * MIB),
        name="ada",
    )(c_all, w_ada, b_ada.reshape(DEPTH, 1, n_out), *[w for w, _, _ in round_next])


def _ffn_kernel(*refs, ln_idx, subtiles, batch_major_in, batch_major_out, interleave):
    n_round = len(interleave)
    x_ref, sh_ref, sc_ref, g_ref, wgu_ref, wd_ref, lng_ref, lnb_ref = refs[:8]
    o_ref = refs[8 + n_round]
    _round_chunks(refs[8:8 + n_round], refs[9 + n_round:9 + 2 * n_round], interleave)
    lbuf = refs[9 + 2 * n_round:]
    sh, sc, g = sh_ref[...], sc_ref[...], g_ref[...]
    nb = sh.shape[0]
    rows = (x_ref.shape[0] * x_ref.shape[1] if batch_major_in else x_ref.shape[0]) // subtiles
    ts = rows // nb
    for s_i in range(subtiles):
        steps = slice(s_i * ts, (s_i + 1) * ts)
        if batch_major_in:
            for b in range(nb):
                lbuf[0][steps, b, :] = x_ref[b, steps, :]
            x = lbuf[0][steps, :, :].reshape(rows, D_MODEL)
        else:
            x = x_ref[s_i * rows:(s_i + 1) * rows, :]
        h = _modulate(x, sh, sc).astype(BF16)
        gu = jnp.dot(h, wgu_ref[...], preferred_element_type=F32)
        act = _swiglu_act(gu).astype(BF16)
        acc = jnp.dot(act, wd_ref[...], preferred_element_type=F32)
        y = _gated_residual(x, g, acc, 0.5)
        y = _layer_norm(y, lng_ref[ln_idx:ln_idx + 1, :], lnb_ref[ln_idx:ln_idx + 1, :])
        if batch_major_out:
            lbuf[0][steps, :, :] = y.reshape(ts, nb, D_MODEL)
            for b in range(nb):
                o_ref[b, steps, :] = lbuf[0][steps, b, :]
        else:
            o_ref[s_i * rows:(s_i + 1) * rows, :] = y


def _ffn_call(x, mod, wgu, wd, ln_g, ln_b, *, layer, sub, ln_idx, is_sample, tm, n_sample, n_prompt,
              batch_major_in=False, batch_major_out=False, round_next=()):
    rows = x.shape[0] * x.shape[1] if batch_major_in else x.shape[0]
    steps = tm // n_prompt
    n_grid = rows // tm
    major_spec = pl.BlockSpec((n_prompt, steps, D_MODEL), lambda i: (0, i, 0))
    rows_spec = pl.BlockSpec((tm, D_MODEL), lambda i: (i, 0))
    out_shape = (n_prompt, rows // n_prompt, D_MODEL) if batch_major_out else (rows, D_MODEL)
    rnd = [_round_specs(w, l, n_grid) for w, l, _ in round_next]
    kern = functools.partial(_ffn_kernel, ln_idx=ln_idx, subtiles=1 if is_sample else FFN_SUBTILES,
                             batch_major_in=batch_major_in, batch_major_out=batch_major_out,
                             interleave=tuple(il for _, _, il in round_next))
    outs = pl.pallas_call(
        kern,
        out_shape=[jax.ShapeDtypeStruct(out_shape, F32)] + [r[2] for r in rnd],
        grid=(n_grid,),
        in_specs=[major_spec if batch_major_in else rows_spec]
        + _mod_specs(layer, sub, is_sample, n_sample, n_prompt)
        + [_resident(wgu, layer), _resident(wd, layer), _resident(ln_g, layer), _resident(ln_b, layer)]
        + [r[0] for r in rnd],
        out_specs=[major_spec if batch_major_out else rows_spec] + [r[1] for r in rnd],
        scratch_shapes=[pltpu.VMEM((steps, n_prompt, D_MODEL), F32)] if batch_major_in or batch_major_out else [],
        compiler_params=pltpu.CompilerParams(
            dimension_semantics=("arbitrary",),
            vmem_limit_bytes=52 * MIB),
        name=f"ffn{sub}_{'s' if is_sample else 'p'}",
    )(x, mod, mod, mod, wgu, wd, ln_g, ln_b, *[w for w, _, _ in round_next])
    return outs if rnd else outs[0]


def _grouped_dot(x, w_ref, first, n):
    c = w_ref.shape[-1]
    return jnp.concatenate([jnp.dot(x[:, i * c:(i + 1) * c], w_ref[first + i], preferred_element_type=F32)
                            for i in range(n)], axis=-1)


def _gates(xc, wg_ref, ba_ref, bx_ref, lam_ref):
    xb = xc.astype(BF16)
    n = wg_ref.shape[0] // 2
    r = jax.nn.sigmoid(_grouped_dot(xb, wg_ref, 0, n) + ba_ref[...])
    i = jax.nn.sigmoid(_grouped_dot(xb, wg_ref, n, n) + bx_ref[...])
    log_a = (-LRU_C * r) * _softplus(-lam_ref[...])
    a = jnp.exp(log_a)
    t = jnp.tanh(log_a)
    mult = jnp.sqrt((-2.0 * t) / (1.0 - t))
    return a, mult * (i * xc)


def _mix_out(x, g, y_pool, y_lru, wout_ref, lng_ref, lnb_ref):
    mix = jnp.dot(y_pool.astype(BF16), wout_ref[:POOL_WIDTH, :], preferred_element_type=F32)
    mix = mix + jnp.dot(y_lru.astype(BF16), wout_ref[POOL_WIDTH:, :], preferred_element_type=F32)
    y = _gated_residual(x, g, mix, 1.0)
    return _layer_norm(y, lng_ref[1:2, :], lnb_ref[1:2, :])


N_MIXER_WEIGHTS = 12


def _mixer_prompt_kernel(*refs, steps, nb, subtiles, interleave):
    n_round = len(interleave)
    x_ref, sh_ref, sc_ref, g_ref = refs[:4]
    (win_ref, wpool_ref, pscale_ref, cw_ref, cb_ref, wg_ref, ba_ref, bx_ref, lam_ref,
     wout_ref, lng_ref, lnb_ref) = refs[4:4 + N_MIXER_WEIGHTS]
    n_in = 4 + N_MIXER_WEIGHTS + n_round + 3
    o_ref, npool_ref, nconv_ref, nh_ref = refs[n_in:n_in + 4]
    _round_chunks(refs[4 + N_MIXER_WEIGHTS:4 + N_MIXER_WEIGHTS + n_round], refs[n_in + 4:n_in + 4 + n_round],
                  interleave)
    zbuf, cbuf, hst, hsbuf = refs[n_in + 4 + n_round:]

    pid = pl.program_id(0)
    tm = steps * nb
    pool_hist = POOL_BUF * nb
    conv_hist = (CONV_WIDTH - 1) * nb

    @pl.when(pid == 0)
    def _new_sequences():
        zbuf[0:pool_hist, :] = jnp.zeros((pool_hist, POOL_WIDTH), F32)
        cbuf[0:conv_hist, :] = jnp.zeros((conv_hist, LRU_WIDTH), F32)
        hst[...] = jnp.zeros_like(hst)

    sh, sc, g = sh_ref[...], sc_ref[...], g_ref[...]
    hm, hsteps = tm // subtiles, steps // subtiles
    t_idx = lax.shift_right_logical(lax.broadcasted_iota(jnp.int32, (hm, POOL_GROUP_DIM), 0), nb.bit_length() - 1)
    subs = range(subtiles)
    rows = [slice(s_i * hm, (s_i + 1) * hm) for s_i in subs]

    xs = [x_ref[rs, :] for rs in rows]
    projs = [jnp.dot(_modulate(x, sh, sc).astype(BF16), win_ref[...], preferred_element_type=F32) for x in xs]
    u_gates = [p[:, POOL_WIDTH + LRU_WIDTH:] for p in projs]

    y_pools, coeffs = [], []
    for s_i, proj in zip(subs, projs):
        r0 = s_i * hm
        u_pool = proj[:, :POOL_WIDTH]
        u_lru = proj[:, POOL_WIDTH:POOL_WIDTH + LRU_WIDTH]

        zbuf[pool_hist + r0:pool_hist + r0 + hm, :] = u_pool
        pos1 = (pid * steps + s_i * hsteps + 1) + t_idx
        cur = zbuf[r0:r0 + pool_hist + hm, :]
        first, w = -POOL_BUF, 1
        pooled = []
        for gi, wnd in enumerate(POOL_WINDOWS):
            assert wnd == 2 * w
            n = cur.shape[0]
            cur = cur[w * nb:, :] + cur[:n - w * nb, :]
            first, w = first + w, wnd
            cols = slice(gi * POOL_GROUP_DIM, (gi + 1) * POOL_GROUP_DIM)
            s = cur[-first * nb:, :POOL_GROUP_DIM]
            pooled.append(s / jnp.minimum(pos1, wnd).astype(F32) - u_pool[:, cols])
            cur = cur[:, POOL_GROUP_DIM:]
        assert first == 0
        pooled = jnp.concatenate(pooled, axis=-1).astype(BF16)
        y_pools.append(_grouped_dot(pooled, wpool_ref, 0, len(POOL_WINDOWS)) * pscale_ref[...])

        cbuf[conv_hist + r0:conv_hist + r0 + hm, :] = u_lru
        xc = None
        for k in range(CONV_WIDTH):
            term = cbuf[r0 + k * nb:r0 + k * nb + hm, :] * cw_ref[k:k + 1, :]
            xc = term if xc is None else xc + term
        coeffs.append(_gates(cb_ref[...] + xc, wg_ref, ba_ref, bx_ref, lam_ref))

    hcur = hst[...]
    for s_i, (a, b) in zip(subs, coeffs):
        r0 = s_i * hm
        for t in range(hsteps):
            hcur = a[t * nb:(t + 1) * nb, :] * hcur + b[t * nb:(t + 1) * nb, :]
            hsbuf[r0 + t * nb:r0 + (t + 1) * nb, :] = hcur
    hst[...] = hcur

    for rs, x, y_pool, u_gate in zip(rows, xs, y_pools, u_gates):
        y_lru = hsbuf[rs, :] * jax.nn.gelu(u_gate)
        o_ref[rs, :] = _mix_out(x, g, y_pool, y_lru, wout_ref, lng_ref, lnb_ref)

    @pl.when(pid == pl.num_programs(0) - 1)
    def _final_state():
        npool_ref[...] = zbuf[tm:tm + pool_hist, :].reshape(POOL_BUF, nb, POOL_WIDTH)
        nconv_ref[...] = cbuf[tm:tm + conv_hist, :].reshape(CONV_WIDTH - 1, nb, LRU_WIDTH)
        nh_ref[...] = hcur

    zbuf[0:pool_hist, :] = zbuf[tm:tm + pool_hist, :]
    cbuf[0:conv_hist, :] = cbuf[tm:tm + conv_hist, :]


def _mixer_prompt_call(x2d, mod, wts, new_states, *, layer, steps, n_sample, n_prompt, round_next=()):
    assert len(wts) == N_MIXER_WEIGHTS
    rows = x2d.shape[0]
    tm = steps * n_prompt
    n_grid = rows // tm
    rnd = [_round_specs(w, l, n_grid) for w, l, _ in round_next]
    kern = functools.partial(_mixer_prompt_kernel, steps=steps, nb=n_prompt, subtiles=MIXER_SUBTILES,
                             interleave=tuple(il for _, _, il in round_next))
    npool, nconv, nh = new_states
    n_in = 4 + len(wts) + len(rnd)
    any_spec = pl.BlockSpec(memory_space=pl.ANY)
    return pl.pallas_call(
        kern,
        out_shape=[jax.ShapeDtypeStruct((rows, D_MODEL), F32),
                   jax.ShapeDtypeStruct(npool.shape, F32),
                   jax.ShapeDtypeStruct(nconv.shape, F32),
                   jax.ShapeDtypeStruct(nh.shape, F32)] + [r[2] for r in rnd],
        grid=(n_grid,),
        in_specs=[pl.BlockSpec((tm, D_MODEL), lambda i: (i, 0))]
        + _mod_specs(layer, 1, False, n_sample, n_prompt)
        + [_resident(w, layer) for w in wts]
        + [r[0] for r in rnd]
        + [any_spec, any_spec, any_spec],
        out_specs=[pl.BlockSpec((tm, D_MODEL), lambda i: (i, 0)),
                   _layer_block(npool.shape, layer), _layer_block(nconv.shape, layer), _layer_block(nh.shape, layer)]
        + [r[1] for r in rnd],
        scratch_shapes=[
            pltpu.VMEM((POOL_BUF * n_prompt + tm, POOL_WIDTH), F32),
            pltpu.VMEM(((CONV_WIDTH - 1) * n_prompt + tm, LRU_WIDTH), F32),
            pltpu.VMEM((n_prompt, LRU_WIDTH), F32),
            pltpu.VMEM((tm, LRU_WIDTH), F32),
        ],
        input_output_aliases={n_in: 1, n_in + 1: 2, n_in + 2: 3},
        compiler_params=pltpu.CompilerParams(
            dimension_semantics=("arbitrary",),
            vmem_limit_bytes=48 * MIB),
        name="mixer_p",
    )(x2d, mod, mod, mod, *wts, *[w for w, _, _ in round_next], npool, nconv, nh)


def _mixer_sample_kernel(x_ref, sh_ref, sc_ref, g_ref, pst_ref, cst_ref, h0_ref,
                         win_ref, wpool_ref, pscale_ref, cw_ref, cb_ref,
                         wg_ref, ba_ref, bx_ref, lam_ref, wout_ref, lng_ref, lnb_ref,
                         npool_in, nconv_in, nh_in,
                         o_ref, npool_ref, nconv_ref, nh_ref):
    del npool_in, nconv_in, nh_in
    x = x_ref[...]
    h = _modulate(x, sh_ref[...], sc_ref[...]).astype(BF16)
    proj = jnp.dot(h, win_ref[...], preferred_element_type=F32)
    u_pool = proj[:, :POOL_WIDTH]
    u_lru = proj[:, POOL_WIDTH:POOL_WIDTH + LRU_WIDTH]
    u_gate = proj[:, POOL_WIDTH + LRU_WIDTH:]

    pooled = []
    for gi, w in enumerate(POOL_WINDOWS):
        cols = slice(gi * POOL_GROUP_DIM, (gi + 1) * POOL_GROUP_DIM)
        s = u_pool[:, cols]
        for back in range(1, w):
            s = s + pst_ref[POOL_BUF - back, :, cols]
        cnt = float(min(PAST_LEN + 1, w))
        pooled.append(s / cnt - u_pool[:, cols])
    pooled = jnp.concatenate(pooled, axis=-1).astype(BF16)
    y_pool = _grouped_dot(pooled, wpool_ref, 0, len(POOL_WINDOWS)) * pscale_ref[...]
    npool_ref[0:POOL_BUF - 1] = pst_ref[1:POOL_BUF]
    npool_ref[POOL_BUF - 1] = u_pool

    xc = None
    for k in range(CONV_WIDTH - 1):
        term = cst_ref[k] * cw_ref[k:k + 1, :]
        xc = term if xc is None else xc + term
    xc = cb_ref[...] + (xc + u_lru * cw_ref[CONV_WIDTH - 1:CONV_WIDTH, :])
    nconv_ref[0:CONV_WIDTH - 2] = cst_ref[1:CONV_WIDTH - 1]
    nconv_ref[CONV_WIDTH - 2] = u_lru

    a, b = _gates(xc, wg_ref, ba_ref, bx_ref, lam_ref)
    hs = a * h0_ref[...] + b
    nh_ref[...] = hs
    y_lru = hs * jax.nn.gelu(u_gate)

    o_ref[...] = _mix_out(x, g_ref[...], y_pool, y_lru, wout_ref, lng_ref, lnb_ref)


def _mixer_sample_call(x2d, mod, state_pool, state_conv, state_h, wts, new_states, *, layer, n_sample, n_prompt):
    assert len(wts) == N_MIXER_WEIGHTS
    rows = x2d.shape[0]
    npool, nconv, nh = new_states
    n_in = 7 + len(wts)
    any_spec = pl.BlockSpec(memory_space=pl.ANY)
    return pl.pallas_call(
        _mixer_sample_kernel,
        out_shape=(jax.ShapeDtypeStruct((rows, D_MODEL), F32),
                   jax.ShapeDtypeStruct(npool.shape, F32),
                   jax.ShapeDtypeStruct(nconv.shape, F32),
                   jax.ShapeDtypeStruct(nh.shape, F32)),
        grid=(1,),
        in_specs=[pl.BlockSpec((rows, D_MODEL), lambda i: (0, 0))]
        + _mod_specs(layer, 1, True, n_sample, n_prompt)
        + [_layer_block(state_pool.shape, layer), _layer_block(state_conv.shape, layer),
           _layer_block(state_h.shape, layer)]
        + [_resident(w, layer) for w in wts]
        + [any_spec, any_spec, any_spec],
        out_specs=(pl.BlockSpec((rows, D_MODEL), lambda i: (0, 0)),
                   _layer_block(npool.shape, layer), _layer_block(nconv.shape, layer), _layer_block(nh.shape, layer)),
        input_output_aliases={n_in: 1, n_in + 1: 2, n_in + 2: 3},
        compiler_params=pltpu.CompilerParams(
            dimension_semantics=("arbitrary",),
            vmem_limit_bytes=48 * MIB),
        name="mixer_s",
    )(x2d, mod, mod, mod, state_pool, state_conv, state_h, *wts, npool, nconv, nh)


def _lane_blocks(w):
    depth, heads, c, _ = w.shape
    per = LANES // c
    eye = jnp.eye(per, dtype=w.dtype)
    blocks = jnp.einsum('lbgij,gh->lbgihj', w.reshape(depth, heads // per, per, c, c), eye)
    return blocks.reshape(depth, heads // per, LANES, LANES)


def kernel(x_prompt, x_sample, state_pool, state_conv, state_rglru, c_prompt, c_sample, w_ada, b_ada, ffn1_wgu, ffn1_wdown, w_in, w_out, w_pool, pool_scale, conv_w, conv_b, w_rg_a, b_rg_a, w_rg_x, b_rg_x, lru_lambda, ffn2_wgu, ffn2_wdown, ln_g, ln_b):
    n_prompt, seq, _ = x_prompt.shape
    n_sample = x_sample.shape[0]
    assert x_sample.shape[1] == 1 and n_sample % n_prompt == 0 and n_prompt == SUBLANES
    assert seq % MIXER_STEPS == 0 and (seq * n_prompt) % FFN_TILE == 0

    row = lambda v: v.reshape(DEPTH, 1, v.shape[-1])
    wpool_bd = _lane_blocks(w_pool).astype(BF16)
    wgate_bd = jnp.concatenate([_lane_blocks(w_rg_a), _lane_blocks(w_rg_x)], axis=1).astype(BF16)

    mod, wgu_bf, wd_bf = _ada_call(jnp.concatenate([c_sample, c_prompt], axis=0), w_ada, b_ada,
                                   round_next=((ffn1_wgu, 0, True), (ffn1_wdown, 0, False)))

    xp = x_prompt
    xs = x_sample.reshape(n_sample, D_MODEL)
    hist_major = lambda a: jnp.swapaxes(a, 1, 2)
    spool_t, sconv_t = hist_major(state_pool), hist_major(state_conv)
    new_p = (jnp.zeros((DEPTH, POOL_BUF, n_prompt, POOL_WIDTH), F32),
             jnp.zeros((DEPTH, CONV_WIDTH - 1, n_prompt, LRU_WIDTH), F32),
             jnp.zeros((DEPTH, n_prompt, LRU_WIDTH), F32))
    new_s = (jnp.zeros(spool_t.shape, F32), jnp.zeros(sconv_t.shape, F32), jnp.zeros(state_rglru.shape, F32))

    common = dict(n_sample=n_sample, n_prompt=n_prompt)
    for l in range(DEPTH):
        ffn_p = functools.partial(_ffn_call, layer=l, is_sample=False, tm=FFN_TILE, **common)
        ffn_s = functools.partial(_ffn_call, layer=l, is_sample=True, tm=n_sample, **common)

        xp, win_bf, wout_bf = ffn_p(xp, mod, wgu_bf, wd_bf, ln_g, ln_b, sub=0, ln_idx=0, batch_major_in=(l == 0),
                                    round_next=((w_in, l, False), (w_out, l, False)))
        xs = ffn_s(xs, mod, wgu_bf, wd_bf, ln_g, ln_b, sub=0, ln_idx=0)

        mixer_w = (win_bf, wpool_bd, row(pool_scale), conv_w, row(conv_b), wgate_bd,
                   row(b_rg_a), row(b_rg_x), row(lru_lambda), wout_bf, ln_g, ln_b)
        xp, *new_p, wgu_bf, wd_bf = _mixer_prompt_call(xp, mod, mixer_w, new_p, layer=l, steps=MIXER_STEPS,
                                                       round_next=((ffn2_wgu, l, True), (ffn2_wdown, l, False)),
                                                       **common)
        xs, *new_s = _mixer_sample_call(xs, mod, spool_t, sconv_t, state_rglru, mixer_w, new_s,
                                        layer=l, **common)

        last = l == DEPTH - 1
        outs = ffn_p(xp, mod, wgu_bf, wd_bf, ln_g, ln_b, sub=2, ln_idx=2, batch_major_out=last,
                     round_next=() if last else ((ffn1_wgu, l + 1, True), (ffn1_wdown, l + 1, False)))
        xs = ffn_s(xs, mod, wgu_bf, wd_bf, ln_g, ln_b, sub=2, ln_idx=2)
        if last:
            xp = outs
        else:
            xp, wgu_bf, wd_bf = outs

    return (xp, xs.reshape(n_sample, 1, D_MODEL),
            hist_major(new_p[0]), hist_major(new_p[1]), new_p[2],
            hist_major(new_s[0]), hist_major(new_s[1]), new_s[2])
```

```python
import functools

import jax
import jax.numpy as jnp
from jax import lax
from jax.experimental import pallas as pl
from jax.experimental.pallas import tpu as pltpu

D_MODEL = 1024
DEPTH = 4
PAST_LEN = 16384
POOL_WIDTH = D_MODEL // 2
LRU_WIDTH = D_MODEL - POOL_WIDTH
IN_WIDTH = POOL_WIDTH + 2 * LRU_WIDTH
POOL_WINDOWS = (2, 4, 8, 16)
POOL_GROUP_DIM = POOL_WIDTH // len(POOL_WINDOWS)
POOL_BUF = max(POOL_WINDOWS) - 1
N_LRU_HEADS = 8
LRU_HEAD_DIM = LRU_WIDTH // N_LRU_HEADS
CONV_WIDTH = 4
LRU_C = 8.0
D_FF = 128 * int(round(8 * D_MODEL / 3 / 128))
N_MOD = 9
ALPHA = (2 * DEPTH) ** 0.25
LN_EPS = 1e-5

F32 = jnp.float32
BF16 = jnp.bfloat16

SUBLANES = 8
LANES = 128
BF16_SUBLANES = 16
MIB = 1024 * 1024
GU_BLOCKS = D_FF // LANES

FFN_TILE = 512
FFN_SUBTILES = 2
MIXER_STEPS = 64
MIXER_SUBTILES = 2
ADA_VMEM = 32 * MIB
FFN_VMEM = 56 * MIB
MIXER_VMEM = 56 * MIB


def _layer_norm(y, g, b):
    mu = jnp.mean(y, axis=-1, keepdims=True)
    d = y - mu
    var = jnp.mean(d * d, axis=-1, keepdims=True)
    return d * lax.rsqrt(var + LN_EPS) * g + b


def _silu(x):
    return x * jax.nn.sigmoid(x)


def _softplus(x):
    return jnp.maximum(x, 0.0) + jnp.log1p(jnp.exp(-jnp.abs(x)))


def _per_batch(fn, x, *ms):
    rows, d = x.shape
    nb = ms[0].shape[0]
    if nb == rows:
        return fn(x, *ms)
    return fn(x.reshape(rows // nb, nb, d), *[m[None] for m in ms]).reshape(rows, d)


def _modulate(x, sh, sc):
    return _per_batch(lambda x_, sh_, sc_: x_ * (1.0 + sc_) + sh_, x, sh, sc)


def _gated_residual(x, g, y, gate_scale):
    return _per_batch(lambda y_, g_: (gate_scale * (1.0 + g_)) * y_, y, g) + ALPHA * x


def _mod_specs(layer, sub, is_sample, n_sample, n_prompt):
    rows, block = (n_sample, 0) if is_sample else (n_prompt, n_sample // n_prompt)
    return [pl.BlockSpec((None, rows, D_MODEL), functools.partial(lambda i, k: (layer, block, k), k=3 * sub + t))
            for t in range(3)]


def _resident(w, layer):
    if w.ndim == 2:
        return pl.BlockSpec(w.shape, lambda i: (0, 0), pipeline_mode=pl.Buffered(1))
    nd = w.ndim
    return pl.BlockSpec((None,) + tuple(w.shape[1:]), lambda i: (layer,) + (0,) * (nd - 1),
                        pipeline_mode=pl.Buffered(1))


def _layer_block(shape, layer):
    nd = len(shape)
    return pl.BlockSpec((None,) + tuple(shape[1:]), lambda i: (layer,) + (0,) * (nd - 1))


def _whole(shape):
    return pl.BlockSpec(tuple(shape), lambda i: (0,) * len(shape))


def _round_specs(w, layer, n_steps, step=lambda i: i):
    rows, cols = w.shape[1:]
    n_chunks = next(n for n in range(n_steps, 0, -1) if rows % (n * BF16_SUBLANES) == 0)
    chunk = rows // n_chunks
    idx = lambda *ids: jnp.minimum(step(*ids), n_chunks - 1)
    return (pl.BlockSpec((None, chunk, cols), lambda *ids: (layer, idx(*ids), 0)),
            pl.BlockSpec((chunk, cols), lambda *ids: (idx(*ids), 0)),
            jax.ShapeDtypeStruct((rows, cols), BF16))


def _round_chunks(src_refs, dst_refs, interleave):
    for src, dst, il in zip(src_refs, dst_refs, interleave):
        if il:
            for i in range(GU_BLOCKS):
                dst[:, 2 * i * LANES:(2 * i + 1) * LANES] = src[:, i * LANES:(i + 1) * LANES].astype(BF16)
                dst[:, (2 * i + 1) * LANES:(2 * i + 2) * LANES] = src[:, D_FF + i * LANES:D_FF + (i + 1) * LANES].astype(BF16)
        else:
            dst[...] = src[...].astype(BF16)


def _split(refs, *counts):
    out, pos = [], 0
    for c in counts:
        out.append(refs[pos:pos + c])
        pos += c
    out.append(refs[pos:])
    return out


def _ada_kernel(*refs, interleave):
    n_round = len(interleave)
    (c_ref, w_ref, b_ref), round_in, (o_ref,), round_out = _split(refs, 3, n_round, 1)
    _round_chunks(round_in, round_out, interleave)
    sc = _silu(c_ref[...]).astype(BF16)
    w = w_ref[...].astype(BF16)
    o_ref[...] = jnp.dot(sc, w, preferred_element_type=F32) + b_ref[...]


def _ada_call(c_all, w_ada, b_ada, round_next=()):
    n_rows = c_all.shape[0]
    n_out = w_ada.shape[-1]
    n_col = 8
    tn = n_out // n_col
    rnd = [_round_specs(w, l, DEPTH * n_col, step=lambda l_, j: l_ * n_col + j) for w, l, _ in round_next]
    return pl.pallas_call(
        functools.partial(_ada_kernel, interleave=tuple(il for _, _, il in round_next)),
        out_shape=[jax.ShapeDtypeStruct((DEPTH, n_rows, n_out), F32)] + [r[2] for r in rnd],
        grid=(DEPTH, n_col),
        in_specs=[
            pl.BlockSpec((n_rows, D_MODEL), lambda l, j: (0, 0)),
            pl.BlockSpec((None, D_MODEL, tn), lambda l, j: (l, 0, j)),
            pl.BlockSpec((None, 1, tn), lambda l, j: (l, 0, j)),
        ] + [r[0] for r in rnd],
        out_specs=[pl.BlockSpec((None, n_rows, tn), lambda l, j: (l, 0, j))] + [r[1] for r in rnd],
        compiler_params=pltpu.CompilerParams(
            dimension_semantics=("arbitrary", "arbitrary"),
            vmem_limit_bytes=ADA_VMEM),
        name="ada",
    )(c_all, w_ada, b_ada.reshape(DEPTH, 1, n_out), *[w for w, _, _ in round_next])


def _swiglu_act(gu):
    blocks = [_silu(gu[:, 2 * i * LANES:(2 * i + 1) * LANES]) * gu[:, (2 * i + 1) * LANES:(2 * i + 2) * LANES]
              for i in range(GU_BLOCKS)]
    return jnp.concatenate(blocks, axis=-1)


def _ffn_rows(x, sh, sc, g, wgu_ref, wd_ref, lng, lnb):
    h = _modulate(x, sh, sc).astype(BF16)
    gu = jnp.dot(h, wgu_ref[...], preferred_element_type=F32)
    act = _swiglu_act(gu).astype(BF16)
    acc = jnp.dot(act, wd_ref[...], preferred_element_type=F32)
    return _layer_norm(_gated_residual(x, g, acc, 0.5), lng, lnb)


def _ffn_kernel(*refs, ln_idx, subtiles, batch_major_in, batch_major_out, interleave):
    n_round = len(interleave)
    (x_ref, sh_ref, sc_ref, g_ref, wgu_ref, wd_ref, lng_ref, lnb_ref), sample_in, round_in, (o_ref, os_ref), \
        round_out, lbuf = _split(refs, 8, 4, n_round, 2, n_round)
    _round_chunks(round_in, round_out, interleave)
    lng, lnb = lng_ref[ln_idx:ln_idx + 1, :], lnb_ref[ln_idx:ln_idx + 1, :]
    sh, sc, g = sh_ref[...], sc_ref[...], g_ref[...]
    nb = sh.shape[0]
    rows = (x_ref.shape[0] * x_ref.shape[1] if batch_major_in else x_ref.shape[0]) // subtiles
    ts = rows // nb
    for s_i in range(subtiles):
        steps = slice(s_i * ts, (s_i + 1) * ts)
        if batch_major_in:
            for b in range(nb):
                lbuf[0][steps, b, :] = x_ref[b, steps, :]
            x = lbuf[0][steps, :, :].reshape(rows, D_MODEL)
        else:
            x = x_ref[s_i * rows:(s_i + 1) * rows, :]
        y = _ffn_rows(x, sh, sc, g, wgu_ref, wd_ref, lng, lnb)
        if batch_major_out:
            lbuf[0][steps, :, :] = y.reshape(ts, nb, D_MODEL)
            for b in range(nb):
                o_ref[b, steps, :] = lbuf[0][steps, b, :]
        else:
            o_ref[s_i * rows:(s_i + 1) * rows, :] = y

    @pl.when(pl.program_id(0) == pl.num_programs(0) - 1)
    def _sample_rows():
        xs_ref, shs_ref, scs_ref, gs_ref = sample_in
        os_ref[...] = _ffn_rows(xs_ref[...], shs_ref[...], scs_ref[...], gs_ref[...], wgu_ref, wd_ref, lng, lnb)


def _ffn_call(x, xs, mod, wgu, wd, ln_g, ln_b, *, layer, sub, ln_idx, tm, n_prompt,
              batch_major_in=False, batch_major_out=False, round_next=()):
    rows = x.shape[0] * x.shape[1] if batch_major_in else x.shape[0]
    n_sample = xs.shape[0]
    steps = tm // n_prompt
    n_grid = rows // tm
    major_spec = pl.BlockSpec((n_prompt, steps, D_MODEL), lambda i: (0, i, 0))
    rows_spec = pl.BlockSpec((tm, D_MODEL), lambda i: (i, 0))
    out_shape = (n_prompt, rows // n_prompt, D_MODEL) if batch_major_out else (rows, D_MODEL)
    rnd = [_round_specs(w, l, n_grid) for w, l, _ in round_next]
    kern = functools.partial(_ffn_kernel, ln_idx=ln_idx, subtiles=FFN_SUBTILES,
                             batch_major_in=batch_major_in, batch_major_out=batch_major_out,
                             interleave=tuple(il for _, _, il in round_next))
    return pl.pallas_call(
        kern,
        out_shape=[jax.ShapeDtypeStruct(out_shape, F32), jax.ShapeDtypeStruct(xs.shape, F32)] + [r[2] for r in rnd],
        grid=(n_grid,),
        in_specs=[major_spec if batch_major_in else rows_spec]
        + _mod_specs(layer, sub, False, n_sample, n_prompt)
        + [_resident(wgu, layer), _resident(wd, layer), _resident(ln_g, layer), _resident(ln_b, layer)]
        + [_whole(xs.shape)] + _mod_specs(layer, sub, True, n_sample, n_prompt)
        + [r[0] for r in rnd],
        out_specs=[major_spec if batch_major_out else rows_spec, _whole(xs.shape)] + [r[1] for r in rnd],
        scratch_shapes=[pltpu.VMEM((steps, n_prompt, D_MODEL), F32)] if batch_major_in or batch_major_out else [],
        compiler_params=pltpu.CompilerParams(
            dimension_semantics=("arbitrary",),
            vmem_limit_bytes=FFN_VMEM),
        name=f"ffn{sub}",
    )(x, mod, mod, mod, wgu, wd, ln_g, ln_b, xs, mod, mod, mod, *[w for w, _, _ in round_next])


def _grouped_dot(x, w_ref, first, n):
    c = w_ref.shape[-1]
    return jnp.concatenate([jnp.dot(x[:, i * c:(i + 1) * c], w_ref[first + i], preferred_element_type=F32)
                            for i in range(n)], axis=-1)


def _gates(xc, wg_ref, ba_ref, bx_ref, lam_ref):
    xb = xc.astype(BF16)
    n = wg_ref.shape[0] // 2
    r = jax.nn.sigmoid(_grouped_dot(xb, wg_ref, 0, n) + ba_ref[...])
    i = jax.nn.sigmoid(_grouped_dot(xb, wg_ref, n, n) + bx_ref[...])
    log_a = (-LRU_C * r) * _softplus(-lam_ref[...])
    a = jnp.exp(log_a)
    t = jnp.tanh(log_a)
    mult = jnp.sqrt((-2.0 * t) / (1.0 - t))
    return a, mult * (i * xc)


def _mix_out(x, g, y_pool, y_lru, wout_ref, lng_ref, lnb_ref):
    mix = jnp.dot(y_pool.astype(BF16), wout_ref[:POOL_WIDTH, :], preferred_element_type=F32)
    mix = mix + jnp.dot(y_lru.astype(BF16), wout_ref[POOL_WIDTH:, :], preferred_element_type=F32)
    y = _gated_residual(x, g, mix, 1.0)
    return _layer_norm(y, lng_ref[1:2, :], lnb_ref[1:2, :])


N_MIXER_WEIGHTS = 12


def _mixer_sample_rows(x_ref, sh_ref, sc_ref, g_ref, pst_ref, cst_ref, h0_ref, weights,
                       o_ref, npool_ref, nconv_ref, nh_ref):
    (win_ref, wpool_ref, pscale_ref, cw_ref, cb_ref, wg_ref, ba_ref, bx_ref, lam_ref,
     wout_ref, lng_ref, lnb_ref) = weights
    x = x_ref[...]
    h = _modulate(x, sh_ref[...], sc_ref[...]).astype(BF16)
    proj = jnp.dot(h, win_ref[...], preferred_element_type=F32)
    u_pool = proj[:, :POOL_WIDTH]
    u_lru = proj[:, POOL_WIDTH:POOL_WIDTH + LRU_WIDTH]
    u_gate = proj[:, POOL_WIDTH + LRU_WIDTH:]

    pooled = []
    for gi, w in enumerate(POOL_WINDOWS):
        cols = slice(gi * POOL_GROUP_DIM, (gi + 1) * POOL_GROUP_DIM)
        s = u_pool[:, cols]
        for back in range(1, w):
            s = s + pst_ref[POOL_BUF - back, :, cols]
        cnt = float(min(PAST_LEN + 1, w))
        pooled.append(s / cnt - u_pool[:, cols])
    pooled = jnp.concatenate(pooled, axis=-1).astype(BF16)
    y_pool = _grouped_dot(pooled, wpool_ref, 0, len(POOL_WINDOWS)) * pscale_ref[...]
    npool_ref[0:POOL_BUF - 1] = pst_ref[1:POOL_BUF]
    npool_ref[POOL_BUF - 1] = u_pool

    xc = None
    for k in range(CONV_WIDTH - 1):
        term = cst_ref[k] * cw_ref[k:k + 1, :]
        xc = term if xc is None else xc + term
    xc = cb_ref[...] + (xc + u_lru * cw_ref[CONV_WIDTH - 1:CONV_WIDTH, :])
    nconv_ref[0:CONV_WIDTH - 2] = cst_ref[1:CONV_WIDTH - 1]
    nconv_ref[CONV_WIDTH - 2] = u_lru

    a, b = _gates(xc, wg_ref, ba_ref, bx_ref, lam_ref)
    hs = a * h0_ref[...] + b
    nh_ref[...] = hs
    y_lru = hs * jax.nn.gelu(u_gate)

    o_ref[...] = _mix_out(x, g_ref[...], y_pool, y_lru, wout_ref, lng_ref, lnb_ref)


def _mixer_kernel(*refs, steps, nb, subtiles, interleave):
    n_round = len(interleave)
    (x_ref, sh_ref, sc_ref, g_ref), weights, sample_in, round_in, _aliased, \
        (o_ref, npool_ref, nconv_ref, nh_ref), sample_out, round_out, (zbuf, cbuf, hst, hsbuf) = \
        _split(refs, 4, N_MIXER_WEIGHTS, 7, n_round, 6, 4, 4, n_round)
    (win_ref, wpool_ref, pscale_ref, cw_ref, cb_ref, wg_ref, ba_ref, bx_ref, lam_ref,
     wout_ref, lng_ref, lnb_ref) = weights
    _round_chunks(round_in, round_out, interleave)

    pid = pl.program_id(0)
    tm = steps * nb
    pool_hist = POOL_BUF * nb
    conv_hist = (CONV_WIDTH - 1) * nb

    @pl.when(pid == 0)
    def _new_sequences():
        zbuf[0:pool_hist, :] = jnp.zeros((pool_hist, POOL_WIDTH), F32)
        cbuf[0:conv_hist, :] = jnp.zeros((conv_hist, LRU_WIDTH), F32)
        hst[...] = jnp.zeros_like(hst)

    sh, sc, g = sh_ref[...], sc_ref[...], g_ref[...]
    hm, hsteps = tm // subtiles, steps // subtiles
    t_idx = lax.shift_right_logical(lax.broadcasted_iota(jnp.int32, (hm, POOL_GROUP_DIM), 0), nb.bit_length() - 1)
    subs = range(subtiles)
    rows = [slice(s_i * hm, (s_i + 1) * hm) for s_i in subs]

    xs = [x_ref[rs, :] for rs in rows]
    projs = [jnp.dot(_modulate(x, sh, sc).astype(BF16), win_ref[...], preferred_element_type=F32) for x in xs]
    u_gates = [p[:, POOL_WIDTH + LRU_WIDTH:] for p in projs]

    y_pools, coeffs = [], []
    for s_i, proj in zip(subs, projs):
        r0 = s_i * hm
        u_pool = proj[:, :POOL_WIDTH]
        u_lru = proj[:, POOL_WIDTH:POOL_WIDTH + LRU_WIDTH]

        zbuf[pool_hist + r0:pool_hist + r0 + hm, :] = u_pool
        pos1 = (pid * steps + s_i * hsteps + 1) + t_idx
        cur = zbuf[r0:r0 + pool_hist + hm, :]
        first, w = -POOL_BUF, 1
        pooled = []
        for gi, wnd in enumerate(POOL_WINDOWS):
            assert wnd == 2 * w
            n = cur.shape[0]
            cur = cur[w * nb:, :] + cur[:n - w * nb, :]
            first, w = first + w, wnd
            cols = slice(gi * POOL_GROUP_DIM, (gi + 1) * POOL_GROUP_DIM)
            s = cur[-first * nb:, :POOL_GROUP_DIM]
            pooled.append(s / jnp.minimum(pos1, wnd).astype(F32) - u_pool[:, cols])
            cur = cur[:, POOL_GROUP_DIM:]
        assert first == 0
        pooled = jnp.concatenate(pooled, axis=-1).astype(BF16)
        y_pools.append(_grouped_dot(pooled, wpool_ref, 0, len(POOL_WINDOWS)) * pscale_ref[...])

        cbuf[conv_hist + r0:conv_hist + r0 + hm, :] = u_lru
        xc = None
        for k in range(CONV_WIDTH):
            term = cbuf[r0 + k * nb:r0 + k * nb + hm, :] * cw_ref[k:k + 1, :]
            xc = term if xc is None else xc + term
        coeffs.append(_gates(cb_ref[...] + xc, wg_ref, ba_ref, bx_ref, lam_ref))

    hcur = hst[...]
    for s_i, (a, b) in zip(subs, coeffs):
        r0 = s_i * hm
        for t in range(hsteps):
            hcur = a[t * nb:(t + 1) * nb, :] * hcur + b[t * nb:(t + 1) * nb, :]
            hsbuf[r0 + t * nb:r0 + (t + 1) * nb, :] = hcur
    hst[...] = hcur

    for rs, x, y_pool, u_gate in zip(rows, xs, y_pools, u_gates):
        y_lru = hsbuf[rs, :] * jax.nn.gelu(u_gate)
        o_ref[rs, :] = _mix_out(x, g, y_pool, y_lru, wout_ref, lng_ref, lnb_ref)

    @pl.when(pid == pl.num_programs(0) - 1)
    def _final_state_and_sample_rows():
        npool_ref[...] = zbuf[tm:tm + pool_hist, :].reshape(POOL_BUF, nb, POOL_WIDTH)
        nconv_ref[...] = cbuf[tm:tm + conv_hist, :].reshape(CONV_WIDTH - 1, nb, LRU_WIDTH)
        nh_ref[...] = hcur
        _mixer_sample_rows(*sample_in, weights, *sample_out)

    zbuf[0:pool_hist, :] = zbuf[tm:tm + pool_hist, :]
    cbuf[0:conv_hist, :] = cbuf[tm:tm + conv_hist, :]


def _mixer_call(x2d, xs, mod, wts, sample_states, new_p, new_s, *, layer, steps, n_prompt, round_next=()):
    assert len(wts) == N_MIXER_WEIGHTS
    rows = x2d.shape[0]
    n_sample = xs.shape[0]
    tm = steps * n_prompt
    n_grid = rows // tm
    rnd = [_round_specs(w, l, n_grid) for w, l, _ in round_next]
    kern = functools.partial(_mixer_kernel, steps=steps, nb=n_prompt, subtiles=MIXER_SUBTILES,
                             interleave=tuple(il for _, _, il in round_next))
    stacks = tuple(new_p) + tuple(new_s)
    n_in = 4 + len(wts) + 4 + len(sample_states) + len(rnd)
    any_spec = pl.BlockSpec(memory_space=pl.ANY)
    stack_specs = [_layer_block(a.shape, layer) for a in stacks]
    rows_spec = pl.BlockSpec((tm, D_MODEL), lambda i: (i, 0))
    return pl.pallas_call(
        kern,
        out_shape=[jax.ShapeDtypeStruct((rows, D_MODEL), F32)]
        + [jax.ShapeDtypeStruct(a.shape, F32) for a in new_p]
        + [jax.ShapeDtypeStruct(xs.shape, F32)]
        + [jax.ShapeDtypeStruct(a.shape, F32) for a in new_s]
        + [r[2] for r in rnd],
        grid=(n_grid,),
        in_specs=[rows_spec]
        + _mod_specs(layer, 1, False, n_sample, n_prompt)
        + [_resident(w, layer) for w in wts]
        + [_whole(xs.shape)] + _mod_specs(layer, 1, True, n_sample, n_prompt)
        + [_layer_block(a.shape, layer) for a in sample_states]
        + [r[0] for r in rnd]
        + [any_spec] * len(stacks),
        out_specs=[rows_spec] + stack_specs[:3] + [_whole(xs.shape)] + stack_specs[3:] + [r[1] for r in rnd],
        scratch_shapes=[
            pltpu.VMEM((POOL_BUF * n_prompt + tm, POOL_WIDTH), F32),
            pltpu.VMEM(((CONV_WIDTH - 1) * n_prompt + tm, LRU_WIDTH), F32),
            pltpu.VMEM((n_prompt, LRU_WIDTH), F32),
            pltpu.VMEM((tm, LRU_WIDTH), F32),
        ],
        input_output_aliases={n_in: 1, n_in + 1: 2, n_in + 2: 3, n_in + 3: 5, n_in + 4: 6, n_in + 5: 7},
        compiler_params=pltpu.CompilerParams(
            dimension_semantics=("arbitrary",),
            vmem_limit_bytes=MIXER_VMEM),
        name="mixer",
    )(x2d, mod, mod, mod, *wts, xs, mod, mod, mod, *sample_states, *[w for w, _, _ in round_next], *stacks)


def _lane_blocks(w):
    depth, heads, c, _ = w.shape
    per = LANES // c
    eye = jnp.eye(per, dtype=w.dtype)
    blocks = jnp.einsum('lbgij,gh->lbgihj', w.reshape(depth, heads // per, per, c, c), eye)
    return blocks.reshape(depth, heads // per, LANES, LANES)


def kernel(x_prompt, x_sample, state_pool, state_conv, state_rglru, c_prompt, c_sample, w_ada, b_ada, ffn1_wgu, ffn1_wdown, w_in, w_out, w_pool, pool_scale, conv_w, conv_b, w_rg_a, b_rg_a, w_rg_x, b_rg_x, lru_lambda, ffn2_wgu, ffn2_wdown, ln_g, ln_b):
    n_prompt, seq, _ = x_prompt.shape
    n_sample = x_sample.shape[0]
    assert x_sample.shape[1] == 1 and n_sample % n_prompt == 0 and n_prompt == SUBLANES
    assert seq % MIXER_STEPS == 0 and (seq * n_prompt) % FFN_TILE == 0

    row = lambda v: v.reshape(DEPTH, 1, v.shape[-1])
    wpool_bd = _lane_blocks(w_pool).astype(BF16)
    wgate_bd = jnp.concatenate([_lane_blocks(w_rg_a), _lane_blocks(w_rg_x)], axis=1).astype(BF16)

    mod, wgu_bf, wd_bf = _ada_call(jnp.concatenate([c_sample, c_prompt], axis=0), w_ada, b_ada,
                                   round_next=((ffn1_wgu, 0, True), (ffn1_wdown, 0, False)))

    xp = x_prompt
    xs = x_sample.reshape(n_sample, D_MODEL)
    hist_major = lambda a: jnp.swapaxes(a, 1, 2)
    sample_states = (hist_major(state_pool), hist_major(state_conv), state_rglru)
    new_p = (jnp.zeros((DEPTH, POOL_BUF, n_prompt, POOL_WIDTH), F32),
             jnp.zeros((DEPTH, CONV_WIDTH - 1, n_prompt, LRU_WIDTH), F32),
             jnp.zeros((DEPTH, n_prompt, LRU_WIDTH), F32))
    new_s = tuple(jnp.zeros(a.shape, F32) for a in sample_states)

    for l in range(DEPTH):
        ffn = functools.partial(_ffn_call, layer=l, tm=FFN_TILE, n_prompt=n_prompt)
        xp, xs, win_bf, wout_bf = ffn(xp, xs, mod, wgu_bf, wd_bf, ln_g, ln_b, sub=0, ln_idx=0,
                                      batch_major_in=(l == 0), round_next=((w_in, l, False), (w_out, l, False)))

        mixer_w = (win_bf, wpool_bd, row(pool_scale), conv_w, row(conv_b), wgate_bd,
                   row(b_rg_a), row(b_rg_x), row(lru_lambda), wout_bf, ln_g, ln_b)
        outs = _mixer_call(xp, xs, mod, mixer_w, sample_states, new_p, new_s, layer=l, steps=MIXER_STEPS,
                           n_prompt=n_prompt, round_next=((ffn2_wgu, l, True), (ffn2_wdown, l, False)))
        xp, new_p, xs, new_s, (wgu_bf, wd_bf) = outs[0], outs[1:4], outs[4], outs[5:8], outs[8:]

        last = l == DEPTH - 1
        outs = ffn(xp, xs, mod, wgu_bf, wd_bf, ln_g, ln_b, sub=2, ln_idx=2, batch_major_out=last,
                   round_next=() if last else ((ffn1_wgu, l + 1, True), (ffn1_wdown, l + 1, False)))
        xp, xs = outs[:2]
        if not last:
            wgu_bf, wd_bf = outs[2:]

    return (xp, xs.reshape(n_sample, 1, D_MODEL),
            hist_major(new_p[0]), hist_major(new_p[1]), new_p[2],
            hist_major(new_s[0]), hist_major(new_s[1]), new_s[2])
```

```python
import functools

import jax
import jax.numpy as jnp
from jax import lax
from jax.experimental import pallas as pl
from jax.experimental.pallas import tpu as pltpu

D_MODEL = 1024
DEPTH = 4
PAST_LEN = 16384
POOL_WIDTH = D_MODEL // 2
LRU_WIDTH = D_MODEL - POOL_WIDTH
IN_WIDTH = POOL_WIDTH + 2 * LRU_WIDTH
POOL_WINDOWS = (2, 4, 8, 16)
POOL_GROUP_DIM = POOL_WIDTH // len(POOL_WINDOWS)
POOL_BUF = max(POOL_WINDOWS) - 1
N_LRU_HEADS = 8
LRU_HEAD_DIM = LRU_WIDTH // N_LRU_HEADS
CONV_WIDTH = 4
LRU_C = 8.0
D_FF = 128 * int(round(8 * D_MODEL / 3 / 128))
N_MOD = 9
ALPHA = (2 * DEPTH) ** 0.25
LN_EPS = 1e-5

F32 = jnp.float32
BF16 = jnp.bfloat16

SUBLANES = 8
LANES = 128
BF16_SUBLANES = 16
MIB = 1024 * 1024
GU_BLOCKS = D_FF // LANES

FFN_TILE = 512
FFN_SUBTILES = 2
MIXER_STEPS = 64
MIXER_SUBTILES = 2
ADA_VMEM = 32 * MIB
FFN_VMEM = 56 * MIB
MIXER_VMEM = 56 * MIB


def _layer_norm(y, g, b):
    mu = jnp.mean(y, axis=-1, keepdims=True)
    d = y - mu
    var = jnp.mean(d * d, axis=-1, keepdims=True)
    return d * lax.rsqrt(var + LN_EPS) * g + b


def _sigmoid(x):
    return 0.5 * jnp.tanh(0.5 * x) + 0.5


def _silu(x):
    return x * _sigmoid(x)


def _softplus(x):
    return jnp.maximum(x, 0.0) + jnp.log1p(jnp.exp(-jnp.abs(x)))


def _per_batch(fn, x, *ms):
    rows, d = x.shape
    nb = ms[0].shape[0]
    if nb == rows:
        return fn(x, *ms)
    return fn(x.reshape(rows // nb, nb, d), *[m[None] for m in ms]).reshape(rows, d)


def _modulate(x, sh, sc):
    return _per_batch(lambda x_, sh_, sc_: x_ * (1.0 + sc_) + sh_, x, sh, sc)


def _gated_residual(x, g, y, gate_scale):
    return _per_batch(lambda y_, g_: (gate_scale * (1.0 + g_)) * y_, y, g) + ALPHA * x


def _mod_specs(layer, sub, is_sample, n_sample, n_prompt):
    rows, block = (n_sample, 0) if is_sample else (n_prompt, n_sample // n_prompt)
    return [pl.BlockSpec((None, rows, D_MODEL), functools.partial(lambda i, k: (layer, block, k), k=3 * sub + t))
            for t in range(3)]


def _resident(w, layer):
    if w.ndim == 2:
        return pl.BlockSpec(w.shape, lambda i: (0, 0), pipeline_mode=pl.Buffered(1))
    nd = w.ndim
    return pl.BlockSpec((None,) + tuple(w.shape[1:]), lambda i: (layer,) + (0,) * (nd - 1),
                        pipeline_mode=pl.Buffered(1))


def _layer_block(shape, layer):
    nd = len(shape)
    return pl.BlockSpec((None,) + tuple(shape[1:]), lambda i: (layer,) + (0,) * (nd - 1))


def _whole(shape):
    return pl.BlockSpec(tuple(shape), lambda i: (0,) * len(shape))


def _round_specs(w, layer, n_steps, step=lambda i: i):
    rows, cols = w.shape[1:]
    n_chunks = next(n for n in range(n_steps, 0, -1) if rows % (n * BF16_SUBLANES) == 0)
    chunk = rows // n_chunks
    idx = lambda *ids: jnp.minimum(step(*ids), n_chunks - 1)
    return (pl.BlockSpec((None, chunk, cols), lambda *ids: (layer, idx(*ids), 0)),
            pl.BlockSpec((chunk, cols), lambda *ids: (idx(*ids), 0)),
            jax.ShapeDtypeStruct((rows, cols), BF16))


def _round_chunks(src_refs, dst_refs, interleave):
    for src, dst, il in zip(src_refs, dst_refs, interleave):
        if il:
            for i in range(GU_BLOCKS):
                dst[:, 2 * i * LANES:(2 * i + 1) * LANES] = src[:, i * LANES:(i + 1) * LANES].astype(BF16)
                dst[:, (2 * i + 1) * LANES:(2 * i + 2) * LANES] = src[:, D_FF + i * LANES:D_FF + (i + 1) * LANES].astype(BF16)
        else:
            dst[...] = src[...].astype(BF16)


def _split(refs, *counts):
    out, pos = [], 0
    for c in counts:
        out.append(refs[pos:pos + c])
        pos += c
    out.append(refs[pos:])
    return out


def _ada_kernel(*refs, interleave):
    n_round = len(interleave)
    (c_ref, w_ref, b_ref), round_in, (o_ref,), round_out = _split(refs, 3, n_round, 1)
    _round_chunks(round_in, round_out, interleave)
    sc = _silu(c_ref[...]).astype(BF16)
    w = w_ref[...].astype(BF16)
    o_ref[...] = jnp.dot(sc, w, preferred_element_type=F32) + b_ref[...]


def _ada_call(c_all, w_ada, b_ada, round_next=()):
    n_rows = c_all.shape[0]
    n_out = w_ada.shape[-1]
    n_col = 8
    tn = n_out // n_col
    rnd = [_round_specs(w, l, DEPTH * n_col, step=lambda l_, j: l_ * n_col + j) for w, l, _ in round_next]
    return pl.pallas_call(
        functools.partial(_ada_kernel, interleave=tuple(il for _, _, il in round_next)),
        out_shape=[jax.ShapeDtypeStruct((DEPTH, n_rows, n_out), F32)] + [r[2] for r in rnd],
        grid=(DEPTH, n_col),
        in_specs=[
            pl.BlockSpec((n_rows, D_MODEL), lambda l, j: (0, 0)),
            pl.BlockSpec((None, D_MODEL, tn), lambda l, j: (l, 0, j)),
            pl.BlockSpec((None, 1, tn), lambda l, j: (l, 0, j)),
        ] + [r[0] for r in rnd],
        out_specs=[pl.BlockSpec((None, n_rows, tn), lambda l, j: (l, 0, j))] + [r[1] for r in rnd],
        compiler_params=pltpu.CompilerParams(
            dimension_semantics=("arbitrary", "arbitrary"),
            vmem_limit_bytes=ADA_VMEM),
        name="ada",
    )(c_all, w_ada, b_ada.reshape(DEPTH, 1, n_out), *[w for w, _, _ in round_next])


def _swiglu_act(gu):
    blocks = [_silu(gu[:, 2 * i * LANES:(2 * i + 1) * LANES]) * gu[:, (2 * i + 1) * LANES:(2 * i + 2) * LANES]
              for i in range(GU_BLOCKS)]
    return jnp.concatenate(blocks, axis=-1)


def _ffn_rows(x, sh, sc, g, wgu_ref, wd_ref, lng, lnb):
    h = _modulate(x, sh, sc).astype(BF16)
    gu = jnp.dot(h, wgu_ref[...], preferred_element_type=F32)
    act = _swiglu_act(gu).astype(BF16)
    acc = jnp.dot(act, wd_ref[...], preferred_element_type=F32)
    return _layer_norm(_gated_residual(x, g, acc, 0.5), lng, lnb)


def _ffn_kernel(*refs, ln_idx, subtiles, batch_major_in, batch_major_out, interleave):
    n_round = len(interleave)
    (x_ref, sh_ref, sc_ref, g_ref, wgu_ref, wd_ref, lng_ref, lnb_ref), sample_in, round_in, (o_ref, os_ref), \
        round_out, lbuf = _split(refs, 8, 4, n_round, 2, n_round)
    _round_chunks(round_in, round_out, interleave)
    lng, lnb = lng_ref[ln_idx:ln_idx + 1, :], lnb_ref[ln_idx:ln_idx + 1, :]
    sh, sc, g = sh_ref[...], sc_ref[...], g_ref[...]
    nb = sh.shape[0]
    rows = (x_ref.shape[0] * x_ref.shape[1] if batch_major_in else x_ref.shape[0]) // subtiles
    ts = rows // nb
    for s_i in range(subtiles):
        steps = slice(s_i * ts, (s_i + 1) * ts)
        if batch_major_in:
            for b in range(nb):
                lbuf[0][steps, b, :] = x_ref[b, steps, :]
            x = lbuf[0][steps, :, :].reshape(rows, D_MODEL)
        else:
            x = x_ref[s_i * rows:(s_i + 1) * rows, :]
        y = _ffn_rows(x, sh, sc, g, wgu_ref, wd_ref, lng, lnb)
        if batch_major_out:
            lbuf[0][steps, :, :] = y.reshape(ts, nb, D_MODEL)
            for b in range(nb):
                o_ref[b, steps, :] = lbuf[0][steps, b, :]
        else:
            o_ref[s_i * rows:(s_i + 1) * rows, :] = y

    @pl.when(pl.program_id(0) == pl.num_programs(0) - 1)
    def _sample_rows():
        xs_ref, shs_ref, scs_ref, gs_ref = sample_in
        os_ref[...] = _ffn_rows(xs_ref[...], shs_ref[...], scs_ref[...], gs_ref[...], wgu_ref, wd_ref, lng, lnb)


def _ffn_call(x, xs, mod, wgu, wd, ln_g, ln_b, *, layer, sub, ln_idx, tm, n_prompt,
              batch_major_in=False, batch_major_out=False, round_next=()):
    rows = x.shape[0] * x.shape[1] if batch_major_in else x.shape[0]
    n_sample = xs.shape[0]
    steps = tm // n_prompt
    n_grid = rows // tm
    major_spec = pl.BlockSpec((n_prompt, steps, D_MODEL), lambda i: (0, i, 0))
    rows_spec = pl.BlockSpec((tm, D_MODEL), lambda i: (i, 0))
    out_shape = (n_prompt, rows // n_prompt, D_MODEL) if batch_major_out else (rows, D_MODEL)
    rnd = [_round_specs(w, l, n_grid) for w, l, _ in round_next]
    kern = functools.partial(_ffn_kernel, ln_idx=ln_idx, subtiles=FFN_SUBTILES,
                             batch_major_in=batch_major_in, batch_major_out=batch_major_out,
                             interleave=tuple(il for _, _, il in round_next))
    return pl.pallas_call(
        kern,
        out_shape=[jax.ShapeDtypeStruct(out_shape, F32), jax.ShapeDtypeStruct(xs.shape, F32)] + [r[2] for r in rnd],
        grid=(n_grid,),
        in_specs=[major_spec if batch_major_in else rows_spec]
        + _mod_specs(layer, sub, False, n_sample, n_prompt)
        + [_resident(wgu, layer), _resident(wd, layer), _resident(ln_g, layer), _resident(ln_b, layer)]
        + [_whole(xs.shape)] + _mod_specs(layer, sub, True, n_sample, n_prompt)
        + [r[0] for r in rnd],
        out_specs=[major_spec if batch_major_out else rows_spec, _whole(xs.shape)] + [r[1] for r in rnd],
        scratch_shapes=[pltpu.VMEM((steps, n_prompt, D_MODEL), F32)] if batch_major_in or batch_major_out else [],
        compiler_params=pltpu.CompilerParams(
            dimension_semantics=("arbitrary",),
            vmem_limit_bytes=FFN_VMEM),
        name=f"ffn{sub}",
    )(x, mod, mod, mod, wgu, wd, ln_g, ln_b, xs, mod, mod, mod, *[w for w, _, _ in round_next])


def _grouped_dot(x, w_ref, first, n):
    c = w_ref.shape[-1]
    return jnp.concatenate([jnp.dot(x[:, i * c:(i + 1) * c], w_ref[first + i], preferred_element_type=F32)
                            for i in range(n)], axis=-1)


def _gates(xc, wg_ref, ba_ref, bx_ref, lam_ref):
    xb = xc.astype(BF16)
    n = wg_ref.shape[0] // 2
    r = _sigmoid(_grouped_dot(xb, wg_ref, 0, n) + ba_ref[...])
    i = _sigmoid(_grouped_dot(xb, wg_ref, n, n) + bx_ref[...])
    log_a = (-LRU_C * r) * _softplus(-lam_ref[...])
    a = jnp.exp(log_a)
    t = jnp.tanh(log_a)
    q = (-2.0 * t) / (1.0 - t)
    mult = jnp.where(q == 0.0, 0.0, q * lax.rsqrt(q))
    return a, mult * (i * xc)


def _mix_out(x, g, y_pool, y_lru, wout_ref, lng_ref, lnb_ref):
    mix = jnp.dot(y_pool.astype(BF16), wout_ref[:POOL_WIDTH, :], preferred_element_type=F32)
    mix = mix + jnp.dot(y_lru.astype(BF16), wout_ref[POOL_WIDTH:, :], preferred_element_type=F32)
    y = _gated_residual(x, g, mix, 1.0)
    return _layer_norm(y, lng_ref[1:2, :], lnb_ref[1:2, :])


N_MIXER_WEIGHTS = 12


def _mixer_sample_rows(x_ref, sh_ref, sc_ref, g_ref, pst_ref, cst_ref, h0_ref, weights,
                       o_ref, npool_ref, nconv_ref, nh_ref):
    (win_ref, wpool_ref, pscale_ref, cw_ref, cb_ref, wg_ref, ba_ref, bx_ref, lam_ref,
     wout_ref, lng_ref, lnb_ref) = weights
    x = x_ref[...]
    h = _modulate(x, sh_ref[...], sc_ref[...]).astype(BF16)
    proj = jnp.dot(h, win_ref[...], preferred_element_type=F32)
    u_pool = proj[:, :POOL_WIDTH]
    u_lru = proj[:, POOL_WIDTH:POOL_WIDTH + LRU_WIDTH]
    u_gate = proj[:, POOL_WIDTH + LRU_WIDTH:]

    pooled = []
    for gi, w in enumerate(POOL_WINDOWS):
        cols = slice(gi * POOL_GROUP_DIM, (gi + 1) * POOL_GROUP_DIM)
        s = u_pool[:, cols]
        for back in range(1, w):
            s = s + pst_ref[POOL_BUF - back, :, cols]
        cnt = float(min(PAST_LEN + 1, w))
        pooled.append(s / cnt - u_pool[:, cols])
    pooled = jnp.concatenate(pooled, axis=-1).astype(BF16)
    y_pool = _grouped_dot(pooled, wpool_ref, 0, len(POOL_WINDOWS)) * pscale_ref[...]
    npool_ref[0:POOL_BUF - 1] = pst_ref[1:POOL_BUF]
    npool_ref[POOL_BUF - 1] = u_pool

    xc = None
    for k in range(CONV_WIDTH - 1):
        term = cst_ref[k] * cw_ref[k:k + 1, :]
        xc = term if xc is None else xc + term
    xc = cb_ref[...] + (xc + u_lru * cw_ref[CONV_WIDTH - 1:CONV_WIDTH, :])
    nconv_ref[0:CONV_WIDTH - 2] = cst_ref[1:CONV_WIDTH - 1]
    nconv_ref[CONV_WIDTH - 2] = u_lru

    a, b = _gates(xc, wg_ref, ba_ref, bx_ref, lam_ref)
    hs = a * h0_ref[...] + b
    nh_ref[...] = hs
    y_lru = hs * jax.nn.gelu(u_gate)

    o_ref[...] = _mix_out(x, g_ref[...], y_pool, y_lru, wout_ref, lng_ref, lnb_ref)


def _mixer_kernel(*refs, steps, nb, subtiles, interleave):
    n_round = len(interleave)
    (x_ref, sh_ref, sc_ref, g_ref), weights, sample_in, round_in, _aliased, \
        (o_ref, npool_ref, nconv_ref, nh_ref), sample_out, round_out, (zbuf, cbuf, hst, hsbuf) = \
        _split(refs, 4, N_MIXER_WEIGHTS, 7, n_round, 6, 4, 4, n_round)
    (win_ref, wpool_ref, pscale_ref, cw_ref, cb_ref, wg_ref, ba_ref, bx_ref, lam_ref,
     wout_ref, lng_ref, lnb_ref) = weights
    _round_chunks(round_in, round_out, interleave)

    pid = pl.program_id(0)
    tm = steps * nb
    pool_hist = POOL_BUF * nb
    conv_hist = (CONV_WIDTH - 1) * nb

    @pl.when(pid == 0)
    def _new_sequences():
        zbuf[0:pool_hist, :] = jnp.zeros((pool_hist, POOL_WIDTH), F32)
        cbuf[0:conv_hist, :] = jnp.zeros((conv_hist, LRU_WIDTH), F32)
        hst[...] = jnp.zeros_like(hst)

    sh, sc, g = sh_ref[...], sc_ref[...], g_ref[...]
    hm, hsteps = tm // subtiles, steps // subtiles
    t_idx = lax.shift_right_logical(lax.broadcasted_iota(jnp.int32, (hm, POOL_GROUP_DIM), 0), nb.bit_length() - 1)
    subs = range(subtiles)
    rows = [slice(s_i * hm, (s_i + 1) * hm) for s_i in subs]

    xs = [x_ref[rs, :] for rs in rows]
    projs = [jnp.dot(_modulate(x, sh, sc).astype(BF16), win_ref[...], preferred_element_type=F32) for x in xs]
    u_gates = [p[:, POOL_WIDTH + LRU_WIDTH:] for p in projs]

    y_pools, coeffs = [], []
    for s_i, proj in zip(subs, projs):
        r0 = s_i * hm
        u_pool = proj[:, :POOL_WIDTH]
        u_lru = proj[:, POOL_WIDTH:POOL_WIDTH + LRU_WIDTH]

        zbuf[pool_hist + r0:pool_hist + r0 + hm, :] = u_pool
        pos1 = (pid * steps + s_i * hsteps + 1) + t_idx
        cur = zbuf[r0:r0 + pool_hist + hm, :]
        first, w = -POOL_BUF, 1
        pooled = []
        for gi, wnd in enumerate(POOL_WINDOWS):
            assert wnd == 2 * w
            n = cur.shape[0]
            cur = cur[w * nb:, :] + cur[:n - w * nb, :]
            first, w = first + w, wnd
            cols = slice(gi * POOL_GROUP_DIM, (gi + 1) * POOL_GROUP_DIM)
            s = cur[-first * nb:, :POOL_GROUP_DIM]
            pooled.append(s / jnp.minimum(pos1, wnd).astype(F32) - u_pool[:, cols])
            cur = cur[:, POOL_GROUP_DIM:]
        assert first == 0
        pooled = jnp.concatenate(pooled, axis=-1).astype(BF16)
        y_pools.append(_grouped_dot(pooled, wpool_ref, 0, len(POOL_WINDOWS)) * pscale_ref[...])

        cbuf[conv_hist + r0:conv_hist + r0 + hm, :] = u_lru
        xc = None
        for k in range(CONV_WIDTH):
            term = cbuf[r0 + k * nb:r0 + k * nb + hm, :] * cw_ref[k:k + 1, :]
            xc = term if xc is None else xc + term
        coeffs.append(_gates(cb_ref[...] + xc, wg_ref, ba_ref, bx_ref, lam_ref))

    hcur = hst[...]
    for s_i, (a, b) in zip(subs, coeffs):
        r0 = s_i * hm
        for t in range(hsteps):
            hcur = a[t * nb:(t + 1) * nb, :] * hcur + b[t * nb:(t + 1) * nb, :]
            hsbuf[r0 + t * nb:r0 + (t + 1) * nb, :] = hcur
    hst[...] = hcur

    for rs, x, y_pool, u_gate in zip(rows, xs, y_pools, u_gates):
        y_lru = hsbuf[rs, :] * jax.nn.gelu(u_gate)
        o_ref[rs, :] = _mix_out(x, g, y_pool, y_lru, wout_ref, lng_ref, lnb_ref)

    @pl.when(pid == pl.num_programs(0) - 1)
    def _final_state_and_sample_rows():
        npool_ref[...] = zbuf[tm:tm + pool_hist, :].reshape(POOL_BUF, nb, POOL_WIDTH)
        nconv_ref[...] = cbuf[tm:tm + conv_hist, :].reshape(CONV_WIDTH - 1, nb, LRU_WIDTH)
        nh_ref[...] = hcur
        _mixer_sample_rows(*sample_in, weights, *sample_out)

    zbuf[0:pool_hist, :] = zbuf[tm:tm + pool_hist, :]
    cbuf[0:conv_hist, :] = cbuf[tm:tm + conv_hist, :]


def _mixer_call(x2d, xs, mod, wts, sample_states, new_p, new_s, *, layer, steps, n_prompt, round_next=()):
    assert len(wts) == N_MIXER_WEIGHTS
    rows = x2d.shape[0]
    n_sample = xs.shape[0]
    tm = steps * n_prompt
    n_grid = rows // tm
    rnd = [_round_specs(w, l, n_grid) for w, l, _ in round_next]
    kern = functools.partial(_mixer_kernel, steps=steps, nb=n_prompt, subtiles=MIXER_SUBTILES,
                             interleave=tuple(il for _, _, il in round_next))
    stacks = tuple(new_p) + tuple(new_s)
    n_in = 4 + len(wts) + 4 + len(sample_states) + len(rnd)
    any_spec = pl.BlockSpec(memory_space=pl.ANY)
    stack_specs = [_layer_block(a.shape, layer) for a in stacks]
    rows_spec = pl.BlockSpec((tm, D_MODEL), lambda i: (i, 0))
    return pl.pallas_call(
        kern,
        out_shape=[jax.ShapeDtypeStruct((rows, D_MODEL), F32)]
        + [jax.ShapeDtypeStruct(a.shape, F32) for a in new_p]
        + [jax.ShapeDtypeStruct(xs.shape, F32)]
        + [jax.ShapeDtypeStruct(a.shape, F32) for a in new_s]
        + [r[2] for r in rnd],
        grid=(n_grid,),
        in_specs=[rows_spec]
        + _mod_specs(layer, 1, False, n_sample, n_prompt)
        + [_resident(w, layer) for w in wts]
        + [_whole(xs.shape)] + _mod_specs(layer, 1, True, n_sample, n_prompt)
        + [_layer_block(a.shape, layer) for a in sample_states]
        + [r[0] for r in rnd]
        + [any_spec] * len(stacks),
        out_specs=[rows_spec] + stack_specs[:3] + [_whole(xs.shape)] + stack_specs[3:] + [r[1] for r in rnd],
        scratch_shapes=[
            pltpu.VMEM((POOL_BUF * n_prompt + tm, POOL_WIDTH), F32),
            pltpu.VMEM(((CONV_WIDTH - 1) * n_prompt + tm, LRU_WIDTH), F32),
            pltpu.VMEM((n_prompt, LRU_WIDTH), F32),
            pltpu.VMEM((tm, LRU_WIDTH), F32),
        ],
        input_output_aliases={n_in: 1, n_in + 1: 2, n_in + 2: 3, n_in + 3: 5, n_in + 4: 6, n_in + 5: 7},
        compiler_params=pltpu.CompilerParams(
            dimension_semantics=("arbitrary",),
            vmem_limit_bytes=MIXER_VMEM),
        name="mixer",
    )(x2d, mod, mod, mod, *wts, xs, mod, mod, mod, *sample_states, *[w for w, _, _ in round_next], *stacks)


def _lane_blocks(w):
    depth, heads, c, _ = w.shape
    per = LANES // c
    eye = jnp.eye(per, dtype=w.dtype)
    blocks = jnp.einsum('lbgij,gh->lbgihj', w.reshape(depth, heads // per, per, c, c), eye)
    return blocks.reshape(depth, heads // per, LANES, LANES)


def kernel(x_prompt, x_sample, state_pool, state_conv, state_rglru, c_prompt, c_sample, w_ada, b_ada, ffn1_wgu, ffn1_wdown, w_in, w_out, w_pool, pool_scale, conv_w, conv_b, w_rg_a, b_rg_a, w_rg_x, b_rg_x, lru_lambda, ffn2_wgu, ffn2_wdown, ln_g, ln_b):
    n_prompt, seq, _ = x_prompt.shape
    n_sample = x_sample.shape[0]
    assert x_sample.shape[1] == 1 and n_sample % n_prompt == 0 and n_prompt == SUBLANES
    assert seq % MIXER_STEPS == 0 and (seq * n_prompt) % FFN_TILE == 0

    row = lambda v: v.reshape(DEPTH, 1, v.shape[-1])
    wpool_bd = _lane_blocks(w_pool).astype(BF16)
    wgate_bd = jnp.concatenate([_lane_blocks(w_rg_a), _lane_blocks(w_rg_x)], axis=1).astype(BF16)

    mod, wgu_bf, wd_bf = _ada_call(jnp.concatenate([c_sample, c_prompt], axis=0), w_ada, b_ada,
                                   round_next=((ffn1_wgu, 0, True), (ffn1_wdown, 0, False)))

    xp = x_prompt
    xs = x_sample.reshape(n_sample, D_MODEL)
    hist_major = lambda a: jnp.swapaxes(a, 1, 2)
    sample_states = (hist_major(state_pool), hist_major(state_conv), state_rglru)
    new_p = (jnp.zeros((DEPTH, POOL_BUF, n_prompt, POOL_WIDTH), F32),
             jnp.zeros((DEPTH, CONV_WIDTH - 1, n_prompt, LRU_WIDTH), F32),
             jnp.zeros((DEPTH, n_prompt, LRU_WIDTH), F32))
    new_s = tuple(jnp.zeros(a.shape, F32) for a in sample_states)

    for l in range(DEPTH):
        ffn = functools.partial(_ffn_call, layer=l, tm=FFN_TILE, n_prompt=n_prompt)
        xp, xs, win_bf, wout_bf = ffn(xp, xs, mod, wgu_bf, wd_bf, ln_g, ln_b, sub=0, ln_idx=0,
                                      batch_major_in=(l == 0), round_next=((w_in, l, False), (w_out, l, False)))

        mixer_w = (win_bf, wpool_bd, row(pool_scale), conv_w, row(conv_b), wgate_bd,
                   row(b_rg_a), row(b_rg_x), row(lru_lambda), wout_bf, ln_g, ln_b)
        outs = _mixer_call(xp, xs, mod, mixer_w, sample_states, new_p, new_s, layer=l, steps=MIXER_STEPS,
                           n_prompt=n_prompt, round_next=((ffn2_wgu, l, True), (ffn2_wdown, l, False)))
        xp, new_p, xs, new_s, (wgu_bf, wd_bf) = outs[0], outs[1:4], outs[4], outs[5:8], outs[8:]

        last = l == DEPTH - 1
        outs = ffn(xp, xs, mod, wgu_bf, wd_bf, ln_g, ln_b, sub=2, ln_idx=2, batch_major_out=last,
                   round_next=() if last else ((ffn1_wgu, l + 1, True), (ffn1_wdown, l + 1, False)))
        xp, xs = outs[:2]
        if not last:
            wgu_bf, wd_bf = outs[2:]

    return (xp, xs.reshape(n_sample, 1, D_MODEL),
            hist_major(new_p[0]), hist_major(new_p[1]), new_p[2],
            hist_major(new_s[0]), hist_major(new_s[1]), new_s[2])
```

```python
import functools

import jax
import jax.numpy as jnp
from jax import lax
from jax.experimental import pallas as pl
from jax.experimental.pallas import tpu as pltpu

D_MODEL = 1024
DEPTH = 4
PAST_LEN = 16384
POOL_WIDTH = D_MODEL // 2
LRU_WIDTH = D_MODEL - POOL_WIDTH
IN_WIDTH = POOL_WIDTH + 2 * LRU_WIDTH
POOL_WINDOWS = (2, 4, 8, 16)
POOL_GROUP_DIM = POOL_WIDTH // len(POOL_WINDOWS)
POOL_BUF = max(POOL_WINDOWS) - 1
N_LRU_HEADS = 8
LRU_HEAD_DIM = LRU_WIDTH // N_LRU_HEADS
CONV_WIDTH = 4
LRU_C = 8.0
D_FF = 128 * int(round(8 * D_MODEL / 3 / 128))
N_MOD = 9
ALPHA = (2 * DEPTH) ** 0.25
LN_EPS = 1e-5

F32 = jnp.float32
BF16 = jnp.bfloat16

SUBLANES = 8
LANES = 128
BF16_SUBLANES = 16
MIB = 1024 * 1024
GU_BLOCKS = D_FF // LANES

FFN_TILE = 512
FFN_SUBTILES = 2
MIXER_STEPS = 64
MIXER_SUBTILES = 1
ADA_VMEM = 32 * MIB
FFN_VMEM = 56 * MIB
MIXER_VMEM = 56 * MIB


def _layer_norm(y, g, b):
    mu = jnp.mean(y, axis=-1, keepdims=True)
    d = y - mu
    var = jnp.mean(d * d, axis=-1, keepdims=True)
    return d * lax.rsqrt(var + LN_EPS) * g + b


def _sigmoid(x):
    return 0.5 * jnp.tanh(0.5 * x) + 0.5


def _silu(x):
    return x * _sigmoid(x)


def _softplus(x):
    return jnp.maximum(x, 0.0) + jnp.log1p(jnp.exp(-jnp.abs(x)))


def _per_batch(fn, x, *ms):
    rows, d = x.shape
    nb = ms[0].shape[0]
    if nb == rows:
        return fn(x, *ms)
    return fn(x.reshape(rows // nb, nb, d), *[m[None] for m in ms]).reshape(rows, d)


def _modulate(x, sh, sc):
    return _per_batch(lambda x_, sh_, sc_: x_ * (1.0 + sc_) + sh_, x, sh, sc)


def _gated_residual(x, g, y, gate_scale):
    return _per_batch(lambda y_, g_: (gate_scale * (1.0 + g_)) * y_, y, g) + ALPHA * x


def _mod_specs(sub, is_sample, n_sample, n_prompt):
    rows, block = (n_sample, 0) if is_sample else (n_prompt, n_sample // n_prompt)
    return [pl.BlockSpec((rows, D_MODEL), functools.partial(lambda i, k: (block, k), k=3 * sub + t))
            for t in range(3)]


def _resident(w, layer):
    if w.ndim == 2:
        return pl.BlockSpec(w.shape, lambda i: (0, 0), pipeline_mode=pl.Buffered(1))
    nd = w.ndim
    return pl.BlockSpec((None,) + tuple(w.shape[1:]), lambda i: (layer,) + (0,) * (nd - 1),
                        pipeline_mode=pl.Buffered(1))


def _layer_block(shape, layer):
    nd = len(shape)
    return pl.BlockSpec((None,) + tuple(shape[1:]), lambda i: (layer,) + (0,) * (nd - 1))


def _whole(shape):
    return pl.BlockSpec(tuple(shape), lambda i: (0,) * len(shape))


def _round_specs(w, layer, n_steps, step=lambda i: i):
    rows, cols = w.shape[1:]
    n_chunks = next(n for n in range(n_steps, 0, -1) if rows % (n * BF16_SUBLANES) == 0)
    chunk = rows // n_chunks
    idx = lambda *ids: jnp.minimum(step(*ids), n_chunks - 1)
    return (pl.BlockSpec((None, chunk, cols), lambda *ids: (layer, idx(*ids), 0)),
            pl.BlockSpec((chunk, cols), lambda *ids: (idx(*ids), 0)),
            jax.ShapeDtypeStruct((rows, cols), BF16))


def _round_chunks(src_refs, dst_refs, interleave):
    for src, dst, il in zip(src_refs, dst_refs, interleave):
        if il:
            for i in range(GU_BLOCKS):
                dst[:, 2 * i * LANES:(2 * i + 1) * LANES] = src[:, i * LANES:(i + 1) * LANES].astype(BF16)
                dst[:, (2 * i + 1) * LANES:(2 * i + 2) * LANES] = src[:, D_FF + i * LANES:D_FF + (i + 1) * LANES].astype(BF16)
        else:
            dst[...] = src[...].astype(BF16)


def _split(refs, *counts):
    out, pos = [], 0
    for c in counts:
        out.append(refs[pos:pos + c])
        pos += c
    out.append(refs[pos:])
    return out


ADA_COLS = 3 * LANES


def _ada_tile(c_ref, w_ref, b_ref, o_ref):
    sc = _silu(c_ref[...]).astype(BF16)
    o_ref[...] = jnp.dot(sc, w_ref[...].astype(BF16), preferred_element_type=F32) + b_ref[...]


def _ada_specs(c_all, w_ada, layer, tn, n_steps):
    n_rows, n_out = c_all.shape[0], w_ada.shape[-1]
    n_tiles = n_out // tn
    assert n_tiles <= n_steps and n_out % tn == 0
    idx = lambda i: jnp.minimum(i, n_tiles - 1)
    return ([_whole(c_all.shape),
             pl.BlockSpec((None, D_MODEL, tn), lambda i: (layer, 0, idx(i))),
             pl.BlockSpec((None, 1, tn), lambda i: (layer, 0, idx(i)))],
            pl.BlockSpec((n_rows, tn), lambda i: (0, idx(i))),
            jax.ShapeDtypeStruct((n_rows, n_out), F32))


def _ada_kernel(*refs, interleave):
    n_round = len(interleave)
    ada_in, round_in, (o_ref,), round_out = _split(refs, 3, n_round, 1)
    _round_chunks(round_in, round_out, interleave)
    _ada_tile(*ada_in, o_ref)


def _ada_call(c_all, w_ada, b_ada, layer, round_next=()):
    n_col = 8
    ada_in, ada_out, ada_shape = _ada_specs(c_all, w_ada, layer, w_ada.shape[-1] // n_col, n_col)
    rnd = [_round_specs(w, l, n_col) for w, l, _ in round_next]
    return pl.pallas_call(
        functools.partial(_ada_kernel, interleave=tuple(il for _, _, il in round_next)),
        out_shape=[ada_shape] + [r[2] for r in rnd],
        grid=(n_col,),
        in_specs=ada_in + [r[0] for r in rnd],
        out_specs=[ada_out] + [r[1] for r in rnd],
        compiler_params=pltpu.CompilerParams(
            dimension_semantics=("arbitrary",),
            vmem_limit_bytes=ADA_VMEM),
        name="ada",
    )(c_all, w_ada, b_ada, *[w for w, _, _ in round_next])


def _swiglu_act(gu):
    blocks = [_silu(gu[:, 2 * i * LANES:(2 * i + 1) * LANES]) * gu[:, (2 * i + 1) * LANES:(2 * i + 2) * LANES]
              for i in range(GU_BLOCKS)]
    return jnp.concatenate(blocks, axis=-1)


def _ffn_rows(x, sh, sc, g, wgu_ref, wd_ref, lng, lnb):
    h = _modulate(x, sh, sc).astype(BF16)
    gu = jnp.dot(h, wgu_ref[...], preferred_element_type=F32)
    act = _swiglu_act(gu).astype(BF16)
    acc = jnp.dot(act, wd_ref[...], preferred_element_type=F32)
    return _layer_norm(_gated_residual(x, g, acc, 0.5), lng, lnb)


def _ffn_kernel(*refs, ln_idx, subtiles, batch_major_in, batch_major_out, interleave, next_mod):
    n_round, n_ada = len(interleave), 3 if next_mod else 0
    (x_ref, sh_ref, sc_ref, g_ref, wgu_ref, wd_ref, lng_ref, lnb_ref), sample_in, ada_in, round_in, \
        (o_ref, os_ref), ada_out, round_out, lbuf = _split(refs, 8, 4, n_ada, n_round, 2, n_ada // 3, n_round)
    _round_chunks(round_in, round_out, interleave)
    if next_mod:
        _ada_tile(*ada_in, *ada_out)
    lng, lnb = lng_ref[ln_idx:ln_idx + 1, :], lnb_ref[ln_idx:ln_idx + 1, :]
    sh, sc, g = sh_ref[...], sc_ref[...], g_ref[...]
    nb = sh.shape[0]
    rows = (x_ref.shape[0] * x_ref.shape[1] if batch_major_in else x_ref.shape[0]) // subtiles
    ts = rows // nb
    for s_i in range(subtiles):
        steps = slice(s_i * ts, (s_i + 1) * ts)
        if batch_major_in:
            for b in range(nb):
                lbuf[0][steps, b, :] = x_ref[b, steps, :]
            x = lbuf[0][steps, :, :].reshape(rows, D_MODEL)
        else:
            x = x_ref[s_i * rows:(s_i + 1) * rows, :]
        y = _ffn_rows(x, sh, sc, g, wgu_ref, wd_ref, lng, lnb)
        if batch_major_out:
            lbuf[0][steps, :, :] = y.reshape(ts, nb, D_MODEL)
            for b in range(nb):
                o_ref[b, steps, :] = lbuf[0][steps, b, :]
        else:
            o_ref[s_i * rows:(s_i + 1) * rows, :] = y

    @pl.when(pl.program_id(0) == pl.num_programs(0) - 1)
    def _sample_rows():
        xs_ref, shs_ref, scs_ref, gs_ref = sample_in
        os_ref[...] = _ffn_rows(xs_ref[...], shs_ref[...], scs_ref[...], gs_ref[...], wgu_ref, wd_ref, lng, lnb)


def _ffn_call(x, xs, mod, wgu, wd, ln_g, ln_b, *, layer, sub, ln_idx, tm, n_prompt,
              batch_major_in=False, batch_major_out=False, round_next=(), next_ada=None):
    rows = x.shape[0] * x.shape[1] if batch_major_in else x.shape[0]
    n_sample = xs.shape[0]
    steps = tm // n_prompt
    n_grid = rows // tm
    major_spec = pl.BlockSpec((n_prompt, steps, D_MODEL), lambda i: (0, i, 0))
    rows_spec = pl.BlockSpec((tm, D_MODEL), lambda i: (i, 0))
    out_shape = (n_prompt, rows // n_prompt, D_MODEL) if batch_major_out else (rows, D_MODEL)
    rnd = [_round_specs(w, l, n_grid) for w, l, _ in round_next]
    ada_in, ada_out, ada_shape, ada_args = [], [], [], []
    if next_ada is not None:
        c_all, w_ada, b_ada, ada_layer = next_ada
        ada_in, out_spec, shape = _ada_specs(c_all, w_ada, ada_layer, ADA_COLS, n_grid)
        ada_out, ada_shape, ada_args = [out_spec], [shape], [c_all, w_ada, b_ada]
    kern = functools.partial(_ffn_kernel, ln_idx=ln_idx, subtiles=FFN_SUBTILES,
                             batch_major_in=batch_major_in, batch_major_out=batch_major_out,
                             interleave=tuple(il for _, _, il in round_next), next_mod=next_ada is not None)
    return pl.pallas_call(
        kern,
        out_shape=[jax.ShapeDtypeStruct(out_shape, F32), jax.ShapeDtypeStruct(xs.shape, F32)] + ada_shape
        + [r[2] for r in rnd],
        grid=(n_grid,),
        in_specs=[major_spec if batch_major_in else rows_spec]
        + _mod_specs(sub, False, n_sample, n_prompt)
        + [_resident(wgu, layer), _resident(wd, layer), _resident(ln_g, layer), _resident(ln_b, layer)]
        + [_whole(xs.shape)] + _mod_specs(sub, True, n_sample, n_prompt)
        + ada_in + [r[0] for r in rnd],
        out_specs=[major_spec if batch_major_out else rows_spec, _whole(xs.shape)] + ada_out + [r[1] for r in rnd],
        scratch_shapes=[pltpu.VMEM((steps, n_prompt, D_MODEL), F32)] if batch_major_in or batch_major_out else [],
        compiler_params=pltpu.CompilerParams(
            dimension_semantics=("arbitrary",),
            vmem_limit_bytes=FFN_VMEM),
        name=f"ffn{sub}",
    )(x, mod, mod, mod, wgu, wd, ln_g, ln_b, xs, mod, mod, mod, *ada_args, *[w for w, _, _ in round_next])


def _grouped_dot(x, w_ref, first, n):
    c = w_ref.shape[-1]
    return jnp.concatenate([jnp.dot(x[:, i * c:(i + 1) * c], w_ref[first + i], preferred_element_type=F32)
                            for i in range(n)], axis=-1)


def _gates(xc, wg_ref, ba_ref, bx_ref, lam_ref):
    xb = xc.astype(BF16)
    n = wg_ref.shape[0] // 2
    r = _sigmoid(_grouped_dot(xb, wg_ref, 0, n) + ba_ref[...])
    i = _sigmoid(_grouped_dot(xb, wg_ref, n, n) + bx_ref[...])
    log_a = (-LRU_C * r) * _softplus(-lam_ref[...])
    a = jnp.exp(log_a)
    t = jnp.tanh(log_a)
    q = (-2.0 * t) / (1.0 - t)
    mult = jnp.where(q == 0.0, 0.0, q * lax.rsqrt(q))
    return a, mult * (i * xc)


def _mix_out(x, g, y_pool, y_lru, wout_ref, lng_ref, lnb_ref):
    mix = jnp.dot(y_pool.astype(BF16), wout_ref[:POOL_WIDTH, :], preferred_element_type=F32)
    mix = mix + jnp.dot(y_lru.astype(BF16), wout_ref[POOL_WIDTH:, :], preferred_element_type=F32)
    y = _gated_residual(x, g, mix, 1.0)
    return _layer_norm(y, lng_ref[1:2, :], lnb_ref[1:2, :])


N_MIXER_WEIGHTS = 12


def _mixer_sample_rows(x_ref, sh_ref, sc_ref, g_ref, pst_ref, cst_ref, h0_ref, weights,
                       o_ref, npool_ref, nconv_ref, nh_ref):
    (win_ref, wpool_ref, pscale_ref, cw_ref, cb_ref, wg_ref, ba_ref, bx_ref, lam_ref,
     wout_ref, lng_ref, lnb_ref) = weights
    x = x_ref[...]
    h = _modulate(x, sh_ref[...], sc_ref[...]).astype(BF16)
    proj = jnp.dot(h, win_ref[...], preferred_element_type=F32)
    u_pool = proj[:, :POOL_WIDTH]
    u_lru = proj[:, POOL_WIDTH:POOL_WIDTH + LRU_WIDTH]
    u_gate = proj[:, POOL_WIDTH + LRU_WIDTH:]

    pooled = []
    for gi, w in enumerate(POOL_WINDOWS):
        cols = slice(gi * POOL_GROUP_DIM, (gi + 1) * POOL_GROUP_DIM)
        s = u_pool[:, cols]
        for back in range(1, w):
            s = s + pst_ref[POOL_BUF - back, :, cols]
        cnt = float(min(PAST_LEN + 1, w))
        pooled.append(s / cnt - u_pool[:, cols])
    pooled = jnp.concatenate(pooled, axis=-1).astype(BF16)
    y_pool = _grouped_dot(pooled, wpool_ref, 0, len(POOL_WINDOWS)) * pscale_ref[...]
    npool_ref[0:POOL_BUF - 1] = pst_ref[1:POOL_BUF]
    npool_ref[POOL_BUF - 1] = u_pool

    xc = None
    for k in range(CONV_WIDTH - 1):
        term = cst_ref[k] * cw_ref[k:k + 1, :]
        xc = term if xc is None else xc + term
    xc = cb_ref[...] + (xc + u_lru * cw_ref[CONV_WIDTH - 1:CONV_WIDTH, :])
    nconv_ref[0:CONV_WIDTH - 2] = cst_ref[1:CONV_WIDTH - 1]
    nconv_ref[CONV_WIDTH - 2] = u_lru

    a, b = _gates(xc, wg_ref, ba_ref, bx_ref, lam_ref)
    hs = a * h0_ref[...] + b
    nh_ref[...] = hs
    y_lru = hs * jax.nn.gelu(u_gate)

    o_ref[...] = _mix_out(x, g_ref[...], y_pool, y_lru, wout_ref, lng_ref, lnb_ref)


def _mixer_kernel(*refs, steps, nb, subtiles, interleave):
    n_round = len(interleave)
    (x_ref, sh_ref, sc_ref, g_ref), weights, sample_in, round_in, _aliased, \
        (o_ref, npool_ref, nconv_ref, nh_ref), sample_out, round_out, (zbuf, cbuf, hst, hsbuf) = \
        _split(refs, 4, N_MIXER_WEIGHTS, 7, n_round, 6, 4, 4, n_round)
    (win_ref, wpool_ref, pscale_ref, cw_ref, cb_ref, wg_ref, ba_ref, bx_ref, lam_ref,
     wout_ref, lng_ref, lnb_ref) = weights
    _round_chunks(round_in, round_out, interleave)

    pid = pl.program_id(0)
    tm = steps * nb
    pool_hist = POOL_BUF * nb
    conv_hist = (CONV_WIDTH - 1) * nb

    @pl.when(pid == 0)
    def _new_sequences():
        zbuf[0:pool_hist, :] = jnp.zeros((pool_hist, POOL_WIDTH), F32)
        cbuf[0:conv_hist, :] = jnp.zeros((conv_hist, LRU_WIDTH), F32)
        hst[...] = jnp.zeros_like(hst)

    sh, sc, g = sh_ref[...], sc_ref[...], g_ref[...]
    hm, hsteps = tm // subtiles, steps // subtiles
    t_idx = lax.shift_right_logical(lax.broadcasted_iota(jnp.int32, (hm, POOL_GROUP_DIM), 0), nb.bit_length() - 1)
    subs = range(subtiles)
    rows = [slice(s_i * hm, (s_i + 1) * hm) for s_i in subs]

    xs = [x_ref[rs, :] for rs in rows]
    projs = [jnp.dot(_modulate(x, sh, sc).astype(BF16), win_ref[...], preferred_element_type=F32) for x in xs]
    u_gates = [p[:, POOL_WIDTH + LRU_WIDTH:] for p in projs]

    y_pools, coeffs = [], []
    for s_i, proj in zip(subs, projs):
        r0 = s_i * hm
        u_pool = proj[:, :POOL_WIDTH]
        u_lru = proj[:, POOL_WIDTH:POOL_WIDTH + LRU_WIDTH]

        zbuf[pool_hist + r0:pool_hist + r0 + hm, :] = u_pool
        pos1 = (pid * steps + s_i * hsteps + 1) + t_idx
        cur = zbuf[r0:r0 + pool_hist + hm, :]
        first, w = -POOL_BUF, 1
        pooled = []
        for gi, wnd in enumerate(POOL_WINDOWS):
            assert wnd == 2 * w
            n = cur.shape[0]
            cur = cur[w * nb:, :] + cur[:n - w * nb, :]
            first, w = first + w, wnd
            cols = slice(gi * POOL_GROUP_DIM, (gi + 1) * POOL_GROUP_DIM)
            s = cur[-first * nb:, :POOL_GROUP_DIM]
            pooled.append(s / jnp.minimum(pos1, wnd).astype(F32) - u_pool[:, cols])
            cur = cur[:, POOL_GROUP_DIM:]
        assert first == 0
        pooled = jnp.concatenate(pooled, axis=-1).astype(BF16)
        y_pools.append(_grouped_dot(pooled, wpool_ref, 0, len(POOL_WINDOWS)) * pscale_ref[...])

        cbuf[conv_hist + r0:conv_hist + r0 + hm, :] = u_lru
        xc = None
        for k in range(CONV_WIDTH):
            term = cbuf[r0 + k * nb:r0 + k * nb + hm, :] * cw_ref[k:k + 1, :]
            xc = term if xc is None else xc + term
        coeffs.append(_gates(cb_ref[...] + xc, wg_ref, ba_ref, bx_ref, lam_ref))

    hcur = hst[...]
    for s_i, (a, b) in zip(subs, coeffs):
        r0 = s_i * hm
        for t in range(hsteps):
            hcur = a[t * nb:(t + 1) * nb, :] * hcur + b[t * nb:(t + 1) * nb, :]
            hsbuf[r0 + t * nb:r0 + (t + 1) * nb, :] = hcur
    hst[...] = hcur

    for rs, x, y_pool, u_gate in zip(rows, xs, y_pools, u_gates):
        y_lru = hsbuf[rs, :] * jax.nn.gelu(u_gate)
        o_ref[rs, :] = _mix_out(x, g, y_pool, y_lru, wout_ref, lng_ref, lnb_ref)

    @pl.when(pid == pl.num_programs(0) - 1)
    def _final_state_and_sample_rows():
        npool_ref[...] = zbuf[tm:tm + pool_hist, :].reshape(POOL_BUF, nb, POOL_WIDTH)
        nconv_ref[...] = cbuf[tm:tm + conv_hist, :].reshape(CONV_WIDTH - 1, nb, LRU_WIDTH)
        nh_ref[...] = hcur
        _mixer_sample_rows(*sample_in, weights, *sample_out)

    zbuf[0:pool_hist, :] = zbuf[tm:tm + pool_hist, :]
    cbuf[0:conv_hist, :] = cbuf[tm:tm + conv_hist, :]


def _mixer_call(x2d, xs, mod, wts, sample_states, new_p, new_s, *, layer, steps, n_prompt, round_next=()):
    assert len(wts) == N_MIXER_WEIGHTS
    rows = x2d.shape[0]
    n_sample = xs.shape[0]
    tm = steps * n_prompt
    n_grid = rows // tm
    rnd = [_round_specs(w, l, n_grid) for w, l, _ in round_next]
    kern = functools.partial(_mixer_kernel, steps=steps, nb=n_prompt, subtiles=MIXER_SUBTILES,
                             interleave=tuple(il for _, _, il in round_next))
    stacks = tuple(new_p) + tuple(new_s)
    n_in = 4 + len(wts) + 4 + len(sample_states) + len(rnd)
    any_spec = pl.BlockSpec(memory_space=pl.ANY)
    stack_specs = [_layer_block(a.shape, layer) for a in stacks]
    rows_spec = pl.BlockSpec((tm, D_MODEL), lambda i: (i, 0))
    return pl.pallas_call(
        kern,
        out_shape=[jax.ShapeDtypeStruct((rows, D_MODEL), F32)]
        + [jax.ShapeDtypeStruct(a.shape, F32) for a in new_p]
        + [jax.ShapeDtypeStruct(xs.shape, F32)]
        + [jax.ShapeDtypeStruct(a.shape, F32) for a in new_s]
        + [r[2] for r in rnd],
        grid=(n_grid,),
        in_specs=[rows_spec]
        + _mod_specs(1, False, n_sample, n_prompt)
        + [_resident(w, layer) for w in wts]
        + [_whole(xs.shape)] + _mod_specs(1, True, n_sample, n_prompt)
        + [_layer_block(a.shape, layer) for a in sample_states]
        + [r[0] for r in rnd]
        + [any_spec] * len(stacks),
        out_specs=[rows_spec] + stack_specs[:3] + [_whole(xs.shape)] + stack_specs[3:] + [r[1] for r in rnd],
        scratch_shapes=[
            pltpu.VMEM((POOL_BUF * n_prompt + tm, POOL_WIDTH), F32),
            pltpu.VMEM(((CONV_WIDTH - 1) * n_prompt + tm, LRU_WIDTH), F32),
            pltpu.VMEM((n_prompt, LRU_WIDTH), F32),
            pltpu.VMEM((tm, LRU_WIDTH), F32),
        ],
        input_output_aliases={n_in: 1, n_in + 1: 2, n_in + 2: 3, n_in + 3: 5, n_in + 4: 6, n_in + 5: 7},
        compiler_params=pltpu.CompilerParams(
            dimension_semantics=("arbitrary",),
            vmem_limit_bytes=MIXER_VMEM),
        name="mixer",
    )(x2d, mod, mod, mod, *wts, xs, mod, mod, mod, *sample_states, *[w for w, _, _ in round_next], *stacks)


def _lane_blocks(w):
    depth, heads, c, _ = w.shape
    per = LANES // c
    eye = jnp.eye(per, dtype=w.dtype)
    blocks = jnp.einsum('lbgij,gh->lbgihj', w.reshape(depth, heads // per, per, c, c), eye)
    return blocks.reshape(depth, heads // per, LANES, LANES)


def kernel(x_prompt, x_sample, state_pool, state_conv, state_rglru, c_prompt, c_sample, w_ada, b_ada, ffn1_wgu, ffn1_wdown, w_in, w_out, w_pool, pool_scale, conv_w, conv_b, w_rg_a, b_rg_a, w_rg_x, b_rg_x, lru_lambda, ffn2_wgu, ffn2_wdown, ln_g, ln_b):
    n_prompt, seq, _ = x_prompt.shape
    n_sample = x_sample.shape[0]
    assert x_sample.shape[1] == 1 and n_sample % n_prompt == 0 and n_prompt == SUBLANES
    assert seq % MIXER_STEPS == 0 and (seq * n_prompt) % FFN_TILE == 0

    row = lambda v: v.reshape(DEPTH, 1, v.shape[-1])
    wpool_bd = _lane_blocks(w_pool).astype(BF16)
    wgate_bd = jnp.concatenate([_lane_blocks(w_rg_a), _lane_blocks(w_rg_x)], axis=1).astype(BF16)

    c_all = jnp.concatenate([c_sample, c_prompt], axis=0)
    b_ada3 = b_ada.reshape(DEPTH, 1, b_ada.shape[-1])
    mod, wgu_bf, wd_bf = _ada_call(c_all, w_ada, b_ada3, 0, round_next=((ffn1_wgu, 0, True), (ffn1_wdown, 0, False)))

    xp = x_prompt
    xs = x_sample.reshape(n_sample, D_MODEL)
    hist_major = lambda a: jnp.swapaxes(a, 1, 2)
    sample_states = (hist_major(state_pool), hist_major(state_conv), state_rglru)
    new_p = (jnp.zeros((DEPTH, POOL_BUF, n_prompt, POOL_WIDTH), F32),
             jnp.zeros((DEPTH, CONV_WIDTH - 1, n_prompt, LRU_WIDTH), F32),
             jnp.zeros((DEPTH, n_prompt, LRU_WIDTH), F32))
    new_s = tuple(jnp.zeros(a.shape, F32) for a in sample_states)

    for l in range(DEPTH):
        ffn = functools.partial(_ffn_call, layer=l, tm=FFN_TILE, n_prompt=n_prompt)
        xp, xs, win_bf, wout_bf, wgu2_bf, wd2_bf = ffn(
            xp, xs, mod, wgu_bf, wd_bf, ln_g, ln_b, sub=0, ln_idx=0, batch_major_in=(l == 0),
            round_next=((w_in, l, False), (w_out, l, False), (ffn2_wgu, l, True), (ffn2_wdown, l, False)))

        mixer_w = (win_bf, wpool_bd, row(pool_scale), conv_w, row(conv_b), wgate_bd,
                   row(b_rg_a), row(b_rg_x), row(lru_lambda), wout_bf, ln_g, ln_b)
        outs = _mixer_call(xp, xs, mod, mixer_w, sample_states, new_p, new_s, layer=l, steps=MIXER_STEPS,
                           n_prompt=n_prompt)
        xp, new_p, xs, new_s = outs[0], outs[1:4], outs[4], outs[5:8]

        last = l == DEPTH - 1
        outs = ffn(xp, xs, mod, wgu2_bf, wd2_bf, ln_g, ln_b, sub=2, ln_idx=2, batch_major_out=last,
                   round_next=() if last else ((ffn1_wgu, l + 1, True), (ffn1_wdown, l + 1, False)),
                   next_ada=None if last else (c_all, w_ada, b_ada3, l + 1))
        xp, xs = outs[:2]
        if not last:
            mod, wgu_bf, wd_bf = outs[2:]

    return (xp, xs.reshape(n_sample, 1, D_MODEL),
            hist_major(new_p[0]), hist_major(new_p[1]), new_p[2],
            hist_major(new_s[0]), hist_major(new_s[1]), new_s[2])
```

```python
import functools

import jax
import jax.numpy as jnp
from jax import lax
from jax.experimental import pallas as pl
from jax.experimental.pallas import tpu as pltpu

D_MODEL = 1024
DEPTH = 4
PAST_LEN = 16384
POOL_WIDTH = D_MODEL // 2
LRU_WIDTH = D_MODEL - POOL_WIDTH
IN_WIDTH = POOL_WIDTH + 2 * LRU_WIDTH
POOL_WINDOWS = (2, 4, 8, 16)
POOL_GROUP_DIM = POOL_WIDTH // len(POOL_WINDOWS)
POOL_BUF = max(POOL_WINDOWS) - 1
N_LRU_HEADS = 8
LRU_HEAD_DIM = LRU_WIDTH // N_LRU_HEADS
CONV_WIDTH = 4
LRU_C = 8.0
D_FF = 128 * int(round(8 * D_MODEL / 3 / 128))
N_MOD = 9
ALPHA = (2 * DEPTH) ** 0.25
LN_EPS = 1e-5

F32 = jnp.float32
BF16 = jnp.bfloat16

SUBLANES = 8
LANES = 128
BF16_SUBLANES = 16
MIB = 1024 * 1024
GU_BLOCKS = D_FF // LANES

FFN_TILE = 1024
FFN_TILE_RELAYOUT = 512
FFN_SUBTILE = 256
MIXER_STEPS = 64
MIXER_SUBTILES = 1
ADA_VMEM = 32 * MIB
FFN_VMEM = 56 * MIB
MIXER_VMEM = 56 * MIB


def _layer_norm(y, g, b):
    mu = jnp.mean(y, axis=-1, keepdims=True)
    d = y - mu
    var = jnp.mean(d * d, axis=-1, keepdims=True)
    return d * lax.rsqrt(var + LN_EPS) * g + b


def _sigmoid(x):
    return 0.5 * jnp.tanh(0.5 * x) + 0.5


def _silu(x):
    return x * _sigmoid(x)


def _softplus(x):
    return jnp.maximum(x, 0.0) + jnp.log1p(jnp.exp(-jnp.abs(x)))


def _per_batch(fn, x, *ms):
    rows, d = x.shape
    nb = ms[0].shape[0]
    if nb == rows:
        return fn(x, *ms)
    return fn(x.reshape(rows // nb, nb, d), *[m[None] for m in ms]).reshape(rows, d)


def _modulate(x, sh, sc):
    return _per_batch(lambda x_, sh_, sc_: x_ * (1.0 + sc_) + sh_, x, sh, sc)


def _gated_residual(x, g, y, gate_scale):
    return _per_batch(lambda y_, g_: (gate_scale * (1.0 + g_)) * y_, y, g) + ALPHA * x


def _mod_specs(sub, is_sample, n_sample, n_prompt):
    rows, block = (n_sample, 0) if is_sample else (n_prompt, n_sample // n_prompt)
    return [pl.BlockSpec((rows, D_MODEL), functools.partial(lambda i, k: (block, k), k=3 * sub + t))
            for t in range(3)]


def _resident(w, layer):
    if w.ndim == 2:
        return pl.BlockSpec(w.shape, lambda i: (0, 0), pipeline_mode=pl.Buffered(1))
    nd = w.ndim
    return pl.BlockSpec((None,) + tuple(w.shape[1:]), lambda i: (layer,) + (0,) * (nd - 1),
                        pipeline_mode=pl.Buffered(1))


def _layer_block(shape, layer):
    nd = len(shape)
    return pl.BlockSpec((None,) + tuple(shape[1:]), lambda i: (layer,) + (0,) * (nd - 1))


def _whole(shape):
    return pl.BlockSpec(tuple(shape), lambda i: (0,) * len(shape))


def _round_specs(w, layer, n_steps, step=lambda i: i):
    rows, cols = w.shape[1:]
    n_chunks = next(n for n in range(n_steps, 0, -1) if rows % (n * BF16_SUBLANES) == 0)
    chunk = rows // n_chunks
    idx = lambda *ids: jnp.minimum(step(*ids), n_chunks - 1)
    return (pl.BlockSpec((None, chunk, cols), lambda *ids: (layer, idx(*ids), 0)),
            pl.BlockSpec((chunk, cols), lambda *ids: (idx(*ids), 0)),
            jax.ShapeDtypeStruct((rows, cols), BF16))


def _round_chunks(src_refs, dst_refs, interleave):
    for src, dst, il in zip(src_refs, dst_refs, interleave):
        if il:
            for i in range(GU_BLOCKS):
                dst[:, 2 * i * LANES:(2 * i + 1) * LANES] = src[:, i * LANES:(i + 1) * LANES].astype(BF16)
                dst[:, (2 * i + 1) * LANES:(2 * i + 2) * LANES] = src[:, D_FF + i * LANES:D_FF + (i + 1) * LANES].astype(BF16)
        else:
            dst[...] = src[...].astype(BF16)


def _split(refs, *counts):
    out, pos = [], 0
    for c in counts:
        out.append(refs[pos:pos + c])
        pos += c
    out.append(refs[pos:])
    return out


def _ada_cols(n_out, n_steps):
    return next(tn for tn in range(LANES, n_out + 1, LANES) if n_out % tn == 0 and n_out // tn <= n_steps)


def _ada_tile(c_ref, w_ref, b_ref, o_ref):
    sc = _silu(c_ref[...]).astype(BF16)
    o_ref[...] = jnp.dot(sc, w_ref[...].astype(BF16), preferred_element_type=F32) + b_ref[...]


def _ada_specs(c_all, w_ada, layer, tn, n_steps):
    n_rows, n_out = c_all.shape[0], w_ada.shape[-1]
    n_tiles = n_out // tn
    assert n_tiles <= n_steps and n_out % tn == 0
    idx = lambda i: jnp.minimum(i, n_tiles - 1)
    return ([_whole(c_all.shape),
             pl.BlockSpec((None, D_MODEL, tn), lambda i: (layer, 0, idx(i))),
             pl.BlockSpec((None, 1, tn), lambda i: (layer, 0, idx(i)))],
            pl.BlockSpec((n_rows, tn), lambda i: (0, idx(i))),
            jax.ShapeDtypeStruct((n_rows, n_out), F32))


def _ada_kernel(*refs, interleave):
    n_round = len(interleave)
    ada_in, round_in, (o_ref,), round_out = _split(refs, 3, n_round, 1)
    _round_chunks(round_in, round_out, interleave)
    _ada_tile(*ada_in, o_ref)


def _ada_call(c_all, w_ada, b_ada, layer, round_next=()):
    n_col = 8
    ada_in, ada_out, ada_shape = _ada_specs(c_all, w_ada, layer, w_ada.shape[-1] // n_col, n_col)
    rnd = [_round_specs(w, l, n_col) for w, l, _ in round_next]
    return pl.pallas_call(
        functools.partial(_ada_kernel, interleave=tuple(il for _, _, il in round_next)),
        out_shape=[ada_shape] + [r[2] for r in rnd],
        grid=(n_col,),
        in_specs=ada_in + [r[0] for r in rnd],
        out_specs=[ada_out] + [r[1] for r in rnd],
        compiler_params=pltpu.CompilerParams(
            dimension_semantics=("arbitrary",),
            vmem_limit_bytes=ADA_VMEM),
        name="ada",
    )(c_all, w_ada, b_ada, *[w for w, _, _ in round_next])


def _swiglu_act(gu):
    blocks = [_silu(gu[:, 2 * i * LANES:(2 * i + 1) * LANES]) * gu[:, (2 * i + 1) * LANES:(2 * i + 2) * LANES]
              for i in range(GU_BLOCKS)]
    return jnp.concatenate(blocks, axis=-1)


def _ffn_rows(x, sh, sc, g, wgu_ref, wd_ref, lng, lnb):
    h = _modulate(x, sh, sc).astype(BF16)
    gu = jnp.dot(h, wgu_ref[...], preferred_element_type=F32)
    act = _swiglu_act(gu).astype(BF16)
    acc = jnp.dot(act, wd_ref[...], preferred_element_type=F32)
    return _layer_norm(_gated_residual(x, g, acc, 0.5), lng, lnb)


def _ffn_kernel(*refs, ln_idx, subtiles, batch_major_in, batch_major_out, interleave, next_mod):
    n_round, n_ada = len(interleave), 3 if next_mod else 0
    (x_ref, sh_ref, sc_ref, g_ref, wgu_ref, wd_ref, lng_ref, lnb_ref), sample_in, ada_in, round_in, \
        (o_ref, os_ref), ada_out, round_out, lbuf = _split(refs, 8, 4, n_ada, n_round, 2, n_ada // 3, n_round)
    _round_chunks(round_in, round_out, interleave)
    if next_mod:
        _ada_tile(*ada_in, *ada_out)
    lng, lnb = lng_ref[ln_idx:ln_idx + 1, :], lnb_ref[ln_idx:ln_idx + 1, :]
    sh, sc, g = sh_ref[...], sc_ref[...], g_ref[...]
    nb = sh.shape[0]
    rows = (x_ref.shape[0] * x_ref.shape[1] if batch_major_in else x_ref.shape[0]) // subtiles
    ts = rows // nb
    for s_i in range(subtiles):
        steps = slice(s_i * ts, (s_i + 1) * ts)
        if batch_major_in:
            for b in range(nb):
                lbuf[0][steps, b, :] = x_ref[b, steps, :]
            x = lbuf[0][steps, :, :].reshape(rows, D_MODEL)
        else:
            x = x_ref[s_i * rows:(s_i + 1) * rows, :]
        y = _ffn_rows(x, sh, sc, g, wgu_ref, wd_ref, lng, lnb)
        if batch_major_out:
            lbuf[0][steps, :, :] = y.reshape(ts, nb, D_MODEL)
            for b in range(nb):
                o_ref[b, steps, :] = lbuf[0][steps, b, :]
        else:
            o_ref[s_i * rows:(s_i + 1) * rows, :] = y

    @pl.when(pl.program_id(0) == pl.num_programs(0) - 1)
    def _sample_rows():
        xs_ref, shs_ref, scs_ref, gs_ref = sample_in
        os_ref[...] = _ffn_rows(xs_ref[...], shs_ref[...], scs_ref[...], gs_ref[...], wgu_ref, wd_ref, lng, lnb)


def _ffn_call(x, xs, mod, wgu, wd, ln_g, ln_b, *, layer, sub, ln_idx, tm, n_prompt,
              batch_major_in=False, batch_major_out=False, round_next=(), next_ada=None):
    rows = x.shape[0] * x.shape[1] if batch_major_in else x.shape[0]
    n_sample = xs.shape[0]
    steps = tm // n_prompt
    n_grid = rows // tm
    major_spec = pl.BlockSpec((n_prompt, steps, D_MODEL), lambda i: (0, i, 0))
    rows_spec = pl.BlockSpec((tm, D_MODEL), lambda i: (i, 0))
    out_shape = (n_prompt, rows // n_prompt, D_MODEL) if batch_major_out else (rows, D_MODEL)
    rnd = [_round_specs(w, l, n_grid) for w, l, _ in round_next]
    ada_in, ada_out, ada_shape, ada_args = [], [], [], []
    if next_ada is not None:
        c_all, w_ada, b_ada, ada_layer = next_ada
        ada_in, out_spec, shape = _ada_specs(c_all, w_ada, ada_layer, _ada_cols(w_ada.shape[-1], n_grid), n_grid)
        ada_out, ada_shape, ada_args = [out_spec], [shape], [c_all, w_ada, b_ada]
    kern = functools.partial(_ffn_kernel, ln_idx=ln_idx, subtiles=tm // FFN_SUBTILE,
                             batch_major_in=batch_major_in, batch_major_out=batch_major_out,
                             interleave=tuple(il for _, _, il in round_next), next_mod=next_ada is not None)
    return pl.pallas_call(
        kern,
        out_shape=[jax.ShapeDtypeStruct(out_shape, F32), jax.ShapeDtypeStruct(xs.shape, F32)] + ada_shape
        + [r[2] for r in rnd],
        grid=(n_grid,),
        in_specs=[major_spec if batch_major_in else rows_spec]
        + _mod_specs(sub, False, n_sample, n_prompt)
        + [_resident(wgu, layer), _resident(wd, layer), _resident(ln_g, layer), _resident(ln_b, layer)]
        + [_whole(xs.shape)] + _mod_specs(sub, True, n_sample, n_prompt)
        + ada_in + [r[0] for r in rnd],
        out_specs=[major_spec if batch_major_out else rows_spec, _whole(xs.shape)] + ada_out + [r[1] for r in rnd],
        scratch_shapes=[pltpu.VMEM((steps, n_prompt, D_MODEL), F32)] if batch_major_in or batch_major_out else [],
        compiler_params=pltpu.CompilerParams(
            dimension_semantics=("arbitrary",),
            vmem_limit_bytes=FFN_VMEM),
        name=f"ffn{sub}",
    )(x, mod, mod, mod, wgu, wd, ln_g, ln_b, xs, mod, mod, mod, *ada_args, *[w for w, _, _ in round_next])


def _grouped_dot(x, w_ref, first, n):
    c = w_ref.shape[-1]
    return jnp.concatenate([jnp.dot(x[:, i * c:(i + 1) * c], w_ref[first + i], preferred_element_type=F32)
                            for i in range(n)], axis=-1)


def _gates(xc, wg_ref, ba_ref, bx_ref, lam_ref):
    xb = xc.astype(BF16)
    n = wg_ref.shape[0] // 2
    r = _sigmoid(_grouped_dot(xb, wg_ref, 0, n) + ba_ref[...])
    i = _sigmoid(_grouped_dot(xb, wg_ref, n, n) + bx_ref[...])
    log_a = (-LRU_C * r) * _softplus(-lam_ref[...])
    a = jnp.exp(log_a)
    t = jnp.tanh(log_a)
    q = (-2.0 * t) / (1.0 - t)
    mult = jnp.where(q == 0.0, 0.0, q * lax.rsqrt(q))
    return a, mult * (i * xc)


def _mix_out(x, g, y_pool, y_lru, wout_ref, lng_ref, lnb_ref):
    mix = jnp.dot(y_pool.astype(BF16), wout_ref[:POOL_WIDTH, :], preferred_element_type=F32)
    mix = mix + jnp.dot(y_lru.astype(BF16), wout_ref[POOL_WIDTH:, :], preferred_element_type=F32)
    y = _gated_residual(x, g, mix, 1.0)
    return _layer_norm(y, lng_ref[1:2, :], lnb_ref[1:2, :])


N_MIXER_WEIGHTS = 12


def _mixer_sample_rows(x_ref, sh_ref, sc_ref, g_ref, pst_ref, cst_ref, h0_ref, weights,
                       o_ref, npool_ref, nconv_ref, nh_ref):
    (win_ref, wpool_ref, pscale_ref, cw_ref, cb_ref, wg_ref, ba_ref, bx_ref, lam_ref,
     wout_ref, lng_ref, lnb_ref) = weights
    x = x_ref[...]
    h = _modulate(x, sh_ref[...], sc_ref[...]).astype(BF16)
    proj = jnp.dot(h, win_ref[...], preferred_element_type=F32)
    u_pool = proj[:, :POOL_WIDTH]
    u_lru = proj[:, POOL_WIDTH:POOL_WIDTH + LRU_WIDTH]
    u_gate = proj[:, POOL_WIDTH + LRU_WIDTH:]

    pooled = []
    for gi, w in enumerate(POOL_WINDOWS):
        cols = slice(gi * POOL_GROUP_DIM, (gi + 1) * POOL_GROUP_DIM)
        s = u_pool[:, cols]
        for back in range(1, w):
            s = s + pst_ref[POOL_BUF - back, :, cols]
        cnt = float(min(PAST_LEN + 1, w))
        pooled.append(s / cnt - u_pool[:, cols])
    pooled = jnp.concatenate(pooled, axis=-1).astype(BF16)
    y_pool = _grouped_dot(pooled, wpool_ref, 0, len(POOL_WINDOWS)) * pscale_ref[...]
    npool_ref[0:POOL_BUF - 1] = pst_ref[1:POOL_BUF]
    npool_ref[POOL_BUF - 1] = u_pool

    xc = None
    for k in range(CONV_WIDTH - 1):
        term = cst_ref[k] * cw_ref[k:k + 1, :]
        xc = term if xc is None else xc + term
    xc = cb_ref[...] + (xc + u_lru * cw_ref[CONV_WIDTH - 1:CONV_WIDTH, :])
    nconv_ref[0:CONV_WIDTH - 2] = cst_ref[1:CONV_WIDTH - 1]
    nconv_ref[CONV_WIDTH - 2] = u_lru

    a, b = _gates(xc, wg_ref, ba_ref, bx_ref, lam_ref)
    hs = a * h0_ref[...] + b
    nh_ref[...] = hs
    y_lru = hs * jax.nn.gelu(u_gate)

    o_ref[...] = _mix_out(x, g_ref[...], y_pool, y_lru, wout_ref, lng_ref, lnb_ref)


def _mixer_kernel(*refs, steps, nb, subtiles, interleave, next_mod):
    n_round, n_ada = len(interleave), 3 if next_mod else 0
    (x_ref, sh_ref, sc_ref, g_ref), weights, sample_in, ada_in, round_in, _aliased, \
        (o_ref, npool_ref, nconv_ref, nh_ref), sample_out, ada_out, round_out, (zbuf, cbuf, hst, hsbuf) = \
        _split(refs, 4, N_MIXER_WEIGHTS, 7, n_ada, n_round, 6, 4, 4, n_ada // 3, n_round)
    if next_mod:
        _ada_tile(*ada_in, *ada_out)
    (win_ref, wpool_ref, pscale_ref, cw_ref, cb_ref, wg_ref, ba_ref, bx_ref, lam_ref,
     wout_ref, lng_ref, lnb_ref) = weights
    _round_chunks(round_in, round_out, interleave)

    pid = pl.program_id(0)
    tm = steps * nb
    pool_hist = POOL_BUF * nb
    conv_hist = (CONV_WIDTH - 1) * nb

    @pl.when(pid == 0)
    def _new_sequences():
        zbuf[0:pool_hist, :] = jnp.zeros((pool_hist, POOL_WIDTH), F32)
        cbuf[0:conv_hist, :] = jnp.zeros((conv_hist, LRU_WIDTH), F32)
        hst[...] = jnp.zeros_like(hst)

    sh, sc, g = sh_ref[...], sc_ref[...], g_ref[...]
    hm, hsteps = tm // subtiles, steps // subtiles
    t_idx = lax.shift_right_logical(lax.broadcasted_iota(jnp.int32, (hm, POOL_GROUP_DIM), 0), nb.bit_length() - 1)
    subs = range(subtiles)
    rows = [slice(s_i * hm, (s_i + 1) * hm) for s_i in subs]

    xs = [x_ref[rs, :] for rs in rows]
    projs = [jnp.dot(_modulate(x, sh, sc).astype(BF16), win_ref[...], preferred_element_type=F32) for x in xs]
    u_gates = [p[:, POOL_WIDTH + LRU_WIDTH:] for p in projs]

    y_pools, coeffs = [], []
    for s_i, proj in zip(subs, projs):
        r0 = s_i * hm
        u_pool = proj[:, :POOL_WIDTH]
        u_lru = proj[:, POOL_WIDTH:POOL_WIDTH + LRU_WIDTH]

        zbuf[pool_hist + r0:pool_hist + r0 + hm, :] = u_pool
        pos1 = (pid * steps + s_i * hsteps + 1) + t_idx
        cur = zbuf[r0:r0 + pool_hist + hm, :]
        first, w = -POOL_BUF, 1
        pooled = []
        for gi, wnd in enumerate(POOL_WINDOWS):
            assert wnd == 2 * w
            n = cur.shape[0]
            cur = cur[w * nb:, :] + cur[:n - w * nb, :]
            first, w = first + w, wnd
            cols = slice(gi * POOL_GROUP_DIM, (gi + 1) * POOL_GROUP_DIM)
            s = cur[-first * nb:, :POOL_GROUP_DIM]
            pooled.append(s / jnp.minimum(pos1, wnd).astype(F32) - u_pool[:, cols])
            cur = cur[:, POOL_GROUP_DIM:]
        assert first == 0
        pooled = jnp.concatenate(pooled, axis=-1).astype(BF16)
        y_pools.append(_grouped_dot(pooled, wpool_ref, 0, len(POOL_WINDOWS)) * pscale_ref[...])

        cbuf[conv_hist + r0:conv_hist + r0 + hm, :] = u_lru
        xc = None
        for k in range(CONV_WIDTH):
            term = cbuf[r0 + k * nb:r0 + k * nb + hm, :] * cw_ref[k:k + 1, :]
            xc = term if xc is None else xc + term
        coeffs.append(_gates(cb_ref[...] + xc, wg_ref, ba_ref, bx_ref, lam_ref))

    hcur = hst[...]
    for s_i, (a, b) in zip(subs, coeffs):
        r0 = s_i * hm
        for t in range(hsteps):
            hcur = a[t * nb:(t + 1) * nb, :] * hcur + b[t * nb:(t + 1) * nb, :]
            hsbuf[r0 + t * nb:r0 + (t + 1) * nb, :] = hcur
    hst[...] = hcur

    for rs, x, y_pool, u_gate in zip(rows, xs, y_pools, u_gates):
        y_lru = hsbuf[rs, :] * jax.nn.gelu(u_gate)
        o_ref[rs, :] = _mix_out(x, g, y_pool, y_lru, wout_ref, lng_ref, lnb_ref)

    @pl.when(pid == pl.num_programs(0) - 1)
    def _final_state_and_sample_rows():
        npool_ref[...] = zbuf[tm:tm + pool_hist, :].reshape(POOL_BUF, nb, POOL_WIDTH)
        nconv_ref[...] = cbuf[tm:tm + conv_hist, :].reshape(CONV_WIDTH - 1, nb, LRU_WIDTH)
        nh_ref[...] = hcur
        _mixer_sample_rows(*sample_in, weights, *sample_out)

    zbuf[0:pool_hist, :] = zbuf[tm:tm + pool_hist, :]
    cbuf[0:conv_hist, :] = cbuf[tm:tm + conv_hist, :]


def _mixer_call(x2d, xs, mod, wts, sample_states, new_p, new_s, *, layer, steps, n_prompt, round_next=(),
                next_ada=None):
    assert len(wts) == N_MIXER_WEIGHTS
    rows = x2d.shape[0]
    n_sample = xs.shape[0]
    tm = steps * n_prompt
    n_grid = rows // tm
    rnd = [_round_specs(w, l, n_grid) for w, l, _ in round_next]
    ada_in, ada_out, ada_shape, ada_args = [], [], [], []
    if next_ada is not None:
        c_all, w_ada, b_ada, ada_layer = next_ada
        ada_in, out_spec, shape = _ada_specs(c_all, w_ada, ada_layer, _ada_cols(w_ada.shape[-1], n_grid), n_grid)
        ada_out, ada_shape, ada_args = [out_spec], [shape], [c_all, w_ada, b_ada]
    kern = functools.partial(_mixer_kernel, steps=steps, nb=n_prompt, subtiles=MIXER_SUBTILES,
                             interleave=tuple(il for _, _, il in round_next), next_mod=next_ada is not None)
    stacks = tuple(new_p) + tuple(new_s)
    n_in = 4 + len(wts) + 4 + len(sample_states) + len(ada_in) + len(rnd)
    any_spec = pl.BlockSpec(memory_space=pl.ANY)
    stack_specs = [_layer_block(a.shape, layer) for a in stacks]
    rows_spec = pl.BlockSpec((tm, D_MODEL), lambda i: (i, 0))
    return pl.pallas_call(
        kern,
        out_shape=[jax.ShapeDtypeStruct((rows, D_MODEL), F32)]
        + [jax.ShapeDtypeStruct(a.shape, F32) for a in new_p]
        + [jax.ShapeDtypeStruct(xs.shape, F32)]
        + [jax.ShapeDtypeStruct(a.shape, F32) for a in new_s]
        + ada_shape + [r[2] for r in rnd],
        grid=(n_grid,),
        in_specs=[rows_spec]
        + _mod_specs(1, False, n_sample, n_prompt)
        + [_resident(w, layer) for w in wts]
        + [_whole(xs.shape)] + _mod_specs(1, True, n_sample, n_prompt)
        + [_layer_block(a.shape, layer) for a in sample_states]
        + ada_in + [r[0] for r in rnd]
        + [any_spec] * len(stacks),
        out_specs=[rows_spec] + stack_specs[:3] + [_whole(xs.shape)] + stack_specs[3:] + ada_out
        + [r[1] for r in rnd],
        scratch_shapes=[
            pltpu.VMEM((POOL_BUF * n_prompt + tm, POOL_WIDTH), F32),
            pltpu.VMEM(((CONV_WIDTH - 1) * n_prompt + tm, LRU_WIDTH), F32),
            pltpu.VMEM((n_prompt, LRU_WIDTH), F32),
            pltpu.VMEM((tm, LRU_WIDTH), F32),
        ],
        input_output_aliases={n_in: 1, n_in + 1: 2, n_in + 2: 3, n_in + 3: 5, n_in + 4: 6, n_in + 5: 7},
        compiler_params=pltpu.CompilerParams(
            dimension_semantics=("arbitrary",),
            vmem_limit_bytes=MIXER_VMEM),
        name="mixer",
    )(x2d, mod, mod, mod, *wts, xs, mod, mod, mod, *sample_states, *ada_args, *[w for w, _, _ in round_next],
      *stacks)


def _lane_blocks(w):
    depth, heads, c, _ = w.shape
    per = LANES // c
    eye = jnp.eye(per, dtype=w.dtype)
    blocks = jnp.einsum('lbgij,gh->lbgihj', w.reshape(depth, heads // per, per, c, c), eye)
    return blocks.reshape(depth, heads // per, LANES, LANES)


def kernel(x_prompt, x_sample, state_pool, state_conv, state_rglru, c_prompt, c_sample, w_ada, b_ada, ffn1_wgu, ffn1_wdown, w_in, w_out, w_pool, pool_scale, conv_w, conv_b, w_rg_a, b_rg_a, w_rg_x, b_rg_x, lru_lambda, ffn2_wgu, ffn2_wdown, ln_g, ln_b):
    n_prompt, seq, _ = x_prompt.shape
    n_sample = x_sample.shape[0]
    assert x_sample.shape[1] == 1 and n_sample % n_prompt == 0 and n_prompt == SUBLANES
    assert seq % MIXER_STEPS == 0 and (seq * n_prompt) % FFN_TILE == 0 and FFN_TILE % FFN_TILE_RELAYOUT == 0

    row = lambda v: v.reshape(DEPTH, 1, v.shape[-1])
    wpool_bd = _lane_blocks(w_pool).astype(BF16)
    wgate_bd = jnp.concatenate([_lane_blocks(w_rg_a), _lane_blocks(w_rg_x)], axis=1).astype(BF16)

    c_all = jnp.concatenate([c_sample, c_prompt], axis=0)
    b_ada3 = b_ada.reshape(DEPTH, 1, b_ada.shape[-1])
    mod, wgu_bf, wd_bf = _ada_call(c_all, w_ada, b_ada3, 0, round_next=((ffn1_wgu, 0, True), (ffn1_wdown, 0, False)))

    xp = x_prompt
    xs = x_sample.reshape(n_sample, D_MODEL)
    hist_major = lambda a: jnp.swapaxes(a, 1, 2)
    sample_states = (hist_major(state_pool), hist_major(state_conv), state_rglru)
    new_p = (jnp.zeros((DEPTH, POOL_BUF, n_prompt, POOL_WIDTH), F32),
             jnp.zeros((DEPTH, CONV_WIDTH - 1, n_prompt, LRU_WIDTH), F32),
             jnp.zeros((DEPTH, n_prompt, LRU_WIDTH), F32))
    new_s = tuple(jnp.zeros(a.shape, F32) for a in sample_states)

    for l in range(DEPTH):
        last = l == DEPTH - 1
        ffn = functools.partial(_ffn_call, layer=l, n_prompt=n_prompt)
        outs = ffn(xp, xs, mod, wgu_bf, wd_bf, ln_g, ln_b, sub=0, ln_idx=0, batch_major_in=(l == 0),
                   tm=FFN_TILE_RELAYOUT if l == 0 else FFN_TILE,
                   round_next=((w_in, l, False), (w_out, l, False)),
                   next_ada=None if last else (c_all, w_ada, b_ada3, l + 1))
        xp, xs, next_mod, (win_bf, wout_bf) = outs[0], outs[1], outs[2:-2], outs[-2:]

        mixer_w = (win_bf, wpool_bd, row(pool_scale), conv_w, row(conv_b), wgate_bd,
                   row(b_rg_a), row(b_rg_x), row(lru_lambda), wout_bf, ln_g, ln_b)
        outs = _mixer_call(xp, xs, mod, mixer_w, sample_states, new_p, new_s, layer=l, steps=MIXER_STEPS,
                           n_prompt=n_prompt, round_next=((ffn2_wgu, l, True), (ffn2_wdown, l, False)))
        xp, new_p, xs, new_s, (wgu2_bf, wd2_bf) = outs[0], outs[1:4], outs[4], outs[5:8], outs[8:]

        outs = ffn(xp, xs, mod, wgu2_bf, wd2_bf, ln_g, ln_b, sub=2, ln_idx=2, batch_major_out=last,
                   tm=FFN_TILE_RELAYOUT if last else FFN_TILE,
                   round_next=() if last else ((ffn1_wgu, l + 1, True), (ffn1_wdown, l + 1, False)))
        xp, xs = outs[:2]
        if not last:
            (mod,), (wgu_bf, wd_bf) = next_mod, outs[2:]

    return (xp, xs.reshape(n_sample, 1, D_MODEL),
            hist_major(new_p[0]), hist_major(new_p[1]), new_p[2],
            hist_major(new_s[0]), hist_major(new_s[1]), new_s[2])
```

```python
import functools

import jax
import jax.numpy as jnp
from jax import lax
from jax.experimental import pallas as pl
from jax.experimental.pallas import tpu as pltpu

D_MODEL = 1024
DEPTH = 4
PAST_LEN = 16384
POOL_WIDTH = D_MODEL // 2
LRU_WIDTH = D_MODEL - POOL_WIDTH
IN_WIDTH = POOL_WIDTH + 2 * LRU_WIDTH
POOL_WINDOWS = (2, 4, 8, 16)
POOL_GROUP_DIM = POOL_WIDTH // len(POOL_WINDOWS)
POOL_BUF = max(POOL_WINDOWS) - 1
N_LRU_HEADS = 8
LRU_HEAD_DIM = LRU_WIDTH // N_LRU_HEADS
CONV_WIDTH = 4
LRU_C = 8.0
D_FF = 128 * int(round(8 * D_MODEL / 3 / 128))
N_MOD = 9
ALPHA = (2 * DEPTH) ** 0.25
LN_EPS = 1e-5

F32 = jnp.float32
BF16 = jnp.bfloat16

SUBLANES = 8
LANES = 128
BF16_SUBLANES = 16
MIB = 1024 * 1024
GU_BLOCKS = D_FF // LANES

FFN_TILE = 1024
FFN_TILE_SMALL = 512
FFN_SUBTILE = 256
MIXER_STEPS = 64
MIXER_SUBTILES = 1
ADA_VMEM = 32 * MIB
FFN_VMEM = 56 * MIB
MIXER_VMEM = 56 * MIB


def _layer_norm(y, g, b):
    mu = jnp.mean(y, axis=-1, keepdims=True)
    d = y - mu
    var = jnp.mean(d * d, axis=-1, keepdims=True)
    return d * lax.rsqrt(var + LN_EPS) * g + b


def _sigmoid(x):
    return 0.5 * jnp.tanh(0.5 * x) + 0.5


def _silu(x):
    return x * _sigmoid(x)


def _softplus(x):
    return jnp.maximum(x, 0.0) + jnp.log1p(jnp.exp(-jnp.abs(x)))


def _per_batch(fn, x, *ms):
    rows, d = x.shape
    nb = ms[0].shape[0]
    if nb == rows:
        return fn(x, *ms)
    return fn(x.reshape(rows // nb, nb, d), *[m[None] for m in ms]).reshape(rows, d)


def _modulate(x, sh, sc):
    return _per_batch(lambda x_, sh_, sc_: x_ * (1.0 + sc_) + sh_, x, sh, sc)


def _gated_residual(x, g, y, gate_scale):
    return _per_batch(lambda y_, g_: (gate_scale * (1.0 + g_)) * y_, y, g) + ALPHA * x


def _mod_specs(sub, is_sample, n_sample, n_prompt):
    rows, block = (n_sample, 0) if is_sample else (n_prompt, n_sample // n_prompt)
    return [pl.BlockSpec((rows, D_MODEL), functools.partial(lambda i, k: (block, k), k=3 * sub + t))
            for t in range(3)]


def _resident(w, layer):
    if w.ndim == 2:
        return pl.BlockSpec(w.shape, lambda i: (0, 0), pipeline_mode=pl.Buffered(1))
    nd = w.ndim
    return pl.BlockSpec((None,) + tuple(w.shape[1:]), lambda i: (layer,) + (0,) * (nd - 1),
                        pipeline_mode=pl.Buffered(1))


def _layer_block(shape, layer):
    nd = len(shape)
    return pl.BlockSpec((None,) + tuple(shape[1:]), lambda i: (layer,) + (0,) * (nd - 1))


def _whole(shape):
    return pl.BlockSpec(tuple(shape), lambda i: (0,) * len(shape))


def _round_specs(w, layer, n_steps, step=lambda i: i):
    rows, cols = w.shape[1:]
    n_chunks = next(n for n in range(n_steps, 0, -1) if rows % (n * BF16_SUBLANES) == 0)
    chunk = rows // n_chunks
    idx = lambda *ids: jnp.minimum(step(*ids), n_chunks - 1)
    return (pl.BlockSpec((None, chunk, cols), lambda *ids: (layer, idx(*ids), 0)),
            pl.BlockSpec((chunk, cols), lambda *ids: (idx(*ids), 0)),
            jax.ShapeDtypeStruct((rows, cols), BF16))


def _round_chunks(src_refs, dst_refs, interleave):
    for src, dst, il in zip(src_refs, dst_refs, interleave):
        if il:
            for i in range(GU_BLOCKS):
                dst[:, 2 * i * LANES:(2 * i + 1) * LANES] = src[:, i * LANES:(i + 1) * LANES].astype(BF16)
                dst[:, (2 * i + 1) * LANES:(2 * i + 2) * LANES] = src[:, D_FF + i * LANES:D_FF + (i + 1) * LANES].astype(BF16)
        else:
            dst[...] = src[...].astype(BF16)


def _split(refs, *counts):
    out, pos = [], 0
    for c in counts:
        out.append(refs[pos:pos + c])
        pos += c
    out.append(refs[pos:])
    return out


def _ada_cols(n_out, n_steps):
    return next(tn for tn in range(LANES, n_out + 1, LANES) if n_out % tn == 0 and n_out // tn <= n_steps)


def _ada_tile(c_ref, w_ref, b_ref, o_ref):
    sc = _silu(c_ref[...]).astype(BF16)
    o_ref[...] = jnp.dot(sc, w_ref[...].astype(BF16), preferred_element_type=F32) + b_ref[...]


def _ada_specs(c_all, w_ada, layer, tn, n_steps):
    n_rows, n_out = c_all.shape[0], w_ada.shape[-1]
    n_tiles = n_out // tn
    assert n_tiles <= n_steps and n_out % tn == 0
    idx = lambda i: jnp.minimum(i, n_tiles - 1)
    return ([_whole(c_all.shape),
             pl.BlockSpec((None, D_MODEL, tn), lambda i: (layer, 0, idx(i))),
             pl.BlockSpec((None, 1, tn), lambda i: (layer, 0, idx(i)))],
            pl.BlockSpec((n_rows, tn), lambda i: (0, idx(i))),
            jax.ShapeDtypeStruct((n_rows, n_out), F32))


def _ada_kernel(*refs, interleave):
    n_round = len(interleave)
    ada_in, round_in, (o_ref,), round_out = _split(refs, 3, n_round, 1)
    _round_chunks(round_in, round_out, interleave)
    _ada_tile(*ada_in, o_ref)


def _ada_call(c_all, w_ada, b_ada, layer, round_next=()):
    n_col = 8
    ada_in, ada_out, ada_shape = _ada_specs(c_all, w_ada, layer, w_ada.shape[-1] // n_col, n_col)
    rnd = [_round_specs(w, l, n_col) for w, l, _ in round_next]
    return pl.pallas_call(
        functools.partial(_ada_kernel, interleave=tuple(il for _, _, il in round_next)),
        out_shape=[ada_shape] + [r[2] for r in rnd],
        grid=(n_col,),
        in_specs=ada_in + [r[0] for r in rnd],
        out_specs=[ada_out] + [r[1] for r in rnd],
        compiler_params=pltpu.CompilerParams(
            dimension_semantics=("arbitrary",),
            vmem_limit_bytes=ADA_VMEM),
        name="ada",
    )(c_all, w_ada, b_ada, *[w for w, _, _ in round_next])


def _swiglu_act(gu):
    blocks = [_silu(gu[:, 2 * i * LANES:(2 * i + 1) * LANES]) * gu[:, (2 * i + 1) * LANES:(2 * i + 2) * LANES]
              for i in range(GU_BLOCKS)]
    return jnp.concatenate(blocks, axis=-1)


def _ffn_rows(x, sh, sc, g, wgu_ref, wd_ref, lng, lnb):
    h = _modulate(x, sh, sc).astype(BF16)
    gu = jnp.dot(h, wgu_ref[...], preferred_element_type=F32)
    act = _swiglu_act(gu).astype(BF16)
    acc = jnp.dot(act, wd_ref[...], preferred_element_type=F32)
    return _layer_norm(_gated_residual(x, g, acc, 0.5), lng, lnb)


def _ffn_kernel(*refs, ln_idx, subtiles, batch_major_in, batch_major_out, interleave, next_mod):
    n_round, n_ada = len(interleave), 3 if next_mod else 0
    (x_ref, sh_ref, sc_ref, g_ref, wgu_ref, wd_ref, lng_ref, lnb_ref), sample_in, ada_in, round_in, \
        (o_ref, os_ref), ada_out, round_out, lbuf = _split(refs, 8, 4, n_ada, n_round, 2, n_ada // 3, n_round)
    _round_chunks(round_in, round_out, interleave)
    if next_mod:
        _ada_tile(*ada_in, *ada_out)
    lng, lnb = lng_ref[ln_idx:ln_idx + 1, :], lnb_ref[ln_idx:ln_idx + 1, :]
    sh, sc, g = sh_ref[...], sc_ref[...], g_ref[...]
    nb = sh.shape[0]
    rows = (x_ref.shape[0] * x_ref.shape[1] if batch_major_in else x_ref.shape[0]) // subtiles
    ts = rows // nb
    for s_i in range(subtiles):
        steps = slice(s_i * ts, (s_i + 1) * ts)
        if batch_major_in:
            for b in range(nb):
                lbuf[0][steps, b, :] = x_ref[b, steps, :]
            x = lbuf[0][steps, :, :].reshape(rows, D_MODEL)
        else:
            x = x_ref[s_i * rows:(s_i + 1) * rows, :]
        y = _ffn_rows(x, sh, sc, g, wgu_ref, wd_ref, lng, lnb)
        if batch_major_out:
            lbuf[0][steps, :, :] = y.reshape(ts, nb, D_MODEL)
            for b in range(nb):
                o_ref[b, steps, :] = lbuf[0][steps, b, :]
        else:
            o_ref[s_i * rows:(s_i + 1) * rows, :] = y

    @pl.when(pl.program_id(0) == pl.num_programs(0) - 1)
    def _sample_rows():
        xs_ref, shs_ref, scs_ref, gs_ref = sample_in
        os_ref[...] = _ffn_rows(xs_ref[...], shs_ref[...], scs_ref[...], gs_ref[...], wgu_ref, wd_ref, lng, lnb)


def _ffn_call(x, xs, mod, wgu, wd, ln_g, ln_b, *, layer, sub, ln_idx, tm, n_prompt,
              batch_major_in=False, batch_major_out=False, round_next=(), next_ada=None):
    rows = x.shape[0] * x.shape[1] if batch_major_in else x.shape[0]
    n_sample = xs.shape[0]
    steps = tm // n_prompt
    n_grid = rows // tm
    major_spec = pl.BlockSpec((n_prompt, steps, D_MODEL), lambda i: (0, i, 0))
    rows_spec = pl.BlockSpec((tm, D_MODEL), lambda i: (i, 0))
    out_shape = (n_prompt, rows // n_prompt, D_MODEL) if batch_major_out else (rows, D_MODEL)
    rnd = [_round_specs(w, l, n_grid) for w, l, _ in round_next]
    ada_in, ada_out, ada_shape, ada_args = [], [], [], []
    if next_ada is not None:
        c_all, w_ada, b_ada, ada_layer = next_ada
        ada_in, out_spec, shape = _ada_specs(c_all, w_ada, ada_layer, _ada_cols(w_ada.shape[-1], n_grid), n_grid)
        ada_out, ada_shape, ada_args = [out_spec], [shape], [c_all, w_ada, b_ada]
    kern = functools.partial(_ffn_kernel, ln_idx=ln_idx, subtiles=tm // FFN_SUBTILE,
                             batch_major_in=batch_major_in, batch_major_out=batch_major_out,
                             interleave=tuple(il for _, _, il in round_next), next_mod=next_ada is not None)
    return pl.pallas_call(
        kern,
        out_shape=[jax.ShapeDtypeStruct(out_shape, F32), jax.ShapeDtypeStruct(xs.shape, F32)] + ada_shape
        + [r[2] for r in rnd],
        grid=(n_grid,),
        in_specs=[major_spec if batch_major_in else rows_spec]
        + _mod_specs(sub, False, n_sample, n_prompt)
        + [_resident(wgu, layer), _resident(wd, layer), _resident(ln_g, layer), _resident(ln_b, layer)]
        + [_whole(xs.shape)] + _mod_specs(sub, True, n_sample, n_prompt)
        + ada_in + [r[0] for r in rnd],
        out_specs=[major_spec if batch_major_out else rows_spec, _whole(xs.shape)] + ada_out + [r[1] for r in rnd],
        scratch_shapes=[pltpu.VMEM((steps, n_prompt, D_MODEL), F32)] if batch_major_in or batch_major_out else [],
        compiler_params=pltpu.CompilerParams(
            dimension_semantics=("arbitrary",),
            vmem_limit_bytes=FFN_VMEM),
        name=f"ffn{sub}",
    )(x, mod, mod, mod, wgu, wd, ln_g, ln_b, xs, mod, mod, mod, *ada_args, *[w for w, _, _ in round_next])


def _grouped_dot(x, w_ref, first, n):
    c = w_ref.shape[-1]
    return jnp.concatenate([jnp.dot(x[:, i * c:(i + 1) * c], w_ref[first + i], preferred_element_type=F32)
                            for i in range(n)], axis=-1)


def _gates(xc, wg_ref, ba_ref, bx_ref, lam_ref):
    xb = xc.astype(BF16)
    n = wg_ref.shape[0] // 2
    r = _sigmoid(_grouped_dot(xb, wg_ref, 0, n) + ba_ref[...])
    i = _sigmoid(_grouped_dot(xb, wg_ref, n, n) + bx_ref[...])
    log_a = (-LRU_C * r) * _softplus(-lam_ref[...])
    a = jnp.exp(log_a)
    t = jnp.tanh(log_a)
    q = (-2.0 * t) / (1.0 - t)
    mult = jnp.where(q == 0.0, 0.0, q * lax.rsqrt(q))
    return a, mult * (i * xc)


def _mix_out(x, g, y_pool, y_lru, wout_ref, lng_ref, lnb_ref):
    mix = jnp.dot(y_pool.astype(BF16), wout_ref[:POOL_WIDTH, :], preferred_element_type=F32)
    mix = mix + jnp.dot(y_lru.astype(BF16), wout_ref[POOL_WIDTH:, :], preferred_element_type=F32)
    y = _gated_residual(x, g, mix, 1.0)
    return _layer_norm(y, lng_ref[1:2, :], lnb_ref[1:2, :])


N_MIXER_WEIGHTS = 12


def _mixer_sample_rows(x_ref, sh_ref, sc_ref, g_ref, pst_ref, cst_ref, h0_ref, weights,
                       o_ref, npool_ref, nconv_ref, nh_ref):
    (win_ref, wpool_ref, pscale_ref, cw_ref, cb_ref, wg_ref, ba_ref, bx_ref, lam_ref,
     wout_ref, lng_ref, lnb_ref) = weights
    x = x_ref[...]
    h = _modulate(x, sh_ref[...], sc_ref[...]).astype(BF16)
    proj = jnp.dot(h, win_ref[...], preferred_element_type=F32)
    u_pool = proj[:, :POOL_WIDTH]
    u_lru = proj[:, POOL_WIDTH:POOL_WIDTH + LRU_WIDTH]
    u_gate = proj[:, POOL_WIDTH + LRU_WIDTH:]

    pooled = []
    for gi, w in enumerate(POOL_WINDOWS):
        cols = slice(gi * POOL_GROUP_DIM, (gi + 1) * POOL_GROUP_DIM)
        s = u_pool[:, cols]
        for back in range(1, w):
            s = s + pst_ref[POOL_BUF - back, :, cols]
        cnt = float(min(PAST_LEN + 1, w))
        pooled.append(s / cnt - u_pool[:, cols])
    pooled = jnp.concatenate(pooled, axis=-1).astype(BF16)
    y_pool = _grouped_dot(pooled, wpool_ref, 0, len(POOL_WINDOWS)) * pscale_ref[...]
    npool_ref[0:POOL_BUF - 1] = pst_ref[1:POOL_BUF]
    npool_ref[POOL_BUF - 1] = u_pool

    xc = None
    for k in range(CONV_WIDTH - 1):
        term = cst_ref[k] * cw_ref[k:k + 1, :]
        xc = term if xc is None else xc + term
    xc = cb_ref[...] + (xc + u_lru * cw_ref[CONV_WIDTH - 1:CONV_WIDTH, :])
    nconv_ref[0:CONV_WIDTH - 2] = cst_ref[1:CONV_WIDTH - 1]
    nconv_ref[CONV_WIDTH - 2] = u_lru

    a, b = _gates(xc, wg_ref, ba_ref, bx_ref, lam_ref)
    hs = a * h0_ref[...] + b
    nh_ref[...] = hs
    y_lru = hs * jax.nn.gelu(u_gate)

    o_ref[...] = _mix_out(x, g_ref[...], y_pool, y_lru, wout_ref, lng_ref, lnb_ref)


def _mixer_kernel(*refs, steps, nb, subtiles, interleave, next_mod):
    n_round, n_ada = len(interleave), 3 if next_mod else 0
    (x_ref, sh_ref, sc_ref, g_ref), weights, sample_in, ada_in, round_in, _aliased, \
        (o_ref, npool_ref, nconv_ref, nh_ref), sample_out, ada_out, round_out, (zbuf, cbuf, hst, hsbuf) = \
        _split(refs, 4, N_MIXER_WEIGHTS, 7, n_ada, n_round, 6, 4, 4, n_ada // 3, n_round)
    if next_mod:
        _ada_tile(*ada_in, *ada_out)
    (win_ref, wpool_ref, pscale_ref, cw_ref, cb_ref, wg_ref, ba_ref, bx_ref, lam_ref,
     wout_ref, lng_ref, lnb_ref) = weights
    _round_chunks(round_in, round_out, interleave)

    pid = pl.program_id(0)
    tm = steps * nb
    pool_hist = POOL_BUF * nb
    conv_hist = (CONV_WIDTH - 1) * nb

    @pl.when(pid == 0)
    def _new_sequences():
        zbuf[0:pool_hist, :] = jnp.zeros((pool_hist, POOL_WIDTH), F32)
        cbuf[0:conv_hist, :] = jnp.zeros((conv_hist, LRU_WIDTH), F32)
        hst[...] = jnp.zeros_like(hst)

    sh, sc, g = sh_ref[...], sc_ref[...], g_ref[...]
    hm, hsteps = tm // subtiles, steps // subtiles
    t_idx = lax.shift_right_logical(lax.broadcasted_iota(jnp.int32, (hm, POOL_GROUP_DIM), 0), nb.bit_length() - 1)
    subs = range(subtiles)
    rows = [slice(s_i * hm, (s_i + 1) * hm) for s_i in subs]

    xs = [x_ref[rs, :] for rs in rows]
    projs = [jnp.dot(_modulate(x, sh, sc).astype(BF16), win_ref[...], preferred_element_type=F32) for x in xs]
    u_gates = [p[:, POOL_WIDTH + LRU_WIDTH:] for p in projs]

    y_pools, coeffs = [], []
    for s_i, proj in zip(subs, projs):
        r0 = s_i * hm
        u_pool = proj[:, :POOL_WIDTH]
        u_lru = proj[:, POOL_WIDTH:POOL_WIDTH + LRU_WIDTH]

        zbuf[pool_hist + r0:pool_hist + r0 + hm, :] = u_pool
        pos1 = (pid * steps + s_i * hsteps + 1) + t_idx
        cur = zbuf[r0:r0 + pool_hist + hm, :]
        first, w = -POOL_BUF, 1
        pooled = []
        for gi, wnd in enumerate(POOL_WINDOWS):
            assert wnd == 2 * w
            n = cur.shape[0]
            cur = cur[w * nb:, :] + cur[:n - w * nb, :]
            first, w = first + w, wnd
            cols = slice(gi * POOL_GROUP_DIM, (gi + 1) * POOL_GROUP_DIM)
            s = cur[-first * nb:, :POOL_GROUP_DIM]
            pooled.append(s / jnp.minimum(pos1, wnd).astype(F32) - u_pool[:, cols])
            cur = cur[:, POOL_GROUP_DIM:]
        assert first == 0
        pooled = jnp.concatenate(pooled, axis=-1).astype(BF16)
        y_pools.append(_grouped_dot(pooled, wpool_ref, 0, len(POOL_WINDOWS)) * pscale_ref[...])

        cbuf[conv_hist + r0:conv_hist + r0 + hm, :] = u_lru
        xc = None
        for k in range(CONV_WIDTH):
            term = cbuf[r0 + k * nb:r0 + k * nb + hm, :] * cw_ref[k:k + 1, :]
            xc = term if xc is None else xc + term
        coeffs.append(_gates(cb_ref[...] + xc, wg_ref, ba_ref, bx_ref, lam_ref))

    hcur = hst[...]
    for s_i, (a, b) in zip(subs, coeffs):
        r0 = s_i * hm
        for t in range(hsteps):
            hcur = a[t * nb:(t + 1) * nb, :] * hcur + b[t * nb:(t + 1) * nb, :]
            hsbuf[r0 + t * nb:r0 + (t + 1) * nb, :] = hcur
    hst[...] = hcur

    for rs, x, y_pool, u_gate in zip(rows, xs, y_pools, u_gates):
        y_lru = hsbuf[rs, :] * jax.nn.gelu(u_gate)
        o_ref[rs, :] = _mix_out(x, g, y_pool, y_lru, wout_ref, lng_ref, lnb_ref)

    @pl.when(pid == pl.num_programs(0) - 1)
    def _final_state_and_sample_rows():
        npool_ref[...] = zbuf[tm:tm + pool_hist, :].reshape(POOL_BUF, nb, POOL_WIDTH)
        nconv_ref[...] = cbuf[tm:tm + conv_hist, :].reshape(CONV_WIDTH - 1, nb, LRU_WIDTH)
        nh_ref[...] = hcur
        _mixer_sample_rows(*sample_in, weights, *sample_out)

    zbuf[0:pool_hist, :] = zbuf[tm:tm + pool_hist, :]
    cbuf[0:conv_hist, :] = cbuf[tm:tm + conv_hist, :]


def _mixer_call(x2d, xs, mod, wts, sample_states, new_p, new_s, *, layer, steps, n_prompt, round_next=(),
                next_ada=None):
    assert len(wts) == N_MIXER_WEIGHTS
    rows = x2d.shape[0]
    n_sample = xs.shape[0]
    tm = steps * n_prompt
    n_grid = rows // tm
    rnd = [_round_specs(w, l, n_grid) for w, l, _ in round_next]
    ada_in, ada_out, ada_shape, ada_args = [], [], [], []
    if next_ada is not None:
        c_all, w_ada, b_ada, ada_layer = next_ada
        ada_in, out_spec, shape = _ada_specs(c_all, w_ada, ada_layer, _ada_cols(w_ada.shape[-1], n_grid), n_grid)
        ada_out, ada_shape, ada_args = [out_spec], [shape], [c_all, w_ada, b_ada]
    kern = functools.partial(_mixer_kernel, steps=steps, nb=n_prompt, subtiles=MIXER_SUBTILES,
                             interleave=tuple(il for _, _, il in round_next), next_mod=next_ada is not None)
    stacks = tuple(new_p) + tuple(new_s)
    n_in = 4 + len(wts) + 4 + len(sample_states) + len(ada_in) + len(rnd)
    any_spec = pl.BlockSpec(memory_space=pl.ANY)
    stack_specs = [_layer_block(a.shape, layer) for a in stacks]
    rows_spec = pl.BlockSpec((tm, D_MODEL), lambda i: (i, 0))
    return pl.pallas_call(
        kern,
        out_shape=[jax.ShapeDtypeStruct((rows, D_MODEL), F32)]
        + [jax.ShapeDtypeStruct(a.shape, F32) for a in new_p]
        + [jax.ShapeDtypeStruct(xs.shape, F32)]
        + [jax.ShapeDtypeStruct(a.shape, F32) for a in new_s]
        + ada_shape + [r[2] for r in rnd],
        grid=(n_grid,),
        in_specs=[rows_spec]
        + _mod_specs(1, False, n_sample, n_prompt)
        + [_resident(w, layer) for w in wts]
        + [_whole(xs.shape)] + _mod_specs(1, True, n_sample, n_prompt)
        + [_layer_block(a.shape, layer) for a in sample_states]
        + ada_in + [r[0] for r in rnd]
        + [any_spec] * len(stacks),
        out_specs=[rows_spec] + stack_specs[:3] + [_whole(xs.shape)] + stack_specs[3:] + ada_out
        + [r[1] for r in rnd],
        scratch_shapes=[
            pltpu.VMEM((POOL_BUF * n_prompt + tm, POOL_WIDTH), F32),
            pltpu.VMEM(((CONV_WIDTH - 1) * n_prompt + tm, LRU_WIDTH), F32),
            pltpu.VMEM((n_prompt, LRU_WIDTH), F32),
            pltpu.VMEM((tm, LRU_WIDTH), F32),
        ],
        input_output_aliases={n_in: 1, n_in + 1: 2, n_in + 2: 3, n_in + 3: 5, n_in + 4: 6, n_in + 5: 7},
        compiler_params=pltpu.CompilerParams(
            dimension_semantics=("arbitrary",),
            vmem_limit_bytes=MIXER_VMEM),
        name="mixer",
    )(x2d, mod, mod, mod, *wts, xs, mod, mod, mod, *sample_states, *ada_args, *[w for w, _, _ in round_next],
      *stacks)


def _lane_blocks(w):
    depth, heads, c, _ = w.shape
    per = LANES // c
    eye = jnp.eye(per, dtype=w.dtype)
    blocks = jnp.einsum('lbgij,gh->lbgihj', w.reshape(depth, heads // per, per, c, c), eye)
    return blocks.reshape(depth, heads // per, LANES, LANES)


def kernel(x_prompt, x_sample, state_pool, state_conv, state_rglru, c_prompt, c_sample, w_ada, b_ada, ffn1_wgu, ffn1_wdown, w_in, w_out, w_pool, pool_scale, conv_w, conv_b, w_rg_a, b_rg_a, w_rg_x, b_rg_x, lru_lambda, ffn2_wgu, ffn2_wdown, ln_g, ln_b):
    n_prompt, seq, _ = x_prompt.shape
    n_sample = x_sample.shape[0]
    assert x_sample.shape[1] == 1 and n_sample % n_prompt == 0 and n_prompt == SUBLANES
    assert seq % MIXER_STEPS == 0 and (seq * n_prompt) % FFN_TILE == 0 and FFN_TILE % FFN_TILE_SMALL == 0

    row = lambda v: v.reshape(DEPTH, 1, v.shape[-1])
    wpool_bd = _lane_blocks(w_pool).astype(BF16)
    wgate_bd = jnp.concatenate([_lane_blocks(w_rg_a), _lane_blocks(w_rg_x)], axis=1).astype(BF16)

    c_all = jnp.concatenate([c_sample, c_prompt], axis=0)
    b_ada3 = b_ada.reshape(DEPTH, 1, b_ada.shape[-1])
    mod, wgu_bf, wd_bf = _ada_call(c_all, w_ada, b_ada3, 0, round_next=((ffn1_wgu, 0, True), (ffn1_wdown, 0, False)))

    xp = x_prompt
    xs = x_sample.reshape(n_sample, D_MODEL)
    hist_major = lambda a: jnp.swapaxes(a, 1, 2)
    sample_states = (hist_major(state_pool), hist_major(state_conv), state_rglru)
    new_p = (jnp.zeros((DEPTH, POOL_BUF, n_prompt, POOL_WIDTH), F32),
             jnp.zeros((DEPTH, CONV_WIDTH - 1, n_prompt, LRU_WIDTH), F32),
             jnp.zeros((DEPTH, n_prompt, LRU_WIDTH), F32))
    new_s = tuple(jnp.zeros(a.shape, F32) for a in sample_states)

    for l in range(DEPTH):
        last = l == DEPTH - 1
        ffn = functools.partial(_ffn_call, layer=l, n_prompt=n_prompt)
        xp, xs, win_bf, wout_bf, wgu2_bf, wd2_bf = ffn(
            xp, xs, mod, wgu_bf, wd_bf, ln_g, ln_b, sub=0, ln_idx=0, batch_major_in=(l == 0),
            tm=FFN_TILE_SMALL if l == 0 else FFN_TILE,
            round_next=((w_in, l, False), (w_out, l, False), (ffn2_wgu, l, True), (ffn2_wdown, l, False)))

        mixer_w = (win_bf, wpool_bd, row(pool_scale), conv_w, row(conv_b), wgate_bd,
                   row(b_rg_a), row(b_rg_x), row(lru_lambda), wout_bf, ln_g, ln_b)
        outs = _mixer_call(xp, xs, mod, mixer_w, sample_states, new_p, new_s, layer=l, steps=MIXER_STEPS,
                           n_prompt=n_prompt)
        xp, new_p, xs, new_s = outs[0], outs[1:4], outs[4], outs[5:8]

        outs = ffn(xp, xs, mod, wgu2_bf, wd2_bf, ln_g, ln_b, sub=2, ln_idx=2, batch_major_out=last,
                   tm=FFN_TILE_SMALL,
                   round_next=() if last else ((ffn1_wgu, l + 1, True), (ffn1_wdown, l + 1, False)),
                   next_ada=None if last else (c_all, w_ada, b_ada3, l + 1))
        xp, xs = outs[:2]
        if not last:
            mod, wgu_bf, wd_bf = outs[2:]

    return (xp, xs.reshape(n_sample, 1, D_MODEL),
            hist_major(new_p[0]), hist_major(new_p[1]), new_p[2],
            hist_major(new_s[0]), hist_major(new_s[1]), new_s[2])
```

```python
import functools

import jax
import jax.numpy as jnp
from jax import lax
from jax.experimental import pallas as pl
from jax.experimental.pallas import tpu as pltpu

D_MODEL = 1024
DEPTH = 4
PAST_LEN = 16384
POOL_WIDTH = D_MODEL // 2
LRU_WIDTH = D_MODEL - POOL_WIDTH
IN_WIDTH = POOL_WIDTH + 2 * LRU_WIDTH
POOL_WINDOWS = (2, 4, 8, 16)
POOL_GROUP_DIM = POOL_WIDTH // len(POOL_WINDOWS)
POOL_BUF = max(POOL_WINDOWS) - 1
N_LRU_HEADS = 8
LRU_HEAD_DIM = LRU_WIDTH // N_LRU_HEADS
CONV_WIDTH = 4
LRU_C = 8.0
D_FF = 128 * int(round(8 * D_MODEL / 3 / 128))
N_MOD = 9
ALPHA = (2 * DEPTH) ** 0.25
LN_EPS = 1e-5

F32 = jnp.float32
BF16 = jnp.bfloat16

SUBLANES = 8
LANES = 128
BF16_SUBLANES = 16
MIB = 1024 * 1024
GU_BLOCKS = D_FF // LANES

FFN_TILE = 1024
FFN_TILE_SMALL = 512
FFN_SUBTILE = 256
MIXER_STEPS = 128
ADA_VMEM = 32 * MIB
FFN_VMEM = 56 * MIB
MIXER_VMEM = 56 * MIB


def _layer_norm(y, g, b):
    mu = jnp.mean(y, axis=-1, keepdims=True)
    d = y - mu
    var = jnp.mean(d * d, axis=-1, keepdims=True)
    return d * lax.rsqrt(var + LN_EPS) * g + b


def _sigmoid(x):
    return 0.5 * jnp.tanh(0.5 * x) + 0.5


def _silu(x):
    return x * _sigmoid(x)


def _softplus(x):
    return jnp.maximum(x, 0.0) + jnp.log1p(jnp.exp(-jnp.abs(x)))


def _per_batch(fn, x, *ms):
    rows, d = x.shape
    nb = ms[0].shape[0]
    if nb == rows:
        return fn(x, *ms)
    return fn(x.reshape(rows // nb, nb, d), *[m[None] for m in ms]).reshape(rows, d)


def _modulate(x, sh, sc):
    return _per_batch(lambda x_, sh_, sc_: x_ * (1.0 + sc_) + sh_, x, sh, sc)


def _gated_residual(x, g, y, gate_scale):
    return _per_batch(lambda y_, g_: (gate_scale * (1.0 + g_)) * y_, y, g) + ALPHA * x


def _mod_specs(sub, is_sample, n_sample, n_prompt):
    rows, block = (n_sample, 0) if is_sample else (n_prompt, n_sample // n_prompt)
    return [pl.BlockSpec((rows, D_MODEL), functools.partial(lambda i, k: (block, k), k=3 * sub + t))
            for t in range(3)]


def _resident(w, layer):
    if w.ndim == 2:
        return pl.BlockSpec(w.shape, lambda i: (0, 0), pipeline_mode=pl.Buffered(1))
    nd = w.ndim
    return pl.BlockSpec((None,) + tuple(w.shape[1:]), lambda i: (layer,) + (0,) * (nd - 1),
                        pipeline_mode=pl.Buffered(1))


def _layer_block(shape, layer):
    nd = len(shape)
    return pl.BlockSpec((None,) + tuple(shape[1:]), lambda i: (layer,) + (0,) * (nd - 1))


def _whole(shape):
    return pl.BlockSpec(tuple(shape), lambda i: (0,) * len(shape))


def _round_specs(w, layer, n_steps, step=lambda i: i):
    rows, cols = w.shape[1:]
    n_chunks = next(n for n in range(n_steps, 0, -1) if rows % (n * BF16_SUBLANES) == 0)
    chunk = rows // n_chunks
    idx = lambda *ids: jnp.minimum(step(*ids), n_chunks - 1)
    return (pl.BlockSpec((None, chunk, cols), lambda *ids: (layer, idx(*ids), 0)),
            pl.BlockSpec((chunk, cols), lambda *ids: (idx(*ids), 0)),
            jax.ShapeDtypeStruct((rows, cols), BF16))


def _round_chunks(src_refs, dst_refs, interleave):
    for src, dst, il in zip(src_refs, dst_refs, interleave):
        if il:
            for i in range(GU_BLOCKS):
                dst[:, 2 * i * LANES:(2 * i + 1) * LANES] = src[:, i * LANES:(i + 1) * LANES].astype(BF16)
                dst[:, (2 * i + 1) * LANES:(2 * i + 2) * LANES] = src[:, D_FF + i * LANES:D_FF + (i + 1) * LANES].astype(BF16)
        else:
            dst[...] = src[...].astype(BF16)


def _split(refs, *counts):
    out, pos = [], 0
    for c in counts:
        out.append(refs[pos:pos + c])
        pos += c
    out.append(refs[pos:])
    return out


def _ada_cols(n_out, n_steps):
    return next(tn for tn in range(LANES, n_out + 1, LANES) if n_out % tn == 0 and n_out // tn <= n_steps)


def _ada_tile(c_ref, w_ref, b_ref, o_ref):
    sc = _silu(c_ref[...]).astype(BF16)
    o_ref[...] = jnp.dot(sc, w_ref[...].astype(BF16), preferred_element_type=F32) + b_ref[...]


def _ada_specs(c_all, w_ada, layer, tn, n_steps):
    n_rows, n_out = c_all.shape[0], w_ada.shape[-1]
    n_tiles = n_out // tn
    assert n_tiles <= n_steps and n_out % tn == 0
    idx = lambda i: jnp.minimum(i, n_tiles - 1)
    return ([_whole(c_all.shape),
             pl.BlockSpec((None, D_MODEL, tn), lambda i: (layer, 0, idx(i))),
             pl.BlockSpec((None, 1, tn), lambda i: (layer, 0, idx(i)))],
            pl.BlockSpec((n_rows, tn), lambda i: (0, idx(i))),
            jax.ShapeDtypeStruct((n_rows, n_out), F32))


def _ada_kernel(*refs, interleave):
    n_round = len(interleave)
    ada_in, round_in, (o_ref,), round_out = _split(refs, 3, n_round, 1)
    _round_chunks(round_in, round_out, interleave)
    _ada_tile(*ada_in, o_ref)


def _ada_call(c_all, w_ada, b_ada, layer, round_next=()):
    n_col = 8
    ada_in, ada_out, ada_shape = _ada_specs(c_all, w_ada, layer, w_ada.shape[-1] // n_col, n_col)
    rnd = [_round_specs(w, l, n_col) for w, l, _ in round_next]
    return pl.pallas_call(
        functools.partial(_ada_kernel, interleave=tuple(il for _, _, il in round_next)),
        out_shape=[ada_shape] + [r[2] for r in rnd],
        grid=(n_col,),
        in_specs=ada_in + [r[0] for r in rnd],
        out_specs=[ada_out] + [r[1] for r in rnd],
        compiler_params=pltpu.CompilerParams(
            dimension_semantics=("arbitrary",),
            vmem_limit_bytes=ADA_VMEM),
        name="ada",
    )(c_all, w_ada, b_ada, *[w for w, _, _ in round_next])


def _swiglu_act(gu):
    blocks = [_silu(gu[:, 2 * i * LANES:(2 * i + 1) * LANES]) * gu[:, (2 * i + 1) * LANES:(2 * i + 2) * LANES]
              for i in range(GU_BLOCKS)]
    return jnp.concatenate(blocks, axis=-1)


def _ffn_rows(x, sh, sc, g, wgu_ref, wd_ref, lng, lnb):
    h = _modulate(x, sh, sc).astype(BF16)
    gu = jnp.dot(h, wgu_ref[...], preferred_element_type=F32)
    act = _swiglu_act(gu).astype(BF16)
    acc = jnp.dot(act, wd_ref[...], preferred_element_type=F32)
    return _layer_norm(_gated_residual(x, g, acc, 0.5), lng, lnb)


def _ffn_kernel(*refs, ln_idx, subtiles, batch_major_in, batch_major_out, interleave, next_mod):
    n_round, n_ada = len(interleave), 3 if next_mod else 0
    (x_ref, sh_ref, sc_ref, g_ref, wgu_ref, wd_ref, lng_ref, lnb_ref), sample_in, ada_in, round_in, \
        (o_ref, os_ref), ada_out, round_out, lbuf = _split(refs, 8, 4, n_ada, n_round, 2, n_ada // 3, n_round)
    _round_chunks(round_in, round_out, interleave)
    if next_mod:
        _ada_tile(*ada_in, *ada_out)
    lng, lnb = lng_ref[ln_idx:ln_idx + 1, :], lnb_ref[ln_idx:ln_idx + 1, :]
    sh, sc, g = sh_ref[...], sc_ref[...], g_ref[...]
    nb = sh.shape[0]
    rows = (x_ref.shape[0] * x_ref.shape[1] if batch_major_in else x_ref.shape[0]) // subtiles
    ts = rows // nb
    for s_i in range(subtiles):
        steps = slice(s_i * ts, (s_i + 1) * ts)
        if batch_major_in:
            for b in range(nb):
                lbuf[0][steps, b, :] = x_ref[b, steps, :]
            x = lbuf[0][steps, :, :].reshape(rows, D_MODEL)
        else:
            x = x_ref[s_i * rows:(s_i + 1) * rows, :]
        y = _ffn_rows(x, sh, sc, g, wgu_ref, wd_ref, lng, lnb)
        if batch_major_out:
            lbuf[0][steps, :, :] = y.reshape(ts, nb, D_MODEL)
            for b in range(nb):
                o_ref[b, steps, :] = lbuf[0][steps, b, :]
        else:
            o_ref[s_i * rows:(s_i + 1) * rows, :] = y

    @pl.when(pl.program_id(0) == pl.num_programs(0) - 1)
    def _sample_rows():
        xs_ref, shs_ref, scs_ref, gs_ref = sample_in
        os_ref[...] = _ffn_rows(xs_ref[...], shs_ref[...], scs_ref[...], gs_ref[...], wgu_ref, wd_ref, lng, lnb)


def _ffn_call(x, xs, mod, wgu, wd, ln_g, ln_b, *, layer, sub, ln_idx, tm, n_prompt,
              batch_major_in=False, batch_major_out=False, round_next=(), next_ada=None):
    rows = x.shape[0] * x.shape[1] if batch_major_in else x.shape[0]
    n_sample = xs.shape[0]
    steps = tm // n_prompt
    n_grid = rows // tm
    major_spec = pl.BlockSpec((n_prompt, steps, D_MODEL), lambda i: (0, i, 0))
    rows_spec = pl.BlockSpec((tm, D_MODEL), lambda i: (i, 0))
    out_shape = (n_prompt, rows // n_prompt, D_MODEL) if batch_major_out else (rows, D_MODEL)
    rnd = [_round_specs(w, l, n_grid) for w, l, _ in round_next]
    ada_in, ada_out, ada_shape, ada_args = [], [], [], []
    if next_ada is not None:
        c_all, w_ada, b_ada, ada_layer = next_ada
        ada_in, out_spec, shape = _ada_specs(c_all, w_ada, ada_layer, _ada_cols(w_ada.shape[-1], n_grid), n_grid)
        ada_out, ada_shape, ada_args = [out_spec], [shape], [c_all, w_ada, b_ada]
    kern = functools.partial(_ffn_kernel, ln_idx=ln_idx, subtiles=tm // FFN_SUBTILE,
                             batch_major_in=batch_major_in, batch_major_out=batch_major_out,
                             interleave=tuple(il for _, _, il in round_next), next_mod=next_ada is not None)
    return pl.pallas_call(
        kern,
        out_shape=[jax.ShapeDtypeStruct(out_shape, F32), jax.ShapeDtypeStruct(xs.shape, F32)] + ada_shape
        + [r[2] for r in rnd],
        grid=(n_grid,),
        in_specs=[major_spec if batch_major_in else rows_spec]
        + _mod_specs(sub, False, n_sample, n_prompt)
        + [_resident(wgu, layer), _resident(wd, layer), _resident(ln_g, layer), _resident(ln_b, layer)]
        + [_whole(xs.shape)] + _mod_specs(sub, True, n_sample, n_prompt)
        + ada_in + [r[0] for r in rnd],
        out_specs=[major_spec if batch_major_out else rows_spec, _whole(xs.shape)] + ada_out + [r[1] for r in rnd],
        scratch_shapes=[pltpu.VMEM((steps, n_prompt, D_MODEL), F32)] if batch_major_in or batch_major_out else [],
        compiler_params=pltpu.CompilerParams(
            dimension_semantics=("arbitrary",),
            vmem_limit_bytes=FFN_VMEM),
        name=f"ffn{sub}",
    )(x, mod, mod, mod, wgu, wd, ln_g, ln_b, xs, mod, mod, mod, *ada_args, *[w for w, _, _ in round_next])


def _grouped_dot(x, w_ref, first, n):
    c = w_ref.shape[-1]
    return jnp.concatenate([jnp.dot(x[:, i * c:(i + 1) * c], w_ref[first + i], preferred_element_type=F32)
                            for i in range(n)], axis=-1)


def _gates(xc, wg_ref, ba_ref, bx_ref, lam_ref):
    xb = xc.astype(BF16)
    n = wg_ref.shape[0] // 2
    r = _sigmoid(_grouped_dot(xb, wg_ref, 0, n) + ba_ref[...])
    i = _sigmoid(_grouped_dot(xb, wg_ref, n, n) + bx_ref[...])
    log_a = (-LRU_C * r) * _softplus(-lam_ref[...])
    a = jnp.exp(log_a)
    t = jnp.tanh(log_a)
    q = (-2.0 * t) / (1.0 - t)
    mult = jnp.where(q == 0.0, 0.0, q * lax.rsqrt(q))
    return a, mult * (i * xc)


def _mix_out(x, g, y_pool, y_lru, wout_ref, lng_ref, lnb_ref):
    mix = jnp.dot(y_pool.astype(BF16), wout_ref[:POOL_WIDTH, :], preferred_element_type=F32)
    mix = mix + jnp.dot(y_lru.astype(BF16), wout_ref[POOL_WIDTH:, :], preferred_element_type=F32)
    y = _gated_residual(x, g, mix, 1.0)
    return _layer_norm(y, lng_ref[1:2, :], lnb_ref[1:2, :])


N_MIXER_WEIGHTS = 12


def _mixer_sample_rows(x_ref, sh_ref, sc_ref, g_ref, pst_ref, cst_ref, h0_ref, weights,
                       o_ref, npool_ref, nconv_ref, nh_ref):
    (win_ref, wpool_ref, pscale_ref, cw_ref, cb_ref, wg_ref, ba_ref, bx_ref, lam_ref,
     wout_ref, lng_ref, lnb_ref) = weights
    x = x_ref[...]
    h = _modulate(x, sh_ref[...], sc_ref[...]).astype(BF16)
    proj = jnp.dot(h, win_ref[...], preferred_element_type=F32)
    u_pool = proj[:, :POOL_WIDTH]
    u_lru = proj[:, POOL_WIDTH:POOL_WIDTH + LRU_WIDTH]
    u_gate = proj[:, POOL_WIDTH + LRU_WIDTH:]

    pooled = []
    for gi, w in enumerate(POOL_WINDOWS):
        cols = slice(gi * POOL_GROUP_DIM, (gi + 1) * POOL_GROUP_DIM)
        s = u_pool[:, cols]
        for back in range(1, w):
            s = s + pst_ref[POOL_BUF - back, :, cols]
        cnt = float(min(PAST_LEN + 1, w))
        pooled.append(s / cnt - u_pool[:, cols])
    pooled = jnp.concatenate(pooled, axis=-1).astype(BF16)
    y_pool = _grouped_dot(pooled, wpool_ref, 0, len(POOL_WINDOWS)) * pscale_ref[...]
    npool_ref[0:POOL_BUF - 1] = pst_ref[1:POOL_BUF]
    npool_ref[POOL_BUF - 1] = u_pool

    xc = None
    for k in range(CONV_WIDTH - 1):
        term = cst_ref[k] * cw_ref[k:k + 1, :]
        xc = term if xc is None else xc + term
    xc = cb_ref[...] + (xc + u_lru * cw_ref[CONV_WIDTH - 1:CONV_WIDTH, :])
    nconv_ref[0:CONV_WIDTH - 2] = cst_ref[1:CONV_WIDTH - 1]
    nconv_ref[CONV_WIDTH - 2] = u_lru

    a, b = _gates(xc, wg_ref, ba_ref, bx_ref, lam_ref)
    hs = a * h0_ref[...] + b
    nh_ref[...] = hs
    y_lru = hs * jax.nn.gelu(u_gate)

    o_ref[...] = _mix_out(x, g_ref[...], y_pool, y_lru, wout_ref, lng_ref, lnb_ref)


def _mixer_kernel(*refs, steps, nb):
    (x_ref, sh_ref, sc_ref, g_ref), weights, sample_in, _aliased, \
        (o_ref, npool_ref, nconv_ref, nh_ref), sample_out, (zbuf, cbuf, hst, hsbuf) = \
        _split(refs, 4, N_MIXER_WEIGHTS, 7, 6, 4, 4)
    (win_ref, wpool_ref, pscale_ref, cw_ref, cb_ref, wg_ref, ba_ref, bx_ref, lam_ref,
     wout_ref, lng_ref, lnb_ref) = weights

    pid = pl.program_id(0)
    tm = steps * nb
    pool_hist = POOL_BUF * nb
    conv_hist = (CONV_WIDTH - 1) * nb

    @pl.when(pid == 0)
    def _new_sequences():
        zbuf[0:pool_hist, :] = jnp.zeros((pool_hist, POOL_WIDTH), F32)
        cbuf[0:conv_hist, :] = jnp.zeros((conv_hist, LRU_WIDTH), F32)
        hst[...] = jnp.zeros_like(hst)

    x = x_ref[...]
    proj = jnp.dot(_modulate(x, sh_ref[...], sc_ref[...]).astype(BF16), win_ref[...], preferred_element_type=F32)
    u_pool = proj[:, :POOL_WIDTH]
    u_lru = proj[:, POOL_WIDTH:POOL_WIDTH + LRU_WIDTH]
    u_gate = proj[:, POOL_WIDTH + LRU_WIDTH:]

    zbuf[pool_hist:pool_hist + tm, :] = u_pool
    t_idx = lax.shift_right_logical(lax.broadcasted_iota(jnp.int32, (tm, POOL_GROUP_DIM), 0), nb.bit_length() - 1)
    pos1 = (pid * steps + 1) + t_idx
    cur = zbuf[...]
    first, w = -POOL_BUF, 1
    pooled = []
    for gi, wnd in enumerate(POOL_WINDOWS):
        assert wnd == 2 * w
        n = cur.shape[0]
        cur = cur[w * nb:, :] + cur[:n - w * nb, :]
        first, w = first + w, wnd
        cols = slice(gi * POOL_GROUP_DIM, (gi + 1) * POOL_GROUP_DIM)
        s = cur[-first * nb:, :POOL_GROUP_DIM]
        pooled.append(s / jnp.minimum(pos1, wnd).astype(F32) - u_pool[:, cols])
        cur = cur[:, POOL_GROUP_DIM:]
    assert first == 0
    pooled = jnp.concatenate(pooled, axis=-1).astype(BF16)
    y_pool = _grouped_dot(pooled, wpool_ref, 0, len(POOL_WINDOWS)) * pscale_ref[...]

    cbuf[conv_hist:conv_hist + tm, :] = u_lru
    xc = None
    for k in range(CONV_WIDTH):
        term = cbuf[k * nb:k * nb + tm, :] * cw_ref[k:k + 1, :]
        xc = term if xc is None else xc + term
    a, b = _gates(cb_ref[...] + xc, wg_ref, ba_ref, bx_ref, lam_ref)

    hcur = hst[...]
    for t in range(steps):
        hcur = a[t * nb:(t + 1) * nb, :] * hcur + b[t * nb:(t + 1) * nb, :]
        hsbuf[t * nb:(t + 1) * nb, :] = hcur
    hst[...] = hcur

    y_lru = hsbuf[...] * jax.nn.gelu(u_gate)
    o_ref[...] = _mix_out(x, g_ref[...], y_pool, y_lru, wout_ref, lng_ref, lnb_ref)

    @pl.when(pid == pl.num_programs(0) - 1)
    def _final_state_and_sample_rows():
        npool_ref[...] = zbuf[tm:tm + pool_hist, :].reshape(POOL_BUF, nb, POOL_WIDTH)
        nconv_ref[...] = cbuf[tm:tm + conv_hist, :].reshape(CONV_WIDTH - 1, nb, LRU_WIDTH)
        nh_ref[...] = hcur
        _mixer_sample_rows(*sample_in, weights, *sample_out)

    zbuf[0:pool_hist, :] = zbuf[tm:tm + pool_hist, :]
    cbuf[0:conv_hist, :] = cbuf[tm:tm + conv_hist, :]


def _mixer_call(x2d, xs, mod, wts, sample_states, new_p, new_s, *, layer, steps, n_prompt):
    assert len(wts) == N_MIXER_WEIGHTS
    rows = x2d.shape[0]
    n_sample = xs.shape[0]
    tm = steps * n_prompt
    stacks = tuple(new_p) + tuple(new_s)
    n_in = 4 + len(wts) + 4 + len(sample_states)
    any_spec = pl.BlockSpec(memory_space=pl.ANY)
    stack_specs = [_layer_block(a.shape, layer) for a in stacks]
    rows_spec = pl.BlockSpec((tm, D_MODEL), lambda i: (i, 0))
    return pl.pallas_call(
        functools.partial(_mixer_kernel, steps=steps, nb=n_prompt),
        out_shape=[jax.ShapeDtypeStruct((rows, D_MODEL), F32)]
        + [jax.ShapeDtypeStruct(a.shape, F32) for a in new_p]
        + [jax.ShapeDtypeStruct(xs.shape, F32)]
        + [jax.ShapeDtypeStruct(a.shape, F32) for a in new_s],
        grid=(rows // tm,),
        in_specs=[rows_spec]
        + _mod_specs(1, False, n_sample, n_prompt)
        + [_resident(w, layer) for w in wts]
        + [_whole(xs.shape)] + _mod_specs(1, True, n_sample, n_prompt)
        + [_layer_block(a.shape, layer) for a in sample_states]
        + [any_spec] * len(stacks),
        out_specs=[rows_spec] + stack_specs[:3] + [_whole(xs.shape)] + stack_specs[3:],
        scratch_shapes=[
            pltpu.VMEM((POOL_BUF * n_prompt + tm, POOL_WIDTH), F32),
            pltpu.VMEM(((CONV_WIDTH - 1) * n_prompt + tm, LRU_WIDTH), F32),
            pltpu.VMEM((n_prompt, LRU_WIDTH), F32),
            pltpu.VMEM((tm, LRU_WIDTH), F32),
        ],
        input_output_aliases={n_in: 1, n_in + 1: 2, n_in + 2: 3, n_in + 3: 5, n_in + 4: 6, n_in + 5: 7},
        compiler_params=pltpu.CompilerParams(
            dimension_semantics=("arbitrary",),
            vmem_limit_bytes=MIXER_VMEM),
        name="mixer",
    )(x2d, mod, mod, mod, *wts, xs, mod, mod, mod, *sample_states, *stacks)


def _lane_blocks(w):
    depth, heads, c, _ = w.shape
    per = LANES // c
    eye = jnp.eye(per, dtype=w.dtype)
    blocks = jnp.einsum('lbgij,gh->lbgihj', w.reshape(depth, heads // per, per, c, c), eye)
    return blocks.reshape(depth, heads // per, LANES, LANES)


def kernel(x_prompt, x_sample, state_pool, state_conv, state_rglru, c_prompt, c_sample, w_ada, b_ada, ffn1_wgu, ffn1_wdown, w_in, w_out, w_pool, pool_scale, conv_w, conv_b, w_rg_a, b_rg_a, w_rg_x, b_rg_x, lru_lambda, ffn2_wgu, ffn2_wdown, ln_g, ln_b):
    n_prompt, seq, _ = x_prompt.shape
    n_sample = x_sample.shape[0]
    assert x_sample.shape[1] == 1 and n_sample % n_prompt == 0 and n_prompt == SUBLANES
    assert seq % MIXER_STEPS == 0 and (seq * n_prompt) % FFN_TILE == 0 and FFN_TILE % FFN_TILE_SMALL == 0

    row = lambda v: v.reshape(DEPTH, 1, v.shape[-1])
    wpool_bd = _lane_blocks(w_pool).astype(BF16)
    wgate_bd = jnp.concatenate([_lane_blocks(w_rg_a), _lane_blocks(w_rg_x)], axis=1).astype(BF16)

    c_all = jnp.concatenate([c_sample, c_prompt], axis=0)
    b_ada3 = b_ada.reshape(DEPTH, 1, b_ada.shape[-1])
    mod, wgu_bf, wd_bf = _ada_call(c_all, w_ada, b_ada3, 0, round_next=((ffn1_wgu, 0, True), (ffn1_wdown, 0, False)))

    xp = x_prompt
    xs = x_sample.reshape(n_sample, D_MODEL)
    hist_major = lambda a: jnp.swapaxes(a, 1, 2)
    sample_states = (hist_major(state_pool), hist_major(state_conv), state_rglru)
    new_p = (jnp.zeros((DEPTH, POOL_BUF, n_prompt, POOL_WIDTH), F32),
             jnp.zeros((DEPTH, CONV_WIDTH - 1, n_prompt, LRU_WIDTH), F32),
             jnp.zeros((DEPTH, n_prompt, LRU_WIDTH), F32))
    new_s = tuple(jnp.zeros(a.shape, F32) for a in sample_states)

    for l in range(DEPTH):
        last = l == DEPTH - 1
        ffn = functools.partial(_ffn_call, layer=l, n_prompt=n_prompt)
        xp, xs, win_bf, wout_bf, wgu2_bf, wd2_bf = ffn(
            xp, xs, mod, wgu_bf, wd_bf, ln_g, ln_b, sub=0, ln_idx=0, batch_major_in=(l == 0),
            tm=FFN_TILE_SMALL if l == 0 else FFN_TILE,
            round_next=((w_in, l, False), (w_out, l, False), (ffn2_wgu, l, True), (ffn2_wdown, l, False)))

        mixer_w = (win_bf, wpool_bd, row(pool_scale), conv_w, row(conv_b), wgate_bd,
                   row(b_rg_a), row(b_rg_x), row(lru_lambda), wout_bf, ln_g, ln_b)
        outs = _mixer_call(xp, xs, mod, mixer_w, sample_states, new_p, new_s, layer=l, steps=MIXER_STEPS,
                           n_prompt=n_prompt)
        xp, new_p, xs, new_s = outs[0], outs[1:4], outs[4], outs[5:8]

        outs = ffn(xp, xs, mod, wgu2_bf, wd2_bf, ln_g, ln_b, sub=2, ln_idx=2, batch_major_out=last,
                   tm=FFN_TILE_SMALL,
                   round_next=() if last else ((ffn1_wgu, l + 1, True), (ffn1_wdown, l + 1, False)),
                   next_ada=None if last else (c_all, w_ada, b_ada3, l + 1))
        xp, xs = outs[:2]
        if not last:
            mod, wgu_bf, wd_bf = outs[2:]

    return (xp, xs.reshape(n_sample, 1, D_MODEL),
            hist_major(new_p[0]), hist_major(new_p[1]), new_p[2],
            hist_major(new_s[0]), hist_major(new_s[1]), new_s[2])
```

```python
import functools

import jax
import jax.numpy as jnp
from jax import lax
from jax.experimental import pallas as pl
from jax.experimental.pallas import tpu as pltpu

D_MODEL = 1024
DEPTH = 4
PAST_LEN = 16384
POOL_WIDTH = D_MODEL // 2
LRU_WIDTH = D_MODEL - POOL_WIDTH
IN_WIDTH = POOL_WIDTH + 2 * LRU_WIDTH
POOL_WINDOWS = (2, 4, 8, 16)
POOL_GROUP_DIM = POOL_WIDTH // len(POOL_WINDOWS)
POOL_BUF = max(POOL_WINDOWS) - 1
N_LRU_HEADS = 8
LRU_HEAD_DIM = LRU_WIDTH // N_LRU_HEADS
CONV_WIDTH = 4
LRU_C = 8.0
D_FF = 128 * int(round(8 * D_MODEL / 3 / 128))
N_MOD = 9
ALPHA = (2 * DEPTH) ** 0.25
LN_EPS = 1e-5

F32 = jnp.float32
BF16 = jnp.bfloat16

SUBLANES = 8
LANES = 128
BF16_SUBLANES = 16
MIB = 1024 * 1024
GU_BLOCKS = D_FF // LANES

FFN_TILE = 1024
FFN_TILE_SMALL = 512
FFN_SUBTILE = 256
MIXER_STEPS = 128
ADA_VMEM = 32 * MIB
FFN_VMEM = 56 * MIB
MIXER_VMEM = 56 * MIB


def _layer_norm(y, g, b):
    mu = jnp.mean(y, axis=-1, keepdims=True)
    d = y - mu
    var = jnp.mean(d * d, axis=-1, keepdims=True)
    return d * lax.rsqrt(var + LN_EPS) * g + b


def _sigmoid(x):
    return 0.5 * jnp.tanh(0.5 * x) + 0.5


def _silu(x):
    return x * _sigmoid(x)


def _softplus(x):
    return jnp.maximum(x, 0.0) + jnp.log1p(jnp.exp(-jnp.abs(x)))


def _per_batch(fn, x, *ms):
    rows, d = x.shape
    nb = ms[0].shape[0]
    if nb == rows:
        return fn(x, *ms)
    return fn(x.reshape(rows // nb, nb, d), *[m[None] for m in ms]).reshape(rows, d)


def _modulate(x, sh, sc):
    return _per_batch(lambda x_, sh_, sc_: x_ * (1.0 + sc_) + sh_, x, sh, sc)


def _gated_residual(x, g, y, gate_scale):
    return _per_batch(lambda y_, g_: (gate_scale * (1.0 + g_)) * y_, y, g) + ALPHA * x


def _mod_specs(sub, is_sample, n_sample, n_prompt):
    rows, block = (n_sample, 0) if is_sample else (n_prompt, n_sample // n_prompt)
    return [pl.BlockSpec((rows, D_MODEL), functools.partial(lambda i, k: (block, k), k=3 * sub + t))
            for t in range(3)]


def _resident(w, layer):
    if w.ndim == 2:
        return pl.BlockSpec(w.shape, lambda i: (0, 0), pipeline_mode=pl.Buffered(1))
    nd = w.ndim
    return pl.BlockSpec((None,) + tuple(w.shape[1:]), lambda i: (layer,) + (0,) * (nd - 1),
                        pipeline_mode=pl.Buffered(1))


def _layer_block(shape, layer):
    nd = len(shape)
    return pl.BlockSpec((None,) + tuple(shape[1:]), lambda i: (layer,) + (0,) * (nd - 1))


def _whole(shape):
    return pl.BlockSpec(tuple(shape), lambda i: (0,) * len(shape))


def _round_specs(w, layer, n_steps, step=lambda i: i):
    rows, cols = w.shape[1:]
    n_chunks = next(n for n in range(n_steps, 0, -1) if rows % (n * BF16_SUBLANES) == 0)
    chunk = rows // n_chunks
    idx = lambda *ids: jnp.minimum(step(*ids), n_chunks - 1)
    return (pl.BlockSpec((None, chunk, cols), lambda *ids: (layer, idx(*ids), 0)),
            pl.BlockSpec((chunk, cols), lambda *ids: (idx(*ids), 0)),
            jax.ShapeDtypeStruct((rows, cols), BF16))


def _round_chunks(src_refs, dst_refs, interleave):
    for src, dst, il in zip(src_refs, dst_refs, interleave):
        if il:
            for i in range(GU_BLOCKS):
                dst[:, 2 * i * LANES:(2 * i + 1) * LANES] = src[:, i * LANES:(i + 1) * LANES].astype(BF16)
                dst[:, (2 * i + 1) * LANES:(2 * i + 2) * LANES] = src[:, D_FF + i * LANES:D_FF + (i + 1) * LANES].astype(BF16)
        else:
            dst[...] = src[...].astype(BF16)


def _split(refs, *counts):
    out, pos = [], 0
    for c in counts:
        out.append(refs[pos:pos + c])
        pos += c
    out.append(refs[pos:])
    return out


def _ada_cols(n_out, n_steps):
    return next(tn for tn in range(LANES, n_out + 1, LANES) if n_out % tn == 0 and n_out // tn <= n_steps)


def _ada_tile(c_ref, w_ref, b_ref, o_ref):
    sc = _silu(c_ref[...]).astype(BF16)
    o_ref[...] = jnp.dot(sc, w_ref[...].astype(BF16), preferred_element_type=F32) + b_ref[...]


def _ada_specs(c_all, w_ada, layer, tn, n_steps):
    n_rows, n_out = c_all.shape[0], w_ada.shape[-1]
    n_tiles = n_out // tn
    assert n_tiles <= n_steps and n_out % tn == 0
    idx = lambda i: jnp.minimum(i, n_tiles - 1)
    return ([_whole(c_all.shape),
             pl.BlockSpec((None, D_MODEL, tn), lambda i: (layer, 0, idx(i))),
             pl.BlockSpec((None, 1, tn), lambda i: (layer, 0, idx(i)))],
            pl.BlockSpec((n_rows, tn), lambda i: (0, idx(i))),
            jax.ShapeDtypeStruct((n_rows, n_out), F32))


def _ada_kernel(*refs, interleave):
    n_round = len(interleave)
    ada_in, round_in, (o_ref,), round_out = _split(refs, 3, n_round, 1)
    _round_chunks(round_in, round_out, interleave)
    _ada_tile(*ada_in, o_ref)


def _ada_call(c_all, w_ada, b_ada, layer, round_next=()):
    n_col = 8
    ada_in, ada_out, ada_shape = _ada_specs(c_all, w_ada, layer, w_ada.shape[-1] // n_col, n_col)
    rnd = [_round_specs(w, l, n_col) for w, l, _ in round_next]
    return pl.pallas_call(
        functools.partial(_ada_kernel, interleave=tuple(il for _, _, il in round_next)),
        out_shape=[ada_shape] + [r[2] for r in rnd],
        grid=(n_col,),
        in_specs=ada_in + [r[0] for r in rnd],
        out_specs=[ada_out] + [r[1] for r in rnd],
        compiler_params=pltpu.CompilerParams(
            dimension_semantics=("arbitrary",),
            vmem_limit_bytes=ADA_VMEM),
        name="ada",
    )(c_all, w_ada, b_ada, *[w for w, _, _ in round_next])


def _swiglu_act(gu):
    blocks = [_silu(gu[:, 2 * i * LANES:(2 * i + 1) * LANES]) * gu[:, (2 * i + 1) * LANES:(2 * i + 2) * LANES]
              for i in range(GU_BLOCKS)]
    return jnp.concatenate(blocks, axis=-1)


def _ffn_rows(x, sh, sc, g, wgu_ref, wd_ref, lng, lnb):
    h = _modulate(x, sh, sc).astype(BF16)
    gu = jnp.dot(h, wgu_ref[...], preferred_element_type=F32)
    act = _swiglu_act(gu).astype(BF16)
    acc = jnp.dot(act, wd_ref[...], preferred_element_type=F32)
    return _layer_norm(_gated_residual(x, g, acc, 0.5), lng, lnb)


def _ffn_kernel(*refs, ln_idx, subtiles, batch_major_in, batch_major_out, interleave, next_mod):
    n_round, n_ada = len(interleave), 3 if next_mod else 0
    (x_ref, sh_ref, sc_ref, g_ref, wgu_ref, wd_ref, lng_ref, lnb_ref), sample_in, ada_in, round_in, \
        (o_ref, os_ref), ada_out, round_out, lbuf = _split(refs, 8, 4, n_ada, n_round, 2, n_ada // 3, n_round)
    _round_chunks(round_in, round_out, interleave)
    if next_mod:
        _ada_tile(*ada_in, *ada_out)
    lng, lnb = lng_ref[ln_idx:ln_idx + 1, :], lnb_ref[ln_idx:ln_idx + 1, :]
    sh, sc, g = sh_ref[...], sc_ref[...], g_ref[...]
    nb = sh.shape[0]
    rows = (x_ref.shape[0] * x_ref.shape[1] if batch_major_in else x_ref.shape[0]) // subtiles
    ts = rows // nb
    for s_i in range(subtiles):
        steps = slice(s_i * ts, (s_i + 1) * ts)
        if batch_major_in:
            for b in range(nb):
                lbuf[0][steps, b, :] = x_ref[b, steps, :]
            x = lbuf[0][steps, :, :].reshape(rows, D_MODEL)
        else:
            x = x_ref[s_i * rows:(s_i + 1) * rows, :]
        y = _ffn_rows(x, sh, sc, g, wgu_ref, wd_ref, lng, lnb)
        if batch_major_out:
            lbuf[0][steps, :, :] = y.reshape(ts, nb, D_MODEL)
            for b in range(nb):
                o_ref[b, steps, :] = lbuf[0][steps, b, :]
        else:
            o_ref[s_i * rows:(s_i + 1) * rows, :] = y

    @pl.when(pl.program_id(0) == pl.num_programs(0) - 1)
    def _sample_rows():
        xs_ref, shs_ref, scs_ref, gs_ref = sample_in
        os_ref[...] = _ffn_rows(xs_ref[...], shs_ref[...], scs_ref[...], gs_ref[...], wgu_ref, wd_ref, lng, lnb)


def _ffn_call(x, xs, mod, wgu, wd, ln_g, ln_b, *, layer, sub, ln_idx, tm, n_prompt,
              batch_major_in=False, batch_major_out=False, round_next=(), next_ada=None):
    rows = x.shape[0] * x.shape[1] if batch_major_in else x.shape[0]
    n_sample = xs.shape[0]
    steps = tm // n_prompt
    n_grid = rows // tm
    major_spec = pl.BlockSpec((n_prompt, steps, D_MODEL), lambda i: (0, i, 0))
    rows_spec = pl.BlockSpec((tm, D_MODEL), lambda i: (i, 0))
    out_shape = (n_prompt, rows // n_prompt, D_MODEL) if batch_major_out else (rows, D_MODEL)
    rnd = [_round_specs(w, l, n_grid) for w, l, _ in round_next]
    ada_in, ada_out, ada_shape, ada_args = [], [], [], []
    if next_ada is not None:
        c_all, w_ada, b_ada, ada_layer = next_ada
        ada_in, out_spec, shape = _ada_specs(c_all, w_ada, ada_layer, _ada_cols(w_ada.shape[-1], n_grid), n_grid)
        ada_out, ada_shape, ada_args = [out_spec], [shape], [c_all, w_ada, b_ada]
    kern = functools.partial(_ffn_kernel, ln_idx=ln_idx, subtiles=tm // FFN_SUBTILE,
                             batch_major_in=batch_major_in, batch_major_out=batch_major_out,
                             interleave=tuple(il for _, _, il in round_next), next_mod=next_ada is not None)
    return pl.pallas_call(
        kern,
        out_shape=[jax.ShapeDtypeStruct(out_shape, F32), jax.ShapeDtypeStruct(xs.shape, F32)] + ada_shape
        + [r[2] for r in rnd],
        grid=(n_grid,),
        in_specs=[major_spec if batch_major_in else rows_spec]
        + _mod_specs(sub, False, n_sample, n_prompt)
        + [_resident(wgu, layer), _resident(wd, layer), _resident(ln_g, layer), _resident(ln_b, layer)]
        + [_whole(xs.shape)] + _mod_specs(sub, True, n_sample, n_prompt)
        + ada_in + [r[0] for r in rnd],
        out_specs=[major_spec if batch_major_out else rows_spec, _whole(xs.shape)] + ada_out + [r[1] for r in rnd],
        scratch_shapes=[pltpu.VMEM((steps, n_prompt, D_MODEL), F32)] if batch_major_in or batch_major_out else [],
        compiler_params=pltpu.CompilerParams(
            dimension_semantics=("arbitrary",),
            vmem_limit_bytes=FFN_VMEM),
        name=f"ffn{sub}",
    )(x, mod, mod, mod, wgu, wd, ln_g, ln_b, xs, mod, mod, mod, *ada_args, *[w for w, _, _ in round_next])


def _grouped_dot(x, w_ref, first, n):
    c = w_ref.shape[-1]
    return jnp.concatenate([jnp.dot(x[:, i * c:(i + 1) * c], w_ref[first + i], preferred_element_type=F32)
                            for i in range(n)], axis=-1)


def _gates(xc, wg_ref, ba_ref, bx_ref, lam_ref):
    xb = xc.astype(BF16)
    n = wg_ref.shape[0] // 2
    r = _sigmoid(_grouped_dot(xb, wg_ref, 0, n) + ba_ref[...])
    i = _sigmoid(_grouped_dot(xb, wg_ref, n, n) + bx_ref[...])
    log_a = (-LRU_C * r) * _softplus(-lam_ref[...])
    a = jnp.exp(log_a)
    t = jnp.tanh(log_a)
    q = (-2.0 * t) / (1.0 - t)
    mult = jnp.where(q == 0.0, 0.0, q * lax.rsqrt(q))
    return a, mult * (i * xc)


def _mix_out(x, g, y_pool, y_lru, wout_ref, lng_ref, lnb_ref):
    mix = jnp.dot(y_pool.astype(BF16), wout_ref[:POOL_WIDTH, :], preferred_element_type=F32)
    mix = mix + jnp.dot(y_lru.astype(BF16), wout_ref[POOL_WIDTH:, :], preferred_element_type=F32)
    y = _gated_residual(x, g, mix, 1.0)
    return _layer_norm(y, lng_ref[1:2, :], lnb_ref[1:2, :])


N_MIXER_WEIGHTS = 12


def _mixer_sample_rows(x_ref, sh_ref, sc_ref, g_ref, pst_ref, cst_ref, h0_ref, weights,
                       o_ref, npool_ref, nconv_ref, nh_ref):
    (win_ref, wpool_ref, pscale_ref, cw_ref, cb_ref, wg_ref, ba_ref, bx_ref, lam_ref,
     wout_ref, lng_ref, lnb_ref) = weights
    x = x_ref[...]
    h = _modulate(x, sh_ref[...], sc_ref[...]).astype(BF16)
    proj = jnp.dot(h, win_ref[...], preferred_element_type=F32)
    u_pool = proj[:, :POOL_WIDTH]
    u_lru = proj[:, POOL_WIDTH:POOL_WIDTH + LRU_WIDTH]
    u_gate = proj[:, POOL_WIDTH + LRU_WIDTH:]

    pooled = []
    for gi, w in enumerate(POOL_WINDOWS):
        cols = slice(gi * POOL_GROUP_DIM, (gi + 1) * POOL_GROUP_DIM)
        s = u_pool[:, cols]
        for back in range(1, w):
            s = s + pst_ref[POOL_BUF - back, :, cols]
        cnt = float(min(PAST_LEN + 1, w))
        pooled.append(s / cnt - u_pool[:, cols])
    pooled = jnp.concatenate(pooled, axis=-1).astype(BF16)
    y_pool = _grouped_dot(pooled, wpool_ref, 0, len(POOL_WINDOWS)) * pscale_ref[...]
    npool_ref[0:POOL_BUF - 1] = pst_ref[1:POOL_BUF]
    npool_ref[POOL_BUF - 1] = u_pool

    xc = None
    for k in range(CONV_WIDTH - 1):
        term = cst_ref[k] * cw_ref[k:k + 1, :]
        xc = term if xc is None else xc + term
    xc = cb_ref[...] + (xc + u_lru * cw_ref[CONV_WIDTH - 1:CONV_WIDTH, :])
    nconv_ref[0:CONV_WIDTH - 2] = cst_ref[1:CONV_WIDTH - 1]
    nconv_ref[CONV_WIDTH - 2] = u_lru

    a, b = _gates(xc, wg_ref, ba_ref, bx_ref, lam_ref)
    hs = a * h0_ref[...] + b
    nh_ref[...] = hs
    y_lru = hs * jax.nn.gelu(u_gate)

    o_ref[...] = _mix_out(x, g_ref[...], y_pool, y_lru, wout_ref, lng_ref, lnb_ref)


def _mixer_kernel(*refs, steps, nb):
    (x_ref, sh_ref, sc_ref, g_ref), weights, sample_in, _aliased, \
        (o_ref, npool_ref, nconv_ref, nh_ref), sample_out, (zbuf, cbuf, hst, hsbuf) = \
        _split(refs, 4, N_MIXER_WEIGHTS, 7, 6, 4, 4)
    (win_ref, wpool_ref, pscale_ref, cw_ref, cb_ref, wg_ref, ba_ref, bx_ref, lam_ref,
     wout_ref, lng_ref, lnb_ref) = weights

    pid = pl.program_id(0)
    tm = steps * nb
    pool_hist = POOL_BUF * nb
    conv_hist = (CONV_WIDTH - 1) * nb

    @pl.when(pid == 0)
    def _new_sequences():
        zbuf[0:pool_hist, :] = jnp.zeros((pool_hist, POOL_WIDTH), F32)
        cbuf[0:conv_hist, :] = jnp.zeros((conv_hist, LRU_WIDTH), F32)
        hst[...] = jnp.zeros_like(hst)

    x = x_ref[...]
    proj = jnp.dot(_modulate(x, sh_ref[...], sc_ref[...]).astype(BF16), win_ref[...], preferred_element_type=F32)
    u_pool = proj[:, :POOL_WIDTH]
    u_lru = proj[:, POOL_WIDTH:POOL_WIDTH + LRU_WIDTH]
    u_gate = proj[:, POOL_WIDTH + LRU_WIDTH:]

    zbuf[pool_hist:pool_hist + tm, :] = u_pool
    t_idx = lax.shift_right_logical(lax.broadcasted_iota(jnp.int32, (tm, POOL_GROUP_DIM), 0), nb.bit_length() - 1)
    pos1 = (pid * steps + 1) + t_idx
    cur = zbuf[...]
    first, w = -POOL_BUF, 1
    pooled = []
    for gi, wnd in enumerate(POOL_WINDOWS):
        assert wnd == 2 * w
        n = cur.shape[0]
        cur = cur[w * nb:, :] + cur[:n - w * nb, :]
        first, w = first + w, wnd
        cols = slice(gi * POOL_GROUP_DIM, (gi + 1) * POOL_GROUP_DIM)
        s = cur[-first * nb:, :POOL_GROUP_DIM]
        pooled.append(s / jnp.minimum(pos1, wnd).astype(F32) - u_pool[:, cols])
        cur = cur[:, POOL_GROUP_DIM:]
    assert first == 0
    pooled = jnp.concatenate(pooled, axis=-1).astype(BF16)
    y_pool = _grouped_dot(pooled, wpool_ref, 0, len(POOL_WINDOWS)) * pscale_ref[...]

    cbuf[conv_hist:conv_hist + tm, :] = u_lru
    xc = None
    for k in range(CONV_WIDTH):
        term = cbuf[k * nb:k * nb + tm, :] * cw_ref[k:k + 1, :]
        xc = term if xc is None else xc + term
    a, b = _gates(cb_ref[...] + xc, wg_ref, ba_ref, bx_ref, lam_ref)

    hcur = hst[...]
    for t in range(steps):
        hcur = a[t * nb:(t + 1) * nb, :] * hcur + b[t * nb:(t + 1) * nb, :]
        hsbuf[t * nb:(t + 1) * nb, :] = hcur
    hst[...] = hcur

    y_lru = hsbuf[...] * jax.nn.gelu(u_gate)
    o_ref[...] = _mix_out(x, g_ref[...], y_pool, y_lru, wout_ref, lng_ref, lnb_ref)

    @pl.when(pid == pl.num_programs(0) - 1)
    def _final_state_and_sample_rows():
        npool_ref[...] = zbuf[tm:tm + pool_hist, :].reshape(POOL_BUF, nb, POOL_WIDTH)
        nconv_ref[...] = cbuf[tm:tm + conv_hist, :].reshape(CONV_WIDTH - 1, nb, LRU_WIDTH)
        nh_ref[...] = hcur
        _mixer_sample_rows(*sample_in, weights, *sample_out)

    zbuf[0:pool_hist, :] = zbuf[tm:tm + pool_hist, :]
    cbuf[0:conv_hist, :] = cbuf[tm:tm + conv_hist, :]


def _mixer_call(x2d, xs, mod, wts, sample_states, new_p, new_s, *, layer, steps, n_prompt):
    assert len(wts) == N_MIXER_WEIGHTS
    rows = x2d.shape[0]
    n_sample = xs.shape[0]
    tm = steps * n_prompt
    stacks = tuple(new_p) + tuple(new_s)
    n_in = 4 + len(wts) + 4 + len(sample_states)
    any_spec = pl.BlockSpec(memory_space=pl.ANY)
    stack_specs = [_layer_block(a.shape, layer) for a in stacks]
    rows_spec = pl.BlockSpec((tm, D_MODEL), lambda i: (i, 0))
    return pl.pallas_call(
        functools.partial(_mixer_kernel, steps=steps, nb=n_prompt),
        out_shape=[jax.ShapeDtypeStruct((rows, D_MODEL), F32)]
        + [jax.ShapeDtypeStruct(a.shape, F32) for a in new_p]
        + [jax.ShapeDtypeStruct(xs.shape, F32)]
        + [jax.ShapeDtypeStruct(a.shape, F32) for a in new_s],
        grid=(rows // tm,),
        in_specs=[rows_spec]
        + _mod_specs(1, False, n_sample, n_prompt)
        + [_resident(w, layer) for w in wts]
        + [_whole(xs.shape)] + _mod_specs(1, True, n_sample, n_prompt)
        + [_layer_block(a.shape, layer) for a in sample_states]
        + [any_spec] * len(stacks),
        out_specs=[rows_spec] + stack_specs[:3] + [_whole(xs.shape)] + stack_specs[3:],
        scratch_shapes=[
            pltpu.VMEM((POOL_BUF * n_prompt + tm, POOL_WIDTH), F32),
            pltpu.VMEM(((CONV_WIDTH - 1) * n_prompt + tm, LRU_WIDTH), F32),
            pltpu.VMEM((n_prompt, LRU_WIDTH), F32),
            pltpu.VMEM((tm, LRU_WIDTH), F32),
        ],
        input_output_aliases={n_in: 1, n_in + 1: 2, n_in + 2: 3, n_in + 3: 5, n_in + 4: 6, n_in + 5: 7},
        compiler_params=pltpu.CompilerParams(
            dimension_semantics=("arbitrary",),
            vmem_limit_bytes=MIXER_VMEM),
        name="mixer",
    )(x2d, mod, mod, mod, *wts, xs, mod, mod, mod, *sample_states, *stacks)


def _lane_blocks(w):
    depth, heads, c, _ = w.shape
    per = LANES // c
    eye = jnp.eye(per, dtype=w.dtype)
    blocks = jnp.einsum('lbgij,gh->lbgihj', w.reshape(depth, heads // per, per, c, c), eye)
    return blocks.reshape(depth, heads // per, LANES, LANES)


def kernel(x_prompt, x_sample, state_pool, state_conv, state_rglru, c_prompt, c_sample, w_ada, b_ada, ffn1_wgu, ffn1_wdown, w_in, w_out, w_pool, pool_scale, conv_w, conv_b, w_rg_a, b_rg_a, w_rg_x, b_rg_x, lru_lambda, ffn2_wgu, ffn2_wdown, ln_g, ln_b):
    n_prompt, seq, _ = x_prompt.shape
    n_sample = x_sample.shape[0]
    assert x_sample.shape[1] == 1 and n_sample % n_prompt == 0 and n_prompt == SUBLANES
    assert seq % MIXER_STEPS == 0 and (seq * n_prompt) % FFN_TILE == 0 and FFN_TILE % FFN_TILE_SMALL == 0

    row = lambda v: v.reshape(DEPTH, 1, v.shape[-1])
    wpool_bd = _lane_blocks(w_pool).astype(BF16)
    wgate_bd = jnp.concatenate([_lane_blocks(w_rg_a), _lane_blocks(w_rg_x)], axis=1).astype(BF16)

    c_all = jnp.concatenate([c_sample, c_prompt], axis=0)
    b_ada3 = b_ada.reshape(DEPTH, 1, b_ada.shape[-1])
    mod, wgu_bf, wd_bf = _ada_call(c_all, w_ada, b_ada3, 0, round_next=((ffn1_wgu, 0, True), (ffn1_wdown, 0, False)))

    xp = x_prompt
    xs = x_sample.reshape(n_sample, D_MODEL)
    hist_major = lambda a: jnp.swapaxes(a, 1, 2)
    sample_states = (hist_major(state_pool), hist_major(state_conv), state_rglru)
    new_p = (jnp.zeros((DEPTH, POOL_BUF, n_prompt, POOL_WIDTH), F32),
             jnp.zeros((DEPTH, CONV_WIDTH - 1, n_prompt, LRU_WIDTH), F32),
             jnp.zeros((DEPTH, n_prompt, LRU_WIDTH), F32))
    new_s = tuple(jnp.zeros(a.shape, F32) for a in sample_states)

    for l in range(DEPTH):
        last = l == DEPTH - 1
        ffn = functools.partial(_ffn_call, layer=l, n_prompt=n_prompt)
        xp, xs, win_bf, wout_bf, wgu2_bf, wd2_bf = ffn(
            xp, xs, mod, wgu_bf, wd_bf, ln_g, ln_b, sub=0, ln_idx=0, batch_major_in=(l == 0),
            tm=FFN_TILE,
            round_next=((w_in, l, False), (w_out, l, False), (ffn2_wgu, l, True), (ffn2_wdown, l, False)))

        mixer_w = (win_bf, wpool_bd, row(pool_scale), conv_w, row(conv_b), wgate_bd,
                   row(b_rg_a), row(b_rg_x), row(lru_lambda), wout_bf, ln_g, ln_b)
        outs = _mixer_call(xp, xs, mod, mixer_w, sample_states, new_p, new_s, layer=l, steps=MIXER_STEPS,
                           n_prompt=n_prompt)
        xp, new_p, xs, new_s = outs[0], outs[1:4], outs[4], outs[5:8]

        outs = ffn(xp, xs, mod, wgu2_bf, wd2_bf, ln_g, ln_b, sub=2, ln_idx=2, batch_major_out=last,
                   tm=FFN_TILE if last else FFN_TILE_SMALL,
                   round_next=() if last else ((ffn1_wgu, l + 1, True), (ffn1_wdown, l + 1, False)),
                   next_ada=None if last else (c_all, w_ada, b_ada3, l + 1))
        xp, xs = outs[:2]
        if not last:
            mod, wgu_bf, wd_bf = outs[2:]

    return (xp, xs.reshape(n_sample, 1, D_MODEL),
            hist_major(new_p[0]), hist_major(new_p[1]), new_p[2],
            hist_major(new_s[0]), hist_major(new_s[1]), new_s[2])
```

```python
import functools

import jax
import jax.numpy as jnp
from jax import lax
from jax.experimental import pallas as pl
from jax.experimental.pallas import tpu as pltpu

D_MODEL = 1024
DEPTH = 4
PAST_LEN = 16384
POOL_WIDTH = D_MODEL // 2
LRU_WIDTH = D_MODEL - POOL_WIDTH
IN_WIDTH = POOL_WIDTH + 2 * LRU_WIDTH
POOL_WINDOWS = (2, 4, 8, 16)
POOL_GROUP_DIM = POOL_WIDTH // len(POOL_WINDOWS)
POOL_BUF = max(POOL_WINDOWS) - 1
N_LRU_HEADS = 8
LRU_HEAD_DIM = LRU_WIDTH // N_LRU_HEADS
CONV_WIDTH = 4
LRU_C = 8.0
D_FF = 128 * int(round(8 * D_MODEL / 3 / 128))
N_MOD = 9
ALPHA = (2 * DEPTH) ** 0.25
LN_EPS = 1e-5

F32 = jnp.float32
BF16 = jnp.bfloat16

SUBLANES = 8
LANES = 128
BF16_SUBLANES = 16
MIB = 1024 * 1024
GU_BLOCKS = D_FF // LANES

FFN_TILE = 1024
FFN_TILE_SMALL = 512
FFN_SUBTILE = 256
MIXER_STEPS = 128
ADA_VMEM = 32 * MIB
FFN_VMEM = 56 * MIB
MIXER_VMEM = 56 * MIB


def _layer_norm(y, g, b):
    mu = jnp.mean(y, axis=-1, keepdims=True)
    d = y - mu
    var = jnp.mean(d * d, axis=-1, keepdims=True)
    return d * lax.rsqrt(var + LN_EPS) * g + b


def _sigmoid(x):
    return 0.5 * jnp.tanh(0.5 * x) + 0.5


def _silu(x):
    return x * _sigmoid(x)


def _softplus(x):
    return jnp.maximum(x, 0.0) + jnp.log1p(jnp.exp(-jnp.abs(x)))


def _per_batch(fn, x, *ms):
    rows, d = x.shape
    nb = ms[0].shape[0]
    if nb == rows:
        return fn(x, *ms)
    return fn(x.reshape(rows // nb, nb, d), *[m[None] for m in ms]).reshape(rows, d)


def _modulate(x, sh, sc):
    return _per_batch(lambda x_, sh_, sc_: x_ * (1.0 + sc_) + sh_, x, sh, sc)


def _gated_residual(x, g, y, gate_scale):
    return _per_batch(lambda y_, g_: (gate_scale * (1.0 + g_)) * y_, y, g) + ALPHA * x


def _mod_specs(sub, is_sample, n_sample, n_prompt):
    rows, block = (n_sample, 0) if is_sample else (n_prompt, n_sample // n_prompt)
    return [pl.BlockSpec((rows, D_MODEL), functools.partial(lambda i, k: (block, k), k=3 * sub + t))
            for t in range(3)]


def _resident(w, layer):
    if w.ndim == 2:
        return pl.BlockSpec(w.shape, lambda i: (0, 0), pipeline_mode=pl.Buffered(1))
    nd = w.ndim
    return pl.BlockSpec((None,) + tuple(w.shape[1:]), lambda i: (layer,) + (0,) * (nd - 1),
                        pipeline_mode=pl.Buffered(1))


def _layer_block(shape, layer):
    nd = len(shape)
    return pl.BlockSpec((None,) + tuple(shape[1:]), lambda i: (layer,) + (0,) * (nd - 1))


def _whole(shape):
    return pl.BlockSpec(tuple(shape), lambda i: (0,) * len(shape))


def _round_specs(w, layer, n_steps, step=lambda i: i):
    rows, cols = w.shape[1:]
    n_chunks = next(n for n in range(n_steps, 0, -1) if rows % (n * BF16_SUBLANES) == 0)
    chunk = rows // n_chunks
    idx = lambda *ids: jnp.minimum(step(*ids), n_chunks - 1)
    return (pl.BlockSpec((None, chunk, cols), lambda *ids: (layer, idx(*ids), 0)),
            pl.BlockSpec((chunk, cols), lambda *ids: (idx(*ids), 0)),
            jax.ShapeDtypeStruct((rows, cols), BF16))


def _round_chunks(src_refs, dst_refs, interleave):
    for src, dst, il in zip(src_refs, dst_refs, interleave):
        if il:
            for i in range(GU_BLOCKS):
                dst[:, 2 * i * LANES:(2 * i + 1) * LANES] = src[:, i * LANES:(i + 1) * LANES].astype(BF16)
                dst[:, (2 * i + 1) * LANES:(2 * i + 2) * LANES] = src[:, D_FF + i * LANES:D_FF + (i + 1) * LANES].astype(BF16)
        else:
            dst[...] = src[...].astype(BF16)


def _split(refs, *counts):
    out, pos = [], 0
    for c in counts:
        out.append(refs[pos:pos + c])
        pos += c
    out.append(refs[pos:])
    return out


def _ada_cols(n_out, n_steps):
    return next(tn for tn in range(LANES, n_out + 1, LANES) if n_out % tn == 0 and n_out // tn <= n_steps)


def _ada_tile(c_ref, w_ref, b_ref, o_ref):
    sc = _silu(c_ref[...]).astype(BF16)
    o_ref[...] = jnp.dot(sc, w_ref[...].astype(BF16), preferred_element_type=F32) + b_ref[...]


def _ada_specs(c_all, w_ada, layer, tn, n_steps):
    n_rows, n_out = c_all.shape[0], w_ada.shape[-1]
    n_tiles = n_out // tn
    assert n_tiles <= n_steps and n_out % tn == 0
    idx = lambda i: jnp.minimum(i, n_tiles - 1)
    return ([_whole(c_all.shape),
             pl.BlockSpec((None, D_MODEL, tn), lambda i: (layer, 0, idx(i))),
             pl.BlockSpec((None, 1, tn), lambda i: (layer, 0, idx(i)))],
            pl.BlockSpec((n_rows, tn), lambda i: (0, idx(i))),
            jax.ShapeDtypeStruct((n_rows, n_out), F32))


def _ada_kernel(*refs, interleave):
    n_round = len(interleave)
    ada_in, round_in, (o_ref,), round_out = _split(refs, 3, n_round, 1)
    _round_chunks(round_in, round_out, interleave)
    _ada_tile(*ada_in, o_ref)


def _ada_call(c_all, w_ada, b_ada, layer, round_next=()):
    n_col = 8
    ada_in, ada_out, ada_shape = _ada_specs(c_all, w_ada, layer, w_ada.shape[-1] // n_col, n_col)
    rnd = [_round_specs(w, l, n_col) for w, l, _ in round_next]
    return pl.pallas_call(
        functools.partial(_ada_kernel, interleave=tuple(il for _, _, il in round_next)),
        out_shape=[ada_shape] + [r[2] for r in rnd],
        grid=(n_col,),
        in_specs=ada_in + [r[0] for r in rnd],
        out_specs=[ada_out] + [r[1] for r in rnd],
        compiler_params=pltpu.CompilerParams(
            dimension_semantics=("arbitrary",),
            vmem_limit_bytes=ADA_VMEM),
        name="ada",
    )(c_all, w_ada, b_ada, *[w for w, _, _ in round_next])


def _swiglu_act(gu):
    blocks = [_silu(gu[:, 2 * i * LANES:(2 * i + 1) * LANES]) * gu[:, (2 * i + 1) * LANES:(2 * i + 2) * LANES]
              for i in range(GU_BLOCKS)]
    return jnp.concatenate(blocks, axis=-1)


def _ffn_rows(x, sh, sc, g, wgu_ref, wd_ref, lng, lnb):
    h = _modulate(x, sh, sc).astype(BF16)
    gu = jnp.dot(h, wgu_ref[...], preferred_element_type=F32)
    act = _swiglu_act(gu).astype(BF16)
    acc = jnp.dot(act, wd_ref[...], preferred_element_type=F32)
    return _layer_norm(_gated_residual(x, g, acc, 0.5), lng, lnb)


def _ffn_kernel(*refs, ln_idx, subtiles, batch_major_in, batch_major_out, interleave, next_mod):
    n_round, n_ada = len(interleave), 3 if next_mod else 0
    (x_ref, sh_ref, sc_ref, g_ref, wgu_ref, wd_ref, lng_ref, lnb_ref), sample_in, ada_in, round_in, \
        (o_ref, os_ref), ada_out, round_out, lbuf = _split(refs, 8, 4, n_ada, n_round, 2, n_ada // 3, n_round)
    _round_chunks(round_in, round_out, interleave)
    if next_mod:
        _ada_tile(*ada_in, *ada_out)
    lng, lnb = lng_ref[ln_idx:ln_idx + 1, :], lnb_ref[ln_idx:ln_idx + 1, :]
    sh, sc, g = sh_ref[...], sc_ref[...], g_ref[...]
    nb = sh.shape[0]
    rows = (x_ref.shape[0] * x_ref.shape[1] if batch_major_in else x_ref.shape[0]) // subtiles
    ts = rows // nb
    for s_i in range(subtiles):
        steps = slice(s_i * ts, (s_i + 1) * ts)
        if batch_major_in:
            for b in range(nb):
                lbuf[0][steps, b, :] = x_ref[b, steps, :]
            x = lbuf[0][steps, :, :].reshape(rows, D_MODEL)
        else:
            x = x_ref[s_i * rows:(s_i + 1) * rows, :]
        y = _ffn_rows(x, sh, sc, g, wgu_ref, wd_ref, lng, lnb)
        if batch_major_out:
            lbuf[0][steps, :, :] = y.reshape(ts, nb, D_MODEL)
            for b in range(nb):
                o_ref[b, steps, :] = lbuf[0][steps, b, :]
        else:
            o_ref[s_i * rows:(s_i + 1) * rows, :] = y

    @pl.when(pl.program_id(0) == pl.num_programs(0) - 1)
    def _sample_rows():
        xs_ref, shs_ref, scs_ref, gs_ref = sample_in
        os_ref[...] = _ffn_rows(xs_ref[...], shs_ref[...], scs_ref[...], gs_ref[...], wgu_ref, wd_ref, lng, lnb)


def _ffn_call(x, xs, mod, wgu, wd, ln_g, ln_b, *, layer, sub, ln_idx, tm, n_prompt,
              batch_major_in=False, batch_major_out=False, round_next=(), next_ada=None):
    rows = x.shape[0] * x.shape[1] if batch_major_in else x.shape[0]
    n_sample = xs.shape[0]
    steps = tm // n_prompt
    n_grid = rows // tm
    major_spec = pl.BlockSpec((n_prompt, steps, D_MODEL), lambda i: (0, i, 0))
    rows_spec = pl.BlockSpec((tm, D_MODEL), lambda i: (i, 0))
    out_shape = (n_prompt, rows // n_prompt, D_MODEL) if batch_major_out else (rows, D_MODEL)
    rnd = [_round_specs(w, l, n_grid) for w, l, _ in round_next]
    ada_in, ada_out, ada_shape, ada_args = [], [], [], []
    if next_ada is not None:
        c_all, w_ada, b_ada, ada_layer = next_ada
        ada_in, out_spec, shape = _ada_specs(c_all, w_ada, ada_layer, _ada_cols(w_ada.shape[-1], n_grid), n_grid)
        ada_out, ada_shape, ada_args = [out_spec], [shape], [c_all, w_ada, b_ada]
    kern = functools.partial(_ffn_kernel, ln_idx=ln_idx, subtiles=tm // FFN_SUBTILE,
                             batch_major_in=batch_major_in, batch_major_out=batch_major_out,
                             interleave=tuple(il for _, _, il in round_next), next_mod=next_ada is not None)
    return pl.pallas_call(
        kern,
        out_shape=[jax.ShapeDtypeStruct(out_shape, F32), jax.ShapeDtypeStruct(xs.shape, F32)] + ada_shape
        + [r[2] for r in rnd],
        grid=(n_grid,),
        in_specs=[major_spec if batch_major_in else rows_spec]
        + _mod_specs(sub, False, n_sample, n_prompt)
        + [_resident(wgu, layer), _resident(wd, layer), _resident(ln_g, layer), _resident(ln_b, layer)]
        + [_whole(xs.shape)] + _mod_specs(sub, True, n_sample, n_prompt)
        + ada_in + [r[0] for r in rnd],
        out_specs=[major_spec if batch_major_out else rows_spec, _whole(xs.shape)] + ada_out + [r[1] for r in rnd],
        scratch_shapes=[pltpu.VMEM((steps, n_prompt, D_MODEL), F32)] if batch_major_in or batch_major_out else [],
        compiler_params=pltpu.CompilerParams(
            dimension_semantics=("arbitrary",),
            vmem_limit_bytes=FFN_VMEM),
        name=f"ffn{sub}",
    )(x, mod, mod, mod, wgu, wd, ln_g, ln_b, xs, mod, mod, mod, *ada_args, *[w for w, _, _ in round_next])


def _grouped_dot(x, w_ref, first, n):
    c = w_ref.shape[-1]
    return jnp.concatenate([jnp.dot(x[:, i * c:(i + 1) * c], w_ref[first + i], preferred_element_type=F32)
                            for i in range(n)], axis=-1)


def _gates(xc, wg_ref, ba_ref, bx_ref, lam_ref):
    xb = xc.astype(BF16)
    n = wg_ref.shape[0] // 2
    r = _sigmoid(_grouped_dot(xb, wg_ref, 0, n) + ba_ref[...])
    i = _sigmoid(_grouped_dot(xb, wg_ref, n, n) + bx_ref[...])
    log_a = (-LRU_C * r) * _softplus(-lam_ref[...])
    a = jnp.exp(log_a)
    t = jnp.tanh(log_a)
    q = (-2.0 * t) / (1.0 - t)
    mult = jnp.where(q == 0.0, 0.0, q * lax.rsqrt(q))
    return a, mult * (i * xc)


def _mix_out(x, g, y_pool, y_lru, wout_ref, lng_ref, lnb_ref):
    mix = jnp.dot(y_pool.astype(BF16), wout_ref[:POOL_WIDTH, :], preferred_element_type=F32)
    mix = mix + jnp.dot(y_lru.astype(BF16), wout_ref[POOL_WIDTH:, :], preferred_element_type=F32)
    y = _gated_residual(x, g, mix, 1.0)
    return _layer_norm(y, lng_ref[1:2, :], lnb_ref[1:2, :])


N_MIXER_WEIGHTS = 12
MIXER_ROW_VECTORS = (2, 4, 6, 7, 8)


def _mixer_sample_rows(x_ref, sh_ref, sc_ref, g_ref, pst_ref, cst_ref, h0_ref, weights,
                       o_ref, npool_ref, nconv_ref, nh_ref):
    (win_ref, wpool_ref, pscale_ref, cw_ref, cb_ref, wg_ref, ba_ref, bx_ref, lam_ref,
     wout_ref, lng_ref, lnb_ref) = weights
    x = x_ref[...]
    h = _modulate(x, sh_ref[...], sc_ref[...]).astype(BF16)
    proj = jnp.dot(h, win_ref[...], preferred_element_type=F32)
    u_pool = proj[:, :POOL_WIDTH]
    u_lru = proj[:, POOL_WIDTH:POOL_WIDTH + LRU_WIDTH]
    u_gate = proj[:, POOL_WIDTH + LRU_WIDTH:]

    pooled = []
    for gi, w in enumerate(POOL_WINDOWS):
        cols = slice(gi * POOL_GROUP_DIM, (gi + 1) * POOL_GROUP_DIM)
        s = u_pool[:, cols]
        for back in range(1, w):
            s = s + pst_ref[POOL_BUF - back, :, cols]
        cnt = float(min(PAST_LEN + 1, w))
        pooled.append(s / cnt - u_pool[:, cols])
    pooled = jnp.concatenate(pooled, axis=-1).astype(BF16)
    y_pool = _grouped_dot(pooled, wpool_ref, 0, len(POOL_WINDOWS)) * pscale_ref[...]
    npool_ref[0:POOL_BUF - 1] = pst_ref[1:POOL_BUF]
    npool_ref[POOL_BUF - 1] = u_pool

    xc = None
    for k in range(CONV_WIDTH - 1):
        term = cst_ref[k] * cw_ref[k:k + 1, :]
        xc = term if xc is None else xc + term
    xc = cb_ref[...] + (xc + u_lru * cw_ref[CONV_WIDTH - 1:CONV_WIDTH, :])
    nconv_ref[0:CONV_WIDTH - 2] = cst_ref[1:CONV_WIDTH - 1]
    nconv_ref[CONV_WIDTH - 2] = u_lru

    a, b = _gates(xc, wg_ref, ba_ref, bx_ref, lam_ref)
    hs = a * h0_ref[...] + b
    nh_ref[...] = hs
    y_lru = hs * jax.nn.gelu(u_gate)

    o_ref[...] = _mix_out(x, g_ref[...], y_pool, y_lru, wout_ref, lng_ref, lnb_ref)


def _mixer_kernel(*refs, steps, nb, layer):
    (x_ref, sh_ref, sc_ref, g_ref), weights, sample_in, _aliased, \
        (o_ref, npool_ref, nconv_ref, nh_ref), sample_out, (zbuf, cbuf, hst, hsbuf) = \
        _split(refs, 4, N_MIXER_WEIGHTS, 7, 6, 4, 4)
    weights = tuple(w.at[pl.ds(layer, 1)] if i in MIXER_ROW_VECTORS else w for i, w in enumerate(weights))
    (win_ref, wpool_ref, pscale_ref, cw_ref, cb_ref, wg_ref, ba_ref, bx_ref, lam_ref,
     wout_ref, lng_ref, lnb_ref) = weights

    pid = pl.program_id(0)
    tm = steps * nb
    pool_hist = POOL_BUF * nb
    conv_hist = (CONV_WIDTH - 1) * nb

    @pl.when(pid == 0)
    def _new_sequences():
        zbuf[0:pool_hist, :] = jnp.zeros((pool_hist, POOL_WIDTH), F32)
        cbuf[0:conv_hist, :] = jnp.zeros((conv_hist, LRU_WIDTH), F32)
        hst[...] = jnp.zeros_like(hst)

    x = x_ref[...]
    proj = jnp.dot(_modulate(x, sh_ref[...], sc_ref[...]).astype(BF16), win_ref[...], preferred_element_type=F32)
    u_pool = proj[:, :POOL_WIDTH]
    u_lru = proj[:, POOL_WIDTH:POOL_WIDTH + LRU_WIDTH]
    u_gate = proj[:, POOL_WIDTH + LRU_WIDTH:]

    zbuf[pool_hist:pool_hist + tm, :] = u_pool
    t_idx = lax.shift_right_logical(lax.broadcasted_iota(jnp.int32, (tm, POOL_GROUP_DIM), 0), nb.bit_length() - 1)
    pos1 = (pid * steps + 1) + t_idx
    cur = zbuf[...]
    first, w = -POOL_BUF, 1
    pooled = []
    for gi, wnd in enumerate(POOL_WINDOWS):
        assert wnd == 2 * w
        n = cur.shape[0]
        cur = cur[w * nb:, :] + cur[:n - w * nb, :]
        first, w = first + w, wnd
        cols = slice(gi * POOL_GROUP_DIM, (gi + 1) * POOL_GROUP_DIM)
        s = cur[-first * nb:, :POOL_GROUP_DIM]
        pooled.append(s / jnp.minimum(pos1, wnd).astype(F32) - u_pool[:, cols])
        cur = cur[:, POOL_GROUP_DIM:]
    assert first == 0
    pooled = jnp.concatenate(pooled, axis=-1).astype(BF16)
    y_pool = _grouped_dot(pooled, wpool_ref, 0, len(POOL_WINDOWS)) * pscale_ref[...]

    cbuf[conv_hist:conv_hist + tm, :] = u_lru
    xc = None
    for k in range(CONV_WIDTH):
        term = cbuf[k * nb:k * nb + tm, :] * cw_ref[k:k + 1, :]
        xc = term if xc is None else xc + term
    a, b = _gates(cb_ref[...] + xc, wg_ref, ba_ref, bx_ref, lam_ref)

    hcur = hst[...]
    for t in range(steps):
        hcur = a[t * nb:(t + 1) * nb, :] * hcur + b[t * nb:(t + 1) * nb, :]
        hsbuf[t * nb:(t + 1) * nb, :] = hcur
    hst[...] = hcur

    y_lru = hsbuf[...] * jax.nn.gelu(u_gate)
    o_ref[...] = _mix_out(x, g_ref[...], y_pool, y_lru, wout_ref, lng_ref, lnb_ref)

    @pl.when(pid == pl.num_programs(0) - 1)
    def _final_state_and_sample_rows():
        npool_ref[...] = zbuf[tm:tm + pool_hist, :].reshape(POOL_BUF, nb, POOL_WIDTH)
        nconv_ref[...] = cbuf[tm:tm + conv_hist, :].reshape(CONV_WIDTH - 1, nb, LRU_WIDTH)
        nh_ref[...] = hcur
        _mixer_sample_rows(*sample_in, weights, *sample_out)

    zbuf[0:pool_hist, :] = zbuf[tm:tm + pool_hist, :]
    cbuf[0:conv_hist, :] = cbuf[tm:tm + conv_hist, :]


def _mixer_call(x2d, xs, mod, wts, sample_states, new_p, new_s, *, layer, steps, n_prompt):
    assert len(wts) == N_MIXER_WEIGHTS
    rows = x2d.shape[0]
    n_sample = xs.shape[0]
    tm = steps * n_prompt
    stacks = tuple(new_p) + tuple(new_s)
    n_in = 4 + len(wts) + 4 + len(sample_states)
    any_spec = pl.BlockSpec(memory_space=pl.ANY)
    stack_specs = [_layer_block(a.shape, layer) for a in stacks]
    rows_spec = pl.BlockSpec((tm, D_MODEL), lambda i: (i, 0))
    return pl.pallas_call(
        functools.partial(_mixer_kernel, steps=steps, nb=n_prompt, layer=layer),
        out_shape=[jax.ShapeDtypeStruct((rows, D_MODEL), F32)]
        + [jax.ShapeDtypeStruct(a.shape, F32) for a in new_p]
        + [jax.ShapeDtypeStruct(xs.shape, F32)]
        + [jax.ShapeDtypeStruct(a.shape, F32) for a in new_s],
        grid=(rows // tm,),
        in_specs=[rows_spec]
        + _mod_specs(1, False, n_sample, n_prompt)
        + [_resident(w, layer) for w in wts]
        + [_whole(xs.shape)] + _mod_specs(1, True, n_sample, n_prompt)
        + [_layer_block(a.shape, layer) for a in sample_states]
        + [any_spec] * len(stacks),
        out_specs=[rows_spec] + stack_specs[:3] + [_whole(xs.shape)] + stack_specs[3:],
        scratch_shapes=[
            pltpu.VMEM((POOL_BUF * n_prompt + tm, POOL_WIDTH), F32),
            pltpu.VMEM(((CONV_WIDTH - 1) * n_prompt + tm, LRU_WIDTH), F32),
            pltpu.VMEM((n_prompt, LRU_WIDTH), F32),
            pltpu.VMEM((tm, LRU_WIDTH), F32),
        ],
        input_output_aliases={n_in: 1, n_in + 1: 2, n_in + 2: 3, n_in + 3: 5, n_in + 4: 6, n_in + 5: 7},
        compiler_params=pltpu.CompilerParams(
            dimension_semantics=("arbitrary",),
            vmem_limit_bytes=MIXER_VMEM),
        name="mixer",
    )(x2d, mod, mod, mod, *wts, xs, mod, mod, mod, *sample_states, *stacks)


def _lane_blocks(w):
    depth, heads, c, _ = w.shape
    per = LANES // c
    eye = jnp.eye(per, dtype=w.dtype)
    blocks = jnp.einsum('lbgij,gh->lbgihj', w.reshape(depth, heads // per, per, c, c), eye)
    return blocks.reshape(depth, heads // per, LANES, LANES)


def kernel(x_prompt, x_sample, state_pool, state_conv, state_rglru, c_prompt, c_sample, w_ada, b_ada, ffn1_wgu, ffn1_wdown, w_in, w_out, w_pool, pool_scale, conv_w, conv_b, w_rg_a, b_rg_a, w_rg_x, b_rg_x, lru_lambda, ffn2_wgu, ffn2_wdown, ln_g, ln_b):
    n_prompt, seq, _ = x_prompt.shape
    n_sample = x_sample.shape[0]
    assert x_sample.shape[1] == 1 and n_sample % n_prompt == 0 and n_prompt == SUBLANES
    assert seq % MIXER_STEPS == 0 and (seq * n_prompt) % FFN_TILE == 0 and FFN_TILE % FFN_TILE_SMALL == 0

    wpool_bd = _lane_blocks(w_pool).astype(BF16)
    wgate_bd = jnp.concatenate([_lane_blocks(w_rg_a), _lane_blocks(w_rg_x)], axis=1).astype(BF16)

    c_all = jnp.concatenate([c_sample, c_prompt], axis=0)
    b_ada3 = b_ada.reshape(DEPTH, 1, b_ada.shape[-1])
    mod, wgu_bf, wd_bf = _ada_call(c_all, w_ada, b_ada3, 0, round_next=((ffn1_wgu, 0, True), (ffn1_wdown, 0, False)))

    xp = x_prompt
    xs = x_sample.reshape(n_sample, D_MODEL)
    hist_major = lambda a: jnp.swapaxes(a, 1, 2)
    sample_states = (hist_major(state_pool), hist_major(state_conv), state_rglru)
    new_p = (jnp.zeros((DEPTH, POOL_BUF, n_prompt, POOL_WIDTH), F32),
             jnp.zeros((DEPTH, CONV_WIDTH - 1, n_prompt, LRU_WIDTH), F32),
             jnp.zeros((DEPTH, n_prompt, LRU_WIDTH), F32))
    new_s = tuple(jnp.zeros(a.shape, F32) for a in sample_states)

    for l in range(DEPTH):
        last = l == DEPTH - 1
        ffn = functools.partial(_ffn_call, layer=l, n_prompt=n_prompt)
        xp, xs, win_bf, wout_bf, wgu2_bf, wd2_bf = ffn(
            xp, xs, mod, wgu_bf, wd_bf, ln_g, ln_b, sub=0, ln_idx=0, batch_major_in=(l == 0),
            tm=FFN_TILE_SMALL if l == 0 else FFN_TILE,
            round_next=((w_in, l, False), (w_out, l, False), (ffn2_wgu, l, True), (ffn2_wdown, l, False)))

        mixer_w = (win_bf, wpool_bd, pool_scale, conv_w, conv_b, wgate_bd, b_rg_a, b_rg_x, lru_lambda,
                   wout_bf, ln_g, ln_b)
        outs = _mixer_call(xp, xs, mod, mixer_w, sample_states, new_p, new_s, layer=l, steps=MIXER_STEPS,
                           n_prompt=n_prompt)
        xp, new_p, xs, new_s = outs[0], outs[1:4], outs[4], outs[5:8]

        outs = ffn(xp, xs, mod, wgu2_bf, wd2_bf, ln_g, ln_b, sub=2, ln_idx=2, batch_major_out=last,
                   tm=FFN_TILE_SMALL,
                   round_next=() if last else ((ffn1_wgu, l + 1, True), (ffn1_wdown, l + 1, False)),
                   next_ada=None if last else (c_all, w_ada, b_ada3, l + 1))
        xp, xs = outs[:2]
        if not last:
            mod, wgu_bf, wd_bf = outs[2:]

    return (xp, xs.reshape(n_sample, 1, D_MODEL),
            hist_major(new_p[0]), hist_major(new_p[1]), new_p[2],
            hist_major(new_s[0]), hist_major(new_s[1]), new_s[2])
```

```python
import functools

import jax
import jax.numpy as jnp
from jax import lax
from jax.experimental import pallas as pl
from jax.experimental.pallas import tpu as pltpu

D_MODEL = 1024
DEPTH = 4
PAST_LEN = 16384
POOL_WIDTH = D_MODEL // 2
LRU_WIDTH = D_MODEL - POOL_WIDTH
IN_WIDTH = POOL_WIDTH + 2 * LRU_WIDTH
POOL_WINDOWS = (2, 4, 8, 16)
POOL_GROUP_DIM = POOL_WIDTH // len(POOL_WINDOWS)
POOL_BUF = max(POOL_WINDOWS) - 1
N_LRU_HEADS = 8
LRU_HEAD_DIM = LRU_WIDTH // N_LRU_HEADS
CONV_WIDTH = 4
LRU_C = 8.0
D_FF = 128 * int(round(8 * D_MODEL / 3 / 128))
N_MOD = 9
ALPHA = (2 * DEPTH) ** 0.25
LN_EPS = 1e-5

F32 = jnp.float32
BF16 = jnp.bfloat16

SUBLANES = 8
LANES = 128
BF16_SUBLANES = 16
MIB = 1024 * 1024
GU_BLOCKS = D_FF // LANES

FFN_TILE = 1024
FFN_TILE_SMALL = 512
FFN_SUBTILE = 256
MIXER_STEPS = 128
ADA_VMEM = 32 * MIB
FFN_VMEM = 56 * MIB
MIXER_VMEM = 56 * MIB


def _layer_norm(y, g, b):
    mu = jnp.mean(y, axis=-1, keepdims=True)
    d = y - mu
    var = jnp.mean(d * d, axis=-1, keepdims=True)
    return d * lax.rsqrt(var + LN_EPS) * g + b


def _sigmoid(x):
    return 0.5 * jnp.tanh(0.5 * x) + 0.5


def _silu(x):
    return x * _sigmoid(x)


def _softplus(x):
    return jnp.maximum(x, 0.0) + jnp.log1p(jnp.exp(-jnp.abs(x)))


def _per_batch(fn, x, *ms):
    rows, d = x.shape
    nb = ms[0].shape[0]
    if nb == rows:
        return fn(x, *ms)
    return fn(x.reshape(rows // nb, nb, d), *[m[None] for m in ms]).reshape(rows, d)


def _modulate(x, sh, sc):
    return _per_batch(lambda x_, sh_, sc_: x_ * (1.0 + sc_) + sh_, x, sh, sc)


def _gated_residual(x, g, y, gate_scale):
    return _per_batch(lambda y_, g_: (gate_scale * (1.0 + g_)) * y_, y, g) + ALPHA * x


def _mod_specs(sub, is_sample, n_sample, n_prompt):
    rows, block = (n_sample, 0) if is_sample else (n_prompt, n_sample // n_prompt)
    return [pl.BlockSpec((rows, D_MODEL), functools.partial(lambda i, k: (block, k), k=3 * sub + t))
            for t in range(3)]


def _resident(w, layer):
    if w.ndim == 2:
        return pl.BlockSpec(w.shape, lambda i: (0, 0), pipeline_mode=pl.Buffered(1))
    nd = w.ndim
    return pl.BlockSpec((None,) + tuple(w.shape[1:]), lambda i: (layer,) + (0,) * (nd - 1),
                        pipeline_mode=pl.Buffered(1))


def _layer_block(shape, layer):
    nd = len(shape)
    return pl.BlockSpec((None,) + tuple(shape[1:]), lambda i: (layer,) + (0,) * (nd - 1))


def _whole(shape):
    return pl.BlockSpec(tuple(shape), lambda i: (0,) * len(shape))


def _round_specs(w, layer, n_steps, step=lambda i: i):
    rows, cols = w.shape[1:]
    n_chunks = next(n for n in range(n_steps, 0, -1) if rows % (n * BF16_SUBLANES) == 0)
    chunk = rows // n_chunks
    idx = lambda *ids: jnp.minimum(step(*ids), n_chunks - 1)
    return (pl.BlockSpec((None, chunk, cols), lambda *ids: (layer, idx(*ids), 0)),
            pl.BlockSpec((chunk, cols), lambda *ids: (idx(*ids), 0)),
            jax.ShapeDtypeStruct((rows, cols), BF16))


def _round_chunks(src_refs, dst_refs, interleave):
    for src, dst, il in zip(src_refs, dst_refs, interleave):
        if il:
            for i in range(GU_BLOCKS):
                dst[:, 2 * i * LANES:(2 * i + 1) * LANES] = src[:, i * LANES:(i + 1) * LANES].astype(BF16)
                dst[:, (2 * i + 1) * LANES:(2 * i + 2) * LANES] = src[:, D_FF + i * LANES:D_FF + (i + 1) * LANES].astype(BF16)
        else:
            dst[...] = src[...].astype(BF16)


def _split(refs, *counts):
    out, pos = [], 0
    for c in counts:
        out.append(refs[pos:pos + c])
        pos += c
    out.append(refs[pos:])
    return out


def _ada_cols(n_out, n_steps):
    return next(tn for tn in range(LANES, n_out + 1, LANES) if n_out % tn == 0 and n_out // tn <= n_steps)


def _ada_tile(c_ref, w_ref, b_ref, o_ref):
    sc = _silu(c_ref[...]).astype(BF16)
    o_ref[...] = jnp.dot(sc, w_ref[...].astype(BF16), preferred_element_type=F32) + b_ref[...]


def _ada_specs(c_all, w_ada, layer, tn, n_steps):
    n_rows, n_out = c_all.shape[0], w_ada.shape[-1]
    n_tiles = n_out // tn
    assert n_tiles <= n_steps and n_out % tn == 0
    idx = lambda i: jnp.minimum(i, n_tiles - 1)
    return ([_whole(c_all.shape),
             pl.BlockSpec((None, D_MODEL, tn), lambda i: (layer, 0, idx(i))),
             pl.BlockSpec((None, 1, tn), lambda i: (layer, 0, idx(i)))],
            pl.BlockSpec((n_rows, tn), lambda i: (0, idx(i))),
            jax.ShapeDtypeStruct((n_rows, n_out), F32))


def _ada_kernel(*refs, interleave):
    n_round = len(interleave)
    ada_in, round_in, (o_ref,), round_out = _split(refs, 3, n_round, 1)
    _round_chunks(round_in, round_out, interleave)
    _ada_tile(*ada_in, o_ref)


def _ada_call(c_all, w_ada, b_ada, layer, round_next=()):
    n_col = 8
    ada_in, ada_out, ada_shape = _ada_specs(c_all, w_ada, layer, w_ada.shape[-1] // n_col, n_col)
    rnd = [_round_specs(w, l, n_col) for w, l, _ in round_next]
    return pl.pallas_call(
        functools.partial(_ada_kernel, interleave=tuple(il for _, _, il in round_next)),
        out_shape=[ada_shape] + [r[2] for r in rnd],
        grid=(n_col,),
        in_specs=ada_in + [r[0] for r in rnd],
        out_specs=[ada_out] + [r[1] for r in rnd],
        compiler_params=pltpu.CompilerParams(
            dimension_semantics=("arbitrary",),
            vmem_limit_bytes=ADA_VMEM),
        name="ada",
    )(c_all, w_ada, b_ada, *[w for w, _, _ in round_next])


def _swiglu_act(gu):
    blocks = [_silu(gu[:, 2 * i * LANES:(2 * i + 1) * LANES]) * gu[:, (2 * i + 1) * LANES:(2 * i + 2) * LANES]
              for i in range(GU_BLOCKS)]
    return jnp.concatenate(blocks, axis=-1)


def _ffn_rows(x, sh, sc, g, wgu_ref, wd_ref, lng, lnb):
    h = _modulate(x, sh, sc).astype(BF16)
    gu = jnp.dot(h, wgu_ref[...], preferred_element_type=F32)
    act = _swiglu_act(gu).astype(BF16)
    acc = jnp.dot(act, wd_ref[...], preferred_element_type=F32)
    return _layer_norm(_gated_residual(x, g, acc, 0.5), lng, lnb)


def _ffn_kernel(*refs, ln_idx, subtiles, batch_major_in, batch_major_out, interleave, next_mod):
    n_round, n_ada = len(interleave), 3 if next_mod else 0
    (x_ref, sh_ref, sc_ref, g_ref, wgu_ref, wd_ref, lng_ref, lnb_ref), sample_in, ada_in, round_in, \
        (o_ref, os_ref), ada_out, round_out, lbuf = _split(refs, 8, 4, n_ada, n_round, 2, n_ada // 3, n_round)
    _round_chunks(round_in, round_out, interleave)
    if next_mod:
        _ada_tile(*ada_in, *ada_out)
    lng, lnb = lng_ref[ln_idx:ln_idx + 1, :], lnb_ref[ln_idx:ln_idx + 1, :]
    sh, sc, g = sh_ref[...], sc_ref[...], g_ref[...]
    nb = sh.shape[0]
    rows = (x_ref.shape[0] * x_ref.shape[1] if batch_major_in else x_ref.shape[0]) // subtiles
    ts = rows // nb
    for s_i in range(subtiles):
        steps = slice(s_i * ts, (s_i + 1) * ts)
        if batch_major_in:
            for b in range(nb):
                lbuf[0][steps, b, :] = x_ref[b, steps, :]
            x = lbuf[0][steps, :, :].reshape(rows, D_MODEL)
        else:
            x = x_ref[s_i * rows:(s_i + 1) * rows, :]
        y = _ffn_rows(x, sh, sc, g, wgu_ref, wd_ref, lng, lnb)
        if batch_major_out:
            lbuf[0][steps, :, :] = y.reshape(ts, nb, D_MODEL)
            for b in range(nb):
                o_ref[b, steps, :] = lbuf[0][steps, b, :]
        else:
            o_ref[s_i * rows:(s_i + 1) * rows, :] = y

    @pl.when(pl.program_id(0) == pl.num_programs(0) - 1)
    def _sample_rows():
        xs_ref, shs_ref, scs_ref, gs_ref = sample_in
        os_ref[...] = _ffn_rows(xs_ref[...], shs_ref[...], scs_ref[...], gs_ref[...], wgu_ref, wd_ref, lng, lnb)


def _ffn_call(x, xs, mod, wgu, wd, ln_g, ln_b, *, layer, sub, ln_idx, tm, n_prompt,
              batch_major_in=False, batch_major_out=False, round_next=(), next_ada=None):
    rows = x.shape[0] * x.shape[1] if batch_major_in else x.shape[0]
    n_sample = xs.shape[0]
    steps = tm // n_prompt
    n_grid = rows // tm
    major_spec = pl.BlockSpec((n_prompt, steps, D_MODEL), lambda i: (0, i, 0))
    rows_spec = pl.BlockSpec((tm, D_MODEL), lambda i: (i, 0))
    out_shape = (n_prompt, rows // n_prompt, D_MODEL) if batch_major_out else (rows, D_MODEL)
    rnd = [_round_specs(w, l, n_grid) for w, l, _ in round_next]
    ada_in, ada_out, ada_shape, ada_args = [], [], [], []
    if next_ada is not None:
        c_all, w_ada, b_ada, ada_layer = next_ada
        ada_in, out_spec, shape = _ada_specs(c_all, w_ada, ada_layer, _ada_cols(w_ada.shape[-1], n_grid), n_grid)
        ada_out, ada_shape, ada_args = [out_spec], [shape], [c_all, w_ada, b_ada]
    kern = functools.partial(_ffn_kernel, ln_idx=ln_idx, subtiles=tm // FFN_SUBTILE,
                             batch_major_in=batch_major_in, batch_major_out=batch_major_out,
                             interleave=tuple(il for _, _, il in round_next), next_mod=next_ada is not None)
    return pl.pallas_call(
        kern,
        out_shape=[jax.ShapeDtypeStruct(out_shape, F32), jax.ShapeDtypeStruct(xs.shape, F32)] + ada_shape
        + [r[2] for r in rnd],
        grid=(n_grid,),
        in_specs=[major_spec if batch_major_in else rows_spec]
        + _mod_specs(sub, False, n_sample, n_prompt)
        + [_resident(wgu, layer), _resident(wd, layer), _resident(ln_g, layer), _resident(ln_b, layer)]
        + [_whole(xs.shape)] + _mod_specs(sub, True, n_sample, n_prompt)
        + ada_in + [r[0] for r in rnd],
        out_specs=[major_spec if batch_major_out else rows_spec, _whole(xs.shape)] + ada_out + [r[1] for r in rnd],
        scratch_shapes=[pltpu.VMEM((steps, n_prompt, D_MODEL), F32)] if batch_major_in or batch_major_out else [],
        compiler_params=pltpu.CompilerParams(
            dimension_semantics=("arbitrary",),
            vmem_limit_bytes=FFN_VMEM),
        name=f"ffn{sub}",
    )(x, mod, mod, mod, wgu, wd, ln_g, ln_b, xs, mod, mod, mod, *ada_args, *[w for w, _, _ in round_next])


def _grouped_dot(x, w_ref, first, n):
    c = w_ref.shape[-1]
    return jnp.concatenate([jnp.dot(x[:, i * c:(i + 1) * c], w_ref[first + i], preferred_element_type=F32)
                            for i in range(n)], axis=-1)


def _gates(xc, wg_ref, ba_ref, bx_ref, lam_ref):
    xb = xc.astype(BF16)
    n = wg_ref.shape[0] // 2
    r = _sigmoid(_grouped_dot(xb, wg_ref, 0, n) + ba_ref[...])
    i = _sigmoid(_grouped_dot(xb, wg_ref, n, n) + bx_ref[...])
    log_a = (-LRU_C * r) * _softplus(-lam_ref[...])
    a = jnp.exp(log_a)
    t = jnp.tanh(log_a)
    q = (-2.0 * t) / (1.0 - t)
    mult = jnp.where(q == 0.0, 0.0, q * lax.rsqrt(q))
    return a, mult * (i * xc)


def _mix_out(x, g, y_pool, y_lru, wout_ref, lng_ref, lnb_ref):
    mix = jnp.dot(y_pool.astype(BF16), wout_ref[:POOL_WIDTH, :], preferred_element_type=F32)
    mix = mix + jnp.dot(y_lru.astype(BF16), wout_ref[POOL_WIDTH:, :], preferred_element_type=F32)
    y = _gated_residual(x, g, mix, 1.0)
    return _layer_norm(y, lng_ref[1:2, :], lnb_ref[1:2, :])


N_MIXER_WEIGHTS = 12
MIXER_ROW_VECTORS = (2, 4, 6, 7, 8)


def _mixer_sample_rows(x_ref, sh_ref, sc_ref, g_ref, pst_ref, cst_ref, h0_ref, weights,
                       o_ref, npool_ref, nconv_ref, nh_ref):
    (win_ref, wpool_ref, pscale_ref, cw_ref, cb_ref, wg_ref, ba_ref, bx_ref, lam_ref,
     wout_ref, lng_ref, lnb_ref) = weights
    x = x_ref[...]
    h = _modulate(x, sh_ref[...], sc_ref[...]).astype(BF16)
    proj = jnp.dot(h, win_ref[...], preferred_element_type=F32)
    u_pool = proj[:, :POOL_WIDTH]
    u_lru = proj[:, POOL_WIDTH:POOL_WIDTH + LRU_WIDTH]
    u_gate = proj[:, POOL_WIDTH + LRU_WIDTH:]

    pooled = []
    for gi, w in enumerate(POOL_WINDOWS):
        cols = slice(gi * POOL_GROUP_DIM, (gi + 1) * POOL_GROUP_DIM)
        s = u_pool[:, cols]
        for back in range(1, w):
            s = s + pst_ref[POOL_BUF - back, :, cols]
        cnt = float(min(PAST_LEN + 1, w))
        pooled.append(s / cnt - u_pool[:, cols])
    pooled = jnp.concatenate(pooled, axis=-1).astype(BF16)
    y_pool = _grouped_dot(pooled, wpool_ref, 0, len(POOL_WINDOWS)) * pscale_ref[...]
    npool_ref[0:POOL_BUF - 1] = pst_ref[1:POOL_BUF]
    npool_ref[POOL_BUF - 1] = u_pool

    xc = None
    for k in range(CONV_WIDTH - 1):
        term = cst_ref[k] * cw_ref[k:k + 1, :]
        xc = term if xc is None else xc + term
    xc = cb_ref[...] + (xc + u_lru * cw_ref[CONV_WIDTH - 1:CONV_WIDTH, :])
    nconv_ref[0:CONV_WIDTH - 2] = cst_ref[1:CONV_WIDTH - 1]
    nconv_ref[CONV_WIDTH - 2] = u_lru

    a, b = _gates(xc, wg_ref, ba_ref, bx_ref, lam_ref)
    hs = a * h0_ref[...] + b
    nh_ref[...] = hs
    y_lru = hs * jax.nn.gelu(u_gate)

    o_ref[...] = _mix_out(x, g_ref[...], y_pool, y_lru, wout_ref, lng_ref, lnb_ref)


def _mixer_kernel(*refs, steps, nb, layer):
    (x_ref, sh_ref, sc_ref, g_ref), weights, sample_in, _aliased, \
        (o_ref, npool_ref, nconv_ref, nh_ref), sample_out, (zbuf, cbuf, hst, hsbuf) = \
        _split(refs, 4, N_MIXER_WEIGHTS, 7, 6, 4, 4)
    weights = tuple(w.at[pl.ds(layer, 1)] if i in MIXER_ROW_VECTORS else w for i, w in enumerate(weights))
    (win_ref, wpool_ref, pscale_ref, cw_ref, cb_ref, wg_ref, ba_ref, bx_ref, lam_ref,
     wout_ref, lng_ref, lnb_ref) = weights

    pid = pl.program_id(0)
    tm = steps * nb
    pool_hist = POOL_BUF * nb
    conv_hist = (CONV_WIDTH - 1) * nb

    @pl.when(pid == 0)
    def _new_sequences():
        zbuf[0:pool_hist, :] = jnp.zeros((pool_hist, POOL_WIDTH), F32)
        cbuf[0:conv_hist, :] = jnp.zeros((conv_hist, LRU_WIDTH), F32)
        hst[...] = jnp.zeros_like(hst)

    x = x_ref[...]
    proj = jnp.dot(_modulate(x, sh_ref[...], sc_ref[...]).astype(BF16), win_ref[...], preferred_element_type=F32)
    u_pool = proj[:, :POOL_WIDTH]
    u_lru = proj[:, POOL_WIDTH:POOL_WIDTH + LRU_WIDTH]
    u_gate = proj[:, POOL_WIDTH + LRU_WIDTH:]

    zbuf[pool_hist:pool_hist + tm, :] = u_pool
    t_idx = lax.shift_right_logical(lax.broadcasted_iota(jnp.int32, (tm, POOL_GROUP_DIM), 0), nb.bit_length() - 1)
    pos1 = (pid * steps + 1) + t_idx
    cur = zbuf[...]
    first, w = -POOL_BUF, 1
    pooled = []
    for gi, wnd in enumerate(POOL_WINDOWS):
        assert wnd == 2 * w
        n = cur.shape[0]
        cur = cur[w * nb:, :] + cur[:n - w * nb, :]
        first, w = first + w, wnd
        cols = slice(gi * POOL_GROUP_DIM, (gi + 1) * POOL_GROUP_DIM)
        s = cur[-first * nb:, :POOL_GROUP_DIM]
        pooled.append(s / jnp.minimum(pos1, wnd).astype(F32) - u_pool[:, cols])
        cur = cur[:, POOL_GROUP_DIM:]
    assert first == 0
    pooled = jnp.concatenate(pooled, axis=-1).astype(BF16)
    y_pool = _grouped_dot(pooled, wpool_ref, 0, len(POOL_WINDOWS)) * pscale_ref[...]

    cbuf[conv_hist:conv_hist + tm, :] = u_lru
    xc = None
    for k in range(CONV_WIDTH):
        term = cbuf[k * nb:k * nb + tm, :] * cw_ref[k:k + 1, :]
        xc = term if xc is None else xc + term
    a, b = _gates(cb_ref[...] + xc, wg_ref, ba_ref, bx_ref, lam_ref)

    hcur = hst[...]
    for t in range(steps):
        hcur = a[t * nb:(t + 1) * nb, :] * hcur + b[t * nb:(t + 1) * nb, :]
        hsbuf[t * nb:(t + 1) * nb, :] = hcur
    hst[...] = hcur

    y_lru = hsbuf[...] * jax.nn.gelu(u_gate)
    o_ref[...] = _mix_out(x, g_ref[...], y_pool, y_lru, wout_ref, lng_ref, lnb_ref)

    @pl.when(pid == pl.num_programs(0) - 1)
    def _final_state_and_sample_rows():
        npool_ref[...] = zbuf[tm:tm + pool_hist, :].reshape(POOL_BUF, nb, POOL_WIDTH)
        nconv_ref[...] = cbuf[tm:tm + conv_hist, :].reshape(CONV_WIDTH - 1, nb, LRU_WIDTH)
        nh_ref[...] = hcur
        _mixer_sample_rows(*sample_in, weights, *sample_out)

    zbuf[0:pool_hist, :] = zbuf[tm:tm + pool_hist, :]
    cbuf[0:conv_hist, :] = cbuf[tm:tm + conv_hist, :]


def _mixer_call(x2d, xs, mod, wts, sample_states, new_p, new_s, *, layer, steps, n_prompt):
    assert len(wts) == N_MIXER_WEIGHTS
    rows = x2d.shape[0]
    n_sample = xs.shape[0]
    tm = steps * n_prompt
    stacks = tuple(new_p) + tuple(new_s)
    n_in = 4 + len(wts) + 4 + len(sample_states)
    any_spec = pl.BlockSpec(memory_space=pl.ANY)
    stack_specs = [_layer_block(a.shape, layer) for a in stacks]
    rows_spec = pl.BlockSpec((tm, D_MODEL), lambda i: (i, 0))
    return pl.pallas_call(
        functools.partial(_mixer_kernel, steps=steps, nb=n_prompt, layer=layer),
        out_shape=[jax.ShapeDtypeStruct((rows, D_MODEL), F32)]
        + [jax.ShapeDtypeStruct(a.shape, F32) for a in new_p]
        + [jax.ShapeDtypeStruct(xs.shape, F32)]
        + [jax.ShapeDtypeStruct(a.shape, F32) for a in new_s],
        grid=(rows // tm,),
        in_specs=[rows_spec]
        + _mod_specs(1, False, n_sample, n_prompt)
        + [_resident(w, layer) for w in wts]
        + [_whole(xs.shape)] + _mod_specs(1, True, n_sample, n_prompt)
        + [_layer_block(a.shape, layer) for a in sample_states]
        + [any_spec] * len(stacks),
        out_specs=[rows_spec] + stack_specs[:3] + [_whole(xs.shape)] + stack_specs[3:],
        scratch_shapes=[
            pltpu.VMEM((POOL_BUF * n_prompt + tm, POOL_WIDTH), F32),
            pltpu.VMEM(((CONV_WIDTH - 1) * n_prompt + tm, LRU_WIDTH), F32),
            pltpu.VMEM((n_prompt, LRU_WIDTH), F32),
            pltpu.VMEM((tm, LRU_WIDTH), F32),
        ],
        input_output_aliases={n_in: 1, n_in + 1: 2, n_in + 2: 3, n_in + 3: 5, n_in + 4: 6, n_in + 5: 7},
        compiler_params=pltpu.CompilerParams(
            dimension_semantics=("arbitrary",),
            vmem_limit_bytes=MIXER_VMEM),
        name="mixer",
    )(x2d, mod, mod, mod, *wts, xs, mod, mod, mod, *sample_states, *stacks)


def _lane_blocks(w):
    depth, heads, c, _ = w.shape
    per = LANES // c
    eye = jnp.eye(per, dtype=w.dtype)
    blocks = jnp.einsum('lbgij,gh->lbgihj', w.reshape(depth, heads // per, per, c, c), eye)
    return blocks.reshape(depth, heads // per, LANES, LANES)


def kernel(x_prompt, x_sample, state_pool, state_conv, state_rglru, c_prompt, c_sample, w_ada, b_ada, ffn1_wgu, ffn1_wdown, w_in, w_out, w_pool, pool_scale, conv_w, conv_b, w_rg_a, b_rg_a, w_rg_x, b_rg_x, lru_lambda, ffn2_wgu, ffn2_wdown, ln_g, ln_b):
    n_prompt, seq, _ = x_prompt.shape
    n_sample = x_sample.shape[0]
    assert x_sample.shape[1] == 1 and n_sample % n_prompt == 0 and n_prompt == SUBLANES
    assert seq % MIXER_STEPS == 0 and (seq * n_prompt) % FFN_TILE == 0 and FFN_TILE % FFN_TILE_SMALL == 0

    wpool_bd = _lane_blocks(w_pool).astype(BF16)
    wgate_bd = jnp.concatenate([_lane_blocks(w_rg_a), _lane_blocks(w_rg_x)], axis=1).astype(BF16)

    c_all = jnp.concatenate([c_sample, c_prompt], axis=0)
    b_ada3 = b_ada.reshape(DEPTH, 1, b_ada.shape[-1])
    mod, wgu_bf, wd_bf = _ada_call(c_all, w_ada, b_ada3, 0, round_next=((ffn1_wgu, 0, True), (ffn1_wdown, 0, False)))

    xp = x_prompt
    xs = x_sample.reshape(n_sample, D_MODEL)
    hist_major = lambda a: jnp.swapaxes(a, 1, 2)
    sample_states = (hist_major(state_pool), hist_major(state_conv), state_rglru)
    new_p = (jnp.zeros((DEPTH, POOL_BUF, n_prompt, POOL_WIDTH), F32),
             jnp.zeros((DEPTH, CONV_WIDTH - 1, n_prompt, LRU_WIDTH), F32),
             jnp.zeros((DEPTH, n_prompt, LRU_WIDTH), F32))
    new_s = tuple(jnp.zeros(a.shape, F32) for a in sample_states)

    for l in range(DEPTH):
        last = l == DEPTH - 1
        ffn = functools.partial(_ffn_call, layer=l, n_prompt=n_prompt)
        xp, xs, win_bf, wout_bf, wgu2_bf, wd2_bf = ffn(
            xp, xs, mod, wgu_bf, wd_bf, ln_g, ln_b, sub=0, ln_idx=0, batch_major_in=(l == 0),
            tm=FFN_TILE,
            round_next=((w_in, l, False), (w_out, l, False), (ffn2_wgu, l, True), (ffn2_wdown, l, False)))

        mixer_w = (win_bf, wpool_bd, pool_scale, conv_w, conv_b, wgate_bd, b_rg_a, b_rg_x, lru_lambda,
                   wout_bf, ln_g, ln_b)
        outs = _mixer_call(xp, xs, mod, mixer_w, sample_states, new_p, new_s, layer=l, steps=MIXER_STEPS,
                           n_prompt=n_prompt)
        xp, new_p, xs, new_s = outs[0], outs[1:4], outs[4], outs[5:8]

        outs = ffn(xp, xs, mod, wgu2_bf, wd2_bf, ln_g, ln_b, sub=2, ln_idx=2, batch_major_out=last,
                   tm=FFN_TILE if last else FFN_TILE_SMALL,
                   round_next=() if last else ((ffn1_wgu, l + 1, True), (ffn1_wdown, l + 1, False)),
                   next_ada=None if last else (c_all, w_ada, b_ada3, l + 1))
        xp, xs = outs[:2]
        if not last:
            mod, wgu_bf, wd_bf = outs[2:]

    return (xp, xs.reshape(n_sample, 1, D_MODEL),
            hist_major(new_p[0]), hist_major(new_p[1]), new_p[2],
            hist_major(new_s[0]), hist_major(new_s[1]), new_s[2])
```

```python
import functools

import jax
import jax.numpy as jnp
from jax import lax
from jax.experimental import pallas as pl
from jax.experimental.pallas import tpu as pltpu

D_MODEL = 1024
DEPTH = 4
PAST_LEN = 16384
POOL_WIDTH = D_MODEL // 2
LRU_WIDTH = D_MODEL - POOL_WIDTH
IN_WIDTH = POOL_WIDTH + 2 * LRU_WIDTH
POOL_WINDOWS = (2, 4, 8, 16)
POOL_GROUP_DIM = POOL_WIDTH // len(POOL_WINDOWS)
POOL_BUF = max(POOL_WINDOWS) - 1
N_LRU_HEADS = 8
LRU_HEAD_DIM = LRU_WIDTH // N_LRU_HEADS
CONV_WIDTH = 4
LRU_C = 8.0
D_FF = 128 * int(round(8 * D_MODEL / 3 / 128))
N_MOD = 9
ALPHA = (2 * DEPTH) ** 0.25
LN_EPS = 1e-5

F32 = jnp.float32
BF16 = jnp.bfloat16

SUBLANES = 8
LANES = 128
BF16_SUBLANES = 16
MIB = 1024 * 1024
GU_BLOCKS = D_FF // LANES

FFN_TILE = 1024
FFN_TILE_SMALL = 512
FFN_SUBTILE = 256
MIXER_STEPS = 128
ADA_VMEM = 32 * MIB
FFN_VMEM = 56 * MIB
MIXER_VMEM = 56 * MIB


def _layer_norm(y, g, b):
    mu = jnp.mean(y, axis=-1, keepdims=True)
    d = y - mu
    var = jnp.mean(d * d, axis=-1, keepdims=True)
    return d * lax.rsqrt(var + LN_EPS) * g + b


def _sigmoid(x):
    return 0.5 * jnp.tanh(0.5 * x) + 0.5


def _silu(x):
    return x * _sigmoid(x)


def _softplus(x):
    return jnp.maximum(x, 0.0) + jnp.log1p(jnp.exp(-jnp.abs(x)))


def _per_batch(fn, x, *ms):
    rows, d = x.shape
    nb = ms[0].shape[0]
    if nb == rows:
        return fn(x, *ms)
    return fn(x.reshape(rows // nb, nb, d), *[m[None] for m in ms]).reshape(rows, d)


def _modulate(x, sh, sc):
    return _per_batch(lambda x_, sh_, sc_: x_ * (1.0 + sc_) + sh_, x, sh, sc)


def _gated_residual(x, g, y, gate_scale):
    return _per_batch(lambda y_, g_: (gate_scale * (1.0 + g_)) * y_, y, g) + ALPHA * x


def _mod_specs(sub, is_sample, n_sample, n_prompt):
    rows, block = (n_sample, 0) if is_sample else (n_prompt, n_sample // n_prompt)
    return [pl.BlockSpec((rows, D_MODEL), functools.partial(lambda i, k: (block, k), k=3 * sub + t))
            for t in range(3)]


def _resident(w, layer):
    if w.ndim == 2:
        return pl.BlockSpec(w.shape, lambda i: (0, 0), pipeline_mode=pl.Buffered(1))
    nd = w.ndim
    return pl.BlockSpec((None,) + tuple(w.shape[1:]), lambda i: (layer,) + (0,) * (nd - 1),
                        pipeline_mode=pl.Buffered(1))


def _layer_block(shape, layer):
    nd = len(shape)
    return pl.BlockSpec((None,) + tuple(shape[1:]), lambda i: (layer,) + (0,) * (nd - 1))


def _whole(shape):
    return pl.BlockSpec(tuple(shape), lambda i: (0,) * len(shape))


def _round_specs(w, layer, n_steps, step=lambda i: i):
    rows, cols = w.shape[1:]
    n_chunks = next(n for n in range(n_steps, 0, -1) if rows % (n * BF16_SUBLANES) == 0)
    chunk = rows // n_chunks
    idx = lambda *ids: jnp.minimum(step(*ids), n_chunks - 1)
    return (pl.BlockSpec((None, chunk, cols), lambda *ids: (layer, idx(*ids), 0)),
            pl.BlockSpec((chunk, cols), lambda *ids: (idx(*ids), 0)),
            jax.ShapeDtypeStruct((rows, cols), BF16))


def _round_chunks(src_refs, dst_refs, interleave):
    for src, dst, il in zip(src_refs, dst_refs, interleave):
        if il:
            for i in range(GU_BLOCKS):
                dst[:, 2 * i * LANES:(2 * i + 1) * LANES] = src[:, i * LANES:(i + 1) * LANES].astype(BF16)
                dst[:, (2 * i + 1) * LANES:(2 * i + 2) * LANES] = src[:, D_FF + i * LANES:D_FF + (i + 1) * LANES].astype(BF16)
        else:
            dst[...] = src[...].astype(BF16)


def _split(refs, *counts):
    out, pos = [], 0
    for c in counts:
        out.append(refs[pos:pos + c])
        pos += c
    out.append(refs[pos:])
    return out


def _ada_cols(n_out, n_steps):
    return next(tn for tn in range(LANES, n_out + 1, LANES) if n_out % tn == 0 and n_out // tn <= n_steps)


def _ada_tile(c_ref, w_ref, b_ref, o_ref):
    sc = _silu(c_ref[...]).astype(BF16)
    o_ref[...] = jnp.dot(sc, w_ref[...].astype(BF16), preferred_element_type=F32) + b_ref[...]


def _ada_specs(c_all, w_ada, layer, tn, n_steps):
    n_rows, n_out = c_all.shape[0], w_ada.shape[-1]
    n_tiles = n_out // tn
    assert n_tiles <= n_steps and n_out % tn == 0
    idx = lambda i: jnp.minimum(i, n_tiles - 1)
    return ([_whole(c_all.shape),
             pl.BlockSpec((None, D_MODEL, tn), lambda i: (layer, 0, idx(i))),
             pl.BlockSpec((None, 1, tn), lambda i: (layer, 0, idx(i)))],
            pl.BlockSpec((n_rows, tn), lambda i: (0, idx(i))),
            jax.ShapeDtypeStruct((n_rows, n_out), F32))


def _ada_kernel(*refs, interleave):
    n_round = len(interleave)
    ada_in, round_in, (o_ref,), round_out = _split(refs, 3, n_round, 1)
    _round_chunks(round_in, round_out, interleave)
    _ada_tile(*ada_in, o_ref)


def _ada_call(c_all, w_ada, b_ada, layer, round_next=()):
    n_col = 8
    ada_in, ada_out, ada_shape = _ada_specs(c_all, w_ada, layer, w_ada.shape[-1] // n_col, n_col)
    rnd = [_round_specs(w, l, n_col) for w, l, _ in round_next]
    return pl.pallas_call(
        functools.partial(_ada_kernel, interleave=tuple(il for _, _, il in round_next)),
        out_shape=[ada_shape] + [r[2] for r in rnd],
        grid=(n_col,),
        in_specs=ada_in + [r[0] for r in rnd],
        out_specs=[ada_out] + [r[1] for r in rnd],
        compiler_params=pltpu.CompilerParams(
            dimension_semantics=("arbitrary",),
            vmem_limit_bytes=ADA_VMEM),
        name="ada",
    )(c_all, w_ada, b_ada, *[w for w, _, _ in round_next])


def _swiglu_act(gu):
    blocks = [_silu(gu[:, 2 * i * LANES:(2 * i + 1) * LANES]) * gu[:, (2 * i + 1) * LANES:(2 * i + 2) * LANES]
              for i in range(GU_BLOCKS)]
    return jnp.concatenate(blocks, axis=-1)


def _ffn_rows(x, sh, sc, g, wgu_ref, wd_ref, lng, lnb):
    h = _modulate(x, sh, sc).astype(BF16)
    gu = jnp.dot(h, wgu_ref[...], preferred_element_type=F32)
    act = _swiglu_act(gu).astype(BF16)
    acc = jnp.dot(act, wd_ref[...], preferred_element_type=F32)
    return _layer_norm(_gated_residual(x, g, acc, 0.5), lng, lnb)


def _ffn_kernel(*refs, ln_idx, subtiles, batch_major_in, batch_major_out, interleave, next_mod):
    n_round, n_ada = len(interleave), 3 if next_mod else 0
    (x_ref, sh_ref, sc_ref, g_ref, wgu_ref, wd_ref, lng_ref, lnb_ref), sample_in, ada_in, round_in, \
        (o_ref, os_ref), ada_out, round_out, lbuf = _split(refs, 8, 4, n_ada, n_round, 2, n_ada // 3, n_round)
    _round_chunks(round_in, round_out, interleave)
    if next_mod:
        _ada_tile(*ada_in, *ada_out)
    lng, lnb = lng_ref[ln_idx:ln_idx + 1, :], lnb_ref[ln_idx:ln_idx + 1, :]
    sh, sc, g = sh_ref[...], sc_ref[...], g_ref[...]
    nb = sh.shape[0]
    rows = (x_ref.shape[0] * x_ref.shape[1] if batch_major_in else x_ref.shape[0]) // subtiles
    ts = rows // nb
    for s_i in range(subtiles):
        steps = slice(s_i * ts, (s_i + 1) * ts)
        if batch_major_in:
            for b in range(nb):
                lbuf[0][steps, b, :] = x_ref[b, steps, :]
            x = lbuf[0][steps, :, :].reshape(rows, D_MODEL)
        else:
            x = x_ref[s_i * rows:(s_i + 1) * rows, :]
        y = _ffn_rows(x, sh, sc, g, wgu_ref, wd_ref, lng, lnb)
        if batch_major_out:
            lbuf[0][steps, :, :] = y.reshape(ts, nb, D_MODEL)
            for b in range(nb):
                o_ref[b, steps, :] = lbuf[0][steps, b, :]
        else:
            o_ref[s_i * rows:(s_i + 1) * rows, :] = y

    @pl.when(pl.program_id(0) == pl.num_programs(0) - 1)
    def _sample_rows():
        xs_ref, shs_ref, scs_ref, gs_ref = sample_in
        xs = xs_ref[:, 0, :] if batch_major_in else xs_ref[...]
        ys = _ffn_rows(xs, shs_ref[...], scs_ref[...], gs_ref[...], wgu_ref, wd_ref, lng, lnb)
        if batch_major_out:
            os_ref[:, 0, :] = ys
        else:
            os_ref[...] = ys


def _ffn_call(x, xs, mod, wgu, wd, ln_g, ln_b, *, layer, sub, ln_idx, tm, n_prompt,
              batch_major_in=False, batch_major_out=False, round_next=(), next_ada=None):
    rows = x.shape[0] * x.shape[1] if batch_major_in else x.shape[0]
    n_sample = xs.shape[0]
    steps = tm // n_prompt
    n_grid = rows // tm
    major_spec = pl.BlockSpec((n_prompt, steps, D_MODEL), lambda i: (0, i, 0))
    rows_spec = pl.BlockSpec((tm, D_MODEL), lambda i: (i, 0))
    out_shape = (n_prompt, rows // n_prompt, D_MODEL) if batch_major_out else (rows, D_MODEL)
    xs_out_shape = (n_sample, 1, D_MODEL) if batch_major_out else (n_sample, D_MODEL)
    rnd = [_round_specs(w, l, n_grid) for w, l, _ in round_next]
    ada_in, ada_out, ada_shape, ada_args = [], [], [], []
    if next_ada is not None:
        c_all, w_ada, b_ada, ada_layer = next_ada
        ada_in, out_spec, shape = _ada_specs(c_all, w_ada, ada_layer, _ada_cols(w_ada.shape[-1], n_grid), n_grid)
        ada_out, ada_shape, ada_args = [out_spec], [shape], [c_all, w_ada, b_ada]
    kern = functools.partial(_ffn_kernel, ln_idx=ln_idx, subtiles=tm // FFN_SUBTILE,
                             batch_major_in=batch_major_in, batch_major_out=batch_major_out,
                             interleave=tuple(il for _, _, il in round_next), next_mod=next_ada is not None)
    return pl.pallas_call(
        kern,
        out_shape=[jax.ShapeDtypeStruct(out_shape, F32), jax.ShapeDtypeStruct(xs_out_shape, F32)] + ada_shape
        + [r[2] for r in rnd],
        grid=(n_grid,),
        in_specs=[major_spec if batch_major_in else rows_spec]
        + _mod_specs(sub, False, n_sample, n_prompt)
        + [_resident(wgu, layer), _resident(wd, layer), _resident(ln_g, layer), _resident(ln_b, layer)]
        + [_whole(xs.shape)] + _mod_specs(sub, True, n_sample, n_prompt)
        + ada_in + [r[0] for r in rnd],
        out_specs=[major_spec if batch_major_out else rows_spec, _whole(xs_out_shape)] + ada_out
        + [r[1] for r in rnd],
        scratch_shapes=[pltpu.VMEM((steps, n_prompt, D_MODEL), F32)] if batch_major_in or batch_major_out else [],
        compiler_params=pltpu.CompilerParams(
            dimension_semantics=("arbitrary",),
            vmem_limit_bytes=FFN_VMEM),
        name=f"ffn{sub}",
    )(x, mod, mod, mod, wgu, wd, ln_g, ln_b, xs, mod, mod, mod, *ada_args, *[w for w, _, _ in round_next])


def _grouped_dot(x, w_ref, first, n):
    c = w_ref.shape[-1]
    return jnp.concatenate([jnp.dot(x[:, i * c:(i + 1) * c], w_ref[first + i], preferred_element_type=F32)
                            for i in range(n)], axis=-1)


def _gates(xc, wg_ref, ba_ref, bx_ref, lam_ref):
    xb = xc.astype(BF16)
    n = wg_ref.shape[0] // 2
    r = _sigmoid(_grouped_dot(xb, wg_ref, 0, n) + ba_ref[...])
    i = _sigmoid(_grouped_dot(xb, wg_ref, n, n) + bx_ref[...])
    log_a = (-LRU_C * r) * _softplus(-lam_ref[...])
    a = jnp.exp(log_a)
    t = jnp.tanh(log_a)
    q = (-2.0 * t) / (1.0 - t)
    mult = jnp.where(q == 0.0, 0.0, q * lax.rsqrt(q))
    return a, mult * (i * xc)


def _mix_out(x, g, y_pool, y_lru, wout_ref, lng_ref, lnb_ref):
    mix = jnp.dot(y_pool.astype(BF16), wout_ref[:POOL_WIDTH, :], preferred_element_type=F32)
    mix = mix + jnp.dot(y_lru.astype(BF16), wout_ref[POOL_WIDTH:, :], preferred_element_type=F32)
    y = _gated_residual(x, g, mix, 1.0)
    return _layer_norm(y, lng_ref[1:2, :], lnb_ref[1:2, :])


N_MIXER_WEIGHTS = 12
MIXER_ROW_VECTORS = (2, 4, 6, 7, 8)


def _mixer_sample_rows(x_ref, sh_ref, sc_ref, g_ref, pst_ref, cst_ref, h0_ref, weights,
                       o_ref, npool_ref, nconv_ref, nh_ref):
    (win_ref, wpool_ref, pscale_ref, cw_ref, cb_ref, wg_ref, ba_ref, bx_ref, lam_ref,
     wout_ref, lng_ref, lnb_ref) = weights
    x = x_ref[...]
    h = _modulate(x, sh_ref[...], sc_ref[...]).astype(BF16)
    proj = jnp.dot(h, win_ref[...], preferred_element_type=F32)
    u_pool = proj[:, :POOL_WIDTH]
    u_lru = proj[:, POOL_WIDTH:POOL_WIDTH + LRU_WIDTH]
    u_gate = proj[:, POOL_WIDTH + LRU_WIDTH:]

    pooled = []
    for gi, w in enumerate(POOL_WINDOWS):
        cols = slice(gi * POOL_GROUP_DIM, (gi + 1) * POOL_GROUP_DIM)
        s = u_pool[:, cols]
        for back in range(1, w):
            s = s + pst_ref[POOL_BUF - back, :, cols]
        cnt = float(min(PAST_LEN + 1, w))
        pooled.append(s / cnt - u_pool[:, cols])
    pooled = jnp.concatenate(pooled, axis=-1).astype(BF16)
    y_pool = _grouped_dot(pooled, wpool_ref, 0, len(POOL_WINDOWS)) * pscale_ref[...]
    npool_ref[0:POOL_BUF - 1] = pst_ref[1:POOL_BUF]
    npool_ref[POOL_BUF - 1] = u_pool

    xc = None
    for k in range(CONV_WIDTH - 1):
        term = cst_ref[k] * cw_ref[k:k + 1, :]
        xc = term if xc is None else xc + term
    xc = cb_ref[...] + (xc + u_lru * cw_ref[CONV_WIDTH - 1:CONV_WIDTH, :])
    nconv_ref[0:CONV_WIDTH - 2] = cst_ref[1:CONV_WIDTH - 1]
    nconv_ref[CONV_WIDTH - 2] = u_lru

    a, b = _gates(xc, wg_ref, ba_ref, bx_ref, lam_ref)
    hs = a * h0_ref[...] + b
    nh_ref[...] = hs
    y_lru = hs * jax.nn.gelu(u_gate)

    o_ref[...] = _mix_out(x, g_ref[...], y_pool, y_lru, wout_ref, lng_ref, lnb_ref)


def _mixer_kernel(*refs, steps, nb, layer):
    (x_ref, sh_ref, sc_ref, g_ref), weights, sample_in, _aliased, \
        (o_ref, npool_ref, nconv_ref, nh_ref), sample_out, (zbuf, cbuf, hst, hsbuf) = \
        _split(refs, 4, N_MIXER_WEIGHTS, 7, 6, 4, 4)
    weights = tuple(w.at[pl.ds(layer, 1)] if i in MIXER_ROW_VECTORS else w for i, w in enumerate(weights))
    (win_ref, wpool_ref, pscale_ref, cw_ref, cb_ref, wg_ref, ba_ref, bx_ref, lam_ref,
     wout_ref, lng_ref, lnb_ref) = weights

    pid = pl.program_id(0)
    tm = steps * nb
    pool_hist = POOL_BUF * nb
    conv_hist = (CONV_WIDTH - 1) * nb

    @pl.when(pid == 0)
    def _new_sequences():
        zbuf[0:pool_hist, :] = jnp.zeros((pool_hist, POOL_WIDTH), F32)
        cbuf[0:conv_hist, :] = jnp.zeros((conv_hist, LRU_WIDTH), F32)
        hst[...] = jnp.zeros_like(hst)

    x = x_ref[...]
    proj = jnp.dot(_modulate(x, sh_ref[...], sc_ref[...]).astype(BF16), win_ref[...], preferred_element_type=F32)
    u_pool = proj[:, :POOL_WIDTH]
    u_lru = proj[:, POOL_WIDTH:POOL_WIDTH + LRU_WIDTH]
    u_gate = proj[:, POOL_WIDTH + LRU_WIDTH:]

    zbuf[pool_hist:pool_hist + tm, :] = u_pool
    t_idx = lax.shift_right_logical(lax.broadcasted_iota(jnp.int32, (tm, POOL_GROUP_DIM), 0), nb.bit_length() - 1)
    pos1 = (pid * steps + 1) + t_idx
    cur = zbuf[...]
    first, w = -POOL_BUF, 1
    pooled = []
    for gi, wnd in enumerate(POOL_WINDOWS):
        assert wnd == 2 * w
        n = cur.shape[0]
        cur = cur[w * nb:, :] + cur[:n - w * nb, :]
        first, w = first + w, wnd
        cols = slice(gi * POOL_GROUP_DIM, (gi + 1) * POOL_GROUP_DIM)
        s = cur[-first * nb:, :POOL_GROUP_DIM]
        pooled.append(s / jnp.minimum(pos1, wnd).astype(F32) - u_pool[:, cols])
        cur = cur[:, POOL_GROUP_DIM:]
    assert first == 0
    pooled = jnp.concatenate(pooled, axis=-1).astype(BF16)
    y_pool = _grouped_dot(pooled, wpool_ref, 0, len(POOL_WINDOWS)) * pscale_ref[...]

    cbuf[conv_hist:conv_hist + tm, :] = u_lru
    xc = None
    for k in range(CONV_WIDTH):
        term = cbuf[k * nb:k * nb + tm, :] * cw_ref[k:k + 1, :]
        xc = term if xc is None else xc + term
    a, b = _gates(cb_ref[...] + xc, wg_ref, ba_ref, bx_ref, lam_ref)

    hcur = hst[...]
    for t in range(steps):
        hcur = a[t * nb:(t + 1) * nb, :] * hcur + b[t * nb:(t + 1) * nb, :]
        hsbuf[t * nb:(t + 1) * nb, :] = hcur
    hst[...] = hcur

    y_lru = hsbuf[...] * jax.nn.gelu(u_gate)
    o_ref[...] = _mix_out(x, g_ref[...], y_pool, y_lru, wout_ref, lng_ref, lnb_ref)

    @pl.when(pid == pl.num_programs(0) - 1)
    def _final_state_and_sample_rows():
        npool_ref[...] = zbuf[tm:tm + pool_hist, :].reshape(POOL_BUF, nb, POOL_WIDTH)
        nconv_ref[...] = cbuf[tm:tm + conv_hist, :].reshape(CONV_WIDTH - 1, nb, LRU_WIDTH)
        nh_ref[...] = hcur
        _mixer_sample_rows(*sample_in, weights, *sample_out)

    zbuf[0:pool_hist, :] = zbuf[tm:tm + pool_hist, :]
    cbuf[0:conv_hist, :] = cbuf[tm:tm + conv_hist, :]


def _mixer_call(x2d, xs, mod, wts, sample_states, new_p, new_s, *, layer, steps, n_prompt):
    assert len(wts) == N_MIXER_WEIGHTS
    rows = x2d.shape[0]
    n_sample = xs.shape[0]
    tm = steps * n_prompt
    stacks = tuple(new_p) + tuple(new_s)
    n_in = 4 + len(wts) + 4 + len(sample_states)
    any_spec = pl.BlockSpec(memory_space=pl.ANY)
    stack_specs = [_layer_block(a.shape, layer) for a in stacks]
    rows_spec = pl.BlockSpec((tm, D_MODEL), lambda i: (i, 0))
    return pl.pallas_call(
        functools.partial(_mixer_kernel, steps=steps, nb=n_prompt, layer=layer),
        out_shape=[jax.ShapeDtypeStruct((rows, D_MODEL), F32)]
        + [jax.ShapeDtypeStruct(a.shape, F32) for a in new_p]
        + [jax.ShapeDtypeStruct(xs.shape, F32)]
        + [jax.ShapeDtypeStruct(a.shape, F32) for a in new_s],
        grid=(rows // tm,),
        in_specs=[rows_spec]
        + _mod_specs(1, False, n_sample, n_prompt)
        + [_resident(w, layer) for w in wts]
        + [_whole(xs.shape)] + _mod_specs(1, True, n_sample, n_prompt)
        + [_layer_block(a.shape, layer) for a in sample_states]
        + [any_spec] * len(stacks),
        out_specs=[rows_spec] + stack_specs[:3] + [_whole(xs.shape)] + stack_specs[3:],
        scratch_shapes=[
            pltpu.VMEM((POOL_BUF * n_prompt + tm, POOL_WIDTH), F32),
            pltpu.VMEM(((CONV_WIDTH - 1) * n_prompt + tm, LRU_WIDTH), F32),
            pltpu.VMEM((n_prompt, LRU_WIDTH), F32),
            pltpu.VMEM((tm, LRU_WIDTH), F32),
        ],
        input_output_aliases={n_in: 1, n_in + 1: 2, n_in + 2: 3, n_in + 3: 5, n_in + 4: 6, n_in + 5: 7},
        compiler_params=pltpu.CompilerParams(
            dimension_semantics=("arbitrary",),
            vmem_limit_bytes=MIXER_VMEM),
        name="mixer",
    )(x2d, mod, mod, mod, *wts, xs, mod, mod, mod, *sample_states, *stacks)


def _lane_blocks(w):
    depth, heads, c, _ = w.shape
    per = LANES // c
    eye = jnp.eye(per, dtype=w.dtype)
    blocks = jnp.einsum('lbgij,gh->lbgihj', w.reshape(depth, heads // per, per, c, c), eye)
    return blocks.reshape(depth, heads // per, LANES, LANES)


def kernel(x_prompt, x_sample, state_pool, state_conv, state_rglru, c_prompt, c_sample, w_ada, b_ada, ffn1_wgu, ffn1_wdown, w_in, w_out, w_pool, pool_scale, conv_w, conv_b, w_rg_a, b_rg_a, w_rg_x, b_rg_x, lru_lambda, ffn2_wgu, ffn2_wdown, ln_g, ln_b):
    n_prompt, seq, _ = x_prompt.shape
    n_sample = x_sample.shape[0]
    assert x_sample.shape[1] == 1 and n_sample % n_prompt == 0 and n_prompt == SUBLANES
    assert seq % MIXER_STEPS == 0 and (seq * n_prompt) % FFN_TILE == 0 and FFN_TILE % FFN_TILE_SMALL == 0

    wpool_bd = _lane_blocks(w_pool).astype(BF16)
    wgate_bd = jnp.concatenate([_lane_blocks(w_rg_a), _lane_blocks(w_rg_x)], axis=1).astype(BF16)

    c_all = jnp.concatenate([c_sample, c_prompt], axis=0)
    b_ada3 = b_ada.reshape(DEPTH, 1, b_ada.shape[-1])
    mod, wgu_bf, wd_bf = _ada_call(c_all, w_ada, b_ada3, 0, round_next=((ffn1_wgu, 0, True), (ffn1_wdown, 0, False)))

    xp = x_prompt
    xs = x_sample
    hist_major = lambda a: jnp.swapaxes(a, 1, 2)
    sample_states = (hist_major(state_pool), hist_major(state_conv), state_rglru)
    new_p = (jnp.zeros((DEPTH, POOL_BUF, n_prompt, POOL_WIDTH), F32),
             jnp.zeros((DEPTH, CONV_WIDTH - 1, n_prompt, LRU_WIDTH), F32),
             jnp.zeros((DEPTH, n_prompt, LRU_WIDTH), F32))
    new_s = tuple(jnp.zeros(a.shape, F32) for a in sample_states)

    for l in range(DEPTH):
        last = l == DEPTH - 1
        ffn = functools.partial(_ffn_call, layer=l, n_prompt=n_prompt)
        xp, xs, win_bf, wout_bf, wgu2_bf, wd2_bf = ffn(
            xp, xs, mod, wgu_bf, wd_bf, ln_g, ln_b, sub=0, ln_idx=0, batch_major_in=(l == 0),
            tm=FFN_TILE,
            round_next=((w_in, l, False), (w_out, l, False), (ffn2_wgu, l, True), (ffn2_wdown, l, False)))

        mixer_w = (win_bf, wpool_bd, pool_scale, conv_w, conv_b, wgate_bd, b_rg_a, b_rg_x, lru_lambda,
                   wout_bf, ln_g, ln_b)
        outs = _mixer_call(xp, xs, mod, mixer_w, sample_states, new_p, new_s, layer=l, steps=MIXER_STEPS,
                           n_prompt=n_prompt)
        xp, new_p, xs, new_s = outs[0], outs[1:4], outs[4], outs[5:8]

        outs = ffn(xp, xs, mod, wgu2_bf, wd2_bf, ln_g, ln_b, sub=2, ln_idx=2, batch_major_out=last,
                   tm=FFN_TILE if last else FFN_TILE_SMALL,
                   round_next=() if last else ((ffn1_wgu, l + 1, True), (ffn1_wdown, l + 1, False)),
                   next_ada=None if last else (c_all, w_ada, b_ada3, l + 1))
        xp, xs = outs[:2]
        if not last:
            mod, wgu_bf, wd_bf = outs[2:]

    return (xp, xs,
            hist_major(new_p[0]), hist_major(new_p[1]), new_p[2],
            hist_major(new_s[0]), hist_major(new_s[1]), new_s[2])
```

```python
import functools

import jax
import jax.numpy as jnp
from jax import lax
from jax.experimental import pallas as pl
from jax.experimental.pallas import tpu as pltpu

D_MODEL = 1024
DEPTH = 4
PAST_LEN = 16384
POOL_WIDTH = D_MODEL // 2
LRU_WIDTH = D_MODEL - POOL_WIDTH
IN_WIDTH = POOL_WIDTH + 2 * LRU_WIDTH
POOL_WINDOWS = (2, 4, 8, 16)
POOL_GROUP_DIM = POOL_WIDTH // len(POOL_WINDOWS)
POOL_BUF = max(POOL_WINDOWS) - 1
N_LRU_HEADS = 8
LRU_HEAD_DIM = LRU_WIDTH // N_LRU_HEADS
CONV_WIDTH = 4
LRU_C = 8.0
D_FF = 128 * int(round(8 * D_MODEL / 3 / 128))
N_MOD = 9
ALPHA = (2 * DEPTH) ** 0.25
LN_EPS = 1e-5

F32 = jnp.float32
BF16 = jnp.bfloat16

SUBLANES = 8
LANES = 128
BF16_SUBLANES = 16
MIB = 1024 * 1024
GU_BLOCKS = D_FF // LANES

FFN_TILE = 1024
FFN_TILE_SMALL = 512
FFN_SUBTILE = 256
MIXER_STEPS = 128
ADA_VMEM = 32 * MIB
FFN_VMEM = 56 * MIB
MIXER_VMEM = 56 * MIB


def _layer_norm(y, g, b):
    mu = jnp.mean(y, axis=-1, keepdims=True)
    d = y - mu
    var = jnp.mean(d * d, axis=-1, keepdims=True)
    return d * lax.rsqrt(var + LN_EPS) * g + b


def _sigmoid(x):
    return 0.5 * jnp.tanh(0.5 * x) + 0.5


def _silu(x):
    return x * _sigmoid(x)


def _softplus(x):
    return jnp.maximum(x, 0.0) + jnp.log1p(jnp.exp(-jnp.abs(x)))


def _per_batch(fn, x, *ms):
    rows, d = x.shape
    nb = ms[0].shape[0]
    if nb == rows:
        return fn(x, *ms)
    return fn(x.reshape(rows // nb, nb, d), *[m[None] for m in ms]).reshape(rows, d)


def _modulate(x, sh, sc):
    return _per_batch(lambda x_, sh_, sc_: x_ * (1.0 + sc_) + sh_, x, sh, sc)


def _gated_residual(x, g, y, gate_scale):
    return _per_batch(lambda y_, g_: (gate_scale * (1.0 + g_)) * y_, y, g) + ALPHA * x


def _mod_specs(sub, is_sample, n_sample, n_prompt):
    rows, block = (n_sample, 0) if is_sample else (n_prompt, n_sample // n_prompt)
    return [pl.BlockSpec((rows, D_MODEL), functools.partial(lambda i, k: (block, k), k=3 * sub + t))
            for t in range(3)]


def _resident(w, layer):
    if w.ndim == 2:
        return pl.BlockSpec(w.shape, lambda i: (0, 0), pipeline_mode=pl.Buffered(1))
    nd = w.ndim
    return pl.BlockSpec((None,) + tuple(w.shape[1:]), lambda i: (layer,) + (0,) * (nd - 1),
                        pipeline_mode=pl.Buffered(1))


def _layer_block(shape, layer):
    nd = len(shape)
    return pl.BlockSpec((None,) + tuple(shape[1:]), lambda i: (layer,) + (0,) * (nd - 1))


def _whole(shape):
    return pl.BlockSpec(tuple(shape), lambda i: (0,) * len(shape))


def _round_specs(w, layer, n_steps, step=lambda i: i):
    rows, cols = w.shape[1:]
    n_chunks = next(n for n in range(n_steps, 0, -1) if rows % (n * BF16_SUBLANES) == 0)
    chunk = rows // n_chunks
    idx = lambda *ids: jnp.minimum(step(*ids), n_chunks - 1)
    return (pl.BlockSpec((None, chunk, cols), lambda *ids: (layer, idx(*ids), 0)),
            pl.BlockSpec((chunk, cols), lambda *ids: (idx(*ids), 0)),
            jax.ShapeDtypeStruct((rows, cols), BF16))


def _round_chunks(src_refs, dst_refs, interleave):
    for src, dst, il in zip(src_refs, dst_refs, interleave):
        if il:
            for i in range(GU_BLOCKS):
                dst[:, 2 * i * LANES:(2 * i + 1) * LANES] = src[:, i * LANES:(i + 1) * LANES].astype(BF16)
                dst[:, (2 * i + 1) * LANES:(2 * i + 2) * LANES] = src[:, D_FF + i * LANES:D_FF + (i + 1) * LANES].astype(BF16)
        else:
            dst[...] = src[...].astype(BF16)


def _split(refs, *counts):
    out, pos = [], 0
    for c in counts:
        out.append(refs[pos:pos + c])
        pos += c
    out.append(refs[pos:])
    return out


def _ada_cols(n_out, n_steps):
    return next(tn for tn in range(LANES, n_out + 1, LANES) if n_out % tn == 0 and n_out // tn <= n_steps)


def _ada_tile(c_ref, w_ref, b_ref, o_ref):
    sc = _silu(c_ref[...]).astype(BF16)
    o_ref[...] = jnp.dot(sc, w_ref[...].astype(BF16), preferred_element_type=F32) + b_ref[...]


def _ada_specs(c_all, w_ada, layer, tn, n_steps):
    n_rows, n_out = c_all.shape[0], w_ada.shape[-1]
    n_tiles = n_out // tn
    assert n_tiles <= n_steps and n_out % tn == 0
    idx = lambda i: jnp.minimum(i, n_tiles - 1)
    return ([_whole(c_all.shape),
             pl.BlockSpec((None, D_MODEL, tn), lambda i: (layer, 0, idx(i))),
             pl.BlockSpec((None, 1, tn), lambda i: (layer, 0, idx(i)))],
            pl.BlockSpec((n_rows, tn), lambda i: (0, idx(i))),
            jax.ShapeDtypeStruct((n_rows, n_out), F32))


def _ada_kernel(*refs, interleave):
    n_round = len(interleave)
    ada_in, round_in, (o_ref,), round_out = _split(refs, 3, n_round, 1)
    _round_chunks(round_in, round_out, interleave)
    _ada_tile(*ada_in, o_ref)


def _ada_call(c_all, w_ada, b_ada, layer, round_next=()):
    n_col = 8
    ada_in, ada_out, ada_shape = _ada_specs(c_all, w_ada, layer, w_ada.shape[-1] // n_col, n_col)
    rnd = [_round_specs(w, l, n_col) for w, l, _ in round_next]
    return pl.pallas_call(
        functools.partial(_ada_kernel, interleave=tuple(il for _, _, il in round_next)),
        out_shape=[ada_shape] + [r[2] for r in rnd],
        grid=(n_col,),
        in_specs=ada_in + [r[0] for r in rnd],
        out_specs=[ada_out] + [r[1] for r in rnd],
        compiler_params=pltpu.CompilerParams(
            dimension_semantics=("arbitrary",),
            vmem_limit_bytes=ADA_VMEM),
        name="ada",
    )(c_all, w_ada, b_ada, *[w for w, _, _ in round_next])


def _swiglu_act(gu):
    blocks = [_silu(gu[:, 2 * i * LANES:(2 * i + 1) * LANES]) * gu[:, (2 * i + 1) * LANES:(2 * i + 2) * LANES]
              for i in range(GU_BLOCKS)]
    return jnp.concatenate(blocks, axis=-1)


def _ffn_rows(x, sh, sc, g, wgu_ref, wd_ref, lng, lnb):
    h = _modulate(x, sh, sc).astype(BF16)
    gu = jnp.dot(h, wgu_ref[...], preferred_element_type=F32)
    act = _swiglu_act(gu).astype(BF16)
    acc = jnp.dot(act, wd_ref[...], preferred_element_type=F32)
    return _layer_norm(_gated_residual(x, g, acc, 0.5), lng, lnb)


def _ffn_kernel(*refs, ln_idx, subtiles, batch_major_in, batch_major_out, interleave, next_mod):
    n_round, n_ada = len(interleave), 3 if next_mod else 0
    (x_ref, sh_ref, sc_ref, g_ref, wgu_ref, wd_ref, lng_ref, lnb_ref), sample_in, ada_in, round_in, \
        (o_ref, os_ref), ada_out, round_out, lbuf = _split(refs, 8, 4, n_ada, n_round, 2, n_ada // 3, n_round)
    _round_chunks(round_in, round_out, interleave)
    if next_mod:
        _ada_tile(*ada_in, *ada_out)
    lng, lnb = lng_ref[ln_idx:ln_idx + 1, :], lnb_ref[ln_idx:ln_idx + 1, :]
    sh, sc, g = sh_ref[...], sc_ref[...], g_ref[...]
    nb = sh.shape[0]
    rows = (x_ref.shape[0] * x_ref.shape[1] if batch_major_in else x_ref.shape[0]) // subtiles
    ts = rows // nb
    for s_i in range(subtiles):
        steps = slice(s_i * ts, (s_i + 1) * ts)
        if batch_major_in:
            for b in range(nb):
                lbuf[0][steps, b, :] = x_ref[b, steps, :]
            x = lbuf[0][steps, :, :].reshape(rows, D_MODEL)
        else:
            x = x_ref[s_i * rows:(s_i + 1) * rows, :]
        y = _ffn_rows(x, sh, sc, g, wgu_ref, wd_ref, lng, lnb)
        if batch_major_out:
            lbuf[0][steps, :, :] = y.reshape(ts, nb, D_MODEL)
            for b in range(nb):
                o_ref[b, steps, :] = lbuf[0][steps, b, :]
        else:
            o_ref[s_i * rows:(s_i + 1) * rows, :] = y

    @pl.when(pl.program_id(0) == pl.num_programs(0) - 1)
    def _sample_rows():
        xs_ref, shs_ref, scs_ref, gs_ref = sample_in
        xs = xs_ref[:, 0, :] if batch_major_in else xs_ref[...]
        ys = _ffn_rows(xs, shs_ref[...], scs_ref[...], gs_ref[...], wgu_ref, wd_ref, lng, lnb)
        if batch_major_out:
            os_ref[:, 0, :] = ys
        else:
            os_ref[...] = ys


def _ffn_call(x, xs, mod, wgu, wd, ln_g, ln_b, *, layer, sub, ln_idx, tm, n_prompt,
              batch_major_in=False, batch_major_out=False, round_next=(), next_ada=None):
    rows = x.shape[0] * x.shape[1] if batch_major_in else x.shape[0]
    n_sample = xs.shape[0]
    steps = tm // n_prompt
    n_grid = rows // tm
    major_spec = pl.BlockSpec((n_prompt, steps, D_MODEL), lambda i: (0, i, 0))
    rows_spec = pl.BlockSpec((tm, D_MODEL), lambda i: (i, 0))
    out_shape = (n_prompt, rows // n_prompt, D_MODEL) if batch_major_out else (rows, D_MODEL)
    xs_out_shape = (n_sample, 1, D_MODEL) if batch_major_out else (n_sample, D_MODEL)
    rnd = [_round_specs(w, l, n_grid) for w, l, _ in round_next]
    ada_in, ada_out, ada_shape, ada_args = [], [], [], []
    if next_ada is not None:
        c_all, w_ada, b_ada, ada_layer = next_ada
        ada_in, out_spec, shape = _ada_specs(c_all, w_ada, ada_layer, _ada_cols(w_ada.shape[-1], n_grid), n_grid)
        ada_out, ada_shape, ada_args = [out_spec], [shape], [c_all, w_ada, b_ada]
    kern = functools.partial(_ffn_kernel, ln_idx=ln_idx, subtiles=tm // FFN_SUBTILE,
                             batch_major_in=batch_major_in, batch_major_out=batch_major_out,
                             interleave=tuple(il for _, _, il in round_next), next_mod=next_ada is not None)
    return pl.pallas_call(
        kern,
        out_shape=[jax.ShapeDtypeStruct(out_shape, F32), jax.ShapeDtypeStruct(xs_out_shape, F32)] + ada_shape
        + [r[2] for r in rnd],
        grid=(n_grid,),
        in_specs=[major_spec if batch_major_in else rows_spec]
        + _mod_specs(sub, False, n_sample, n_prompt)
        + [_resident(wgu, layer), _resident(wd, layer), _resident(ln_g, layer), _resident(ln_b, layer)]
        + [_whole(xs.shape)] + _mod_specs(sub, True, n_sample, n_prompt)
        + ada_in + [r[0] for r in rnd],
        out_specs=[major_spec if batch_major_out else rows_spec, _whole(xs_out_shape)] + ada_out
        + [r[1] for r in rnd],
        scratch_shapes=[pltpu.VMEM((steps, n_prompt, D_MODEL), F32)] if batch_major_in or batch_major_out else [],
        compiler_params=pltpu.CompilerParams(
            dimension_semantics=("arbitrary",),
            vmem_limit_bytes=FFN_VMEM),
        name=f"ffn{sub}",
    )(x, mod, mod, mod, wgu, wd, ln_g, ln_b, xs, mod, mod, mod, *ada_args, *[w for w, _, _ in round_next])


def _grouped_dot(x, w_ref, first, n):
    c = w_ref.shape[-1]
    return jnp.concatenate([jnp.dot(x[:, i * c:(i + 1) * c], w_ref[first + i], preferred_element_type=F32)
                            for i in range(n)], axis=-1)


def _gates(xc, wg_ref, ba, bx, lam, first=0):
    xb = xc.astype(BF16)
    n = wg_ref.shape[0] // 2
    cnt = xc.shape[-1] // wg_ref.shape[-1]
    r = _sigmoid(_grouped_dot(xb, wg_ref, first, cnt) + ba)
    i = _sigmoid(_grouped_dot(xb, wg_ref, n + first, cnt) + bx)
    log_a = (-LRU_C * r) * _softplus(-lam)
    a = jnp.exp(log_a)
    t = jnp.tanh(log_a)
    q = (-2.0 * t) / (1.0 - t)
    mult = jnp.where(q == 0.0, 0.0, q * lax.rsqrt(q))
    return a, mult * (i * xc)


def _mix_out(x, g, y_pool, y_lru, wout_ref, lng_ref, lnb_ref):
    mix = jnp.dot(y_pool.astype(BF16), wout_ref[:POOL_WIDTH, :], preferred_element_type=F32)
    mix = mix + jnp.dot(y_lru.astype(BF16), wout_ref[POOL_WIDTH:, :], preferred_element_type=F32)
    y = _gated_residual(x, g, mix, 1.0)
    return _layer_norm(y, lng_ref[1:2, :], lnb_ref[1:2, :])


N_MIXER_WEIGHTS = 12
MIXER_ROW_VECTORS = (2, 4, 6, 7, 8)


def _mixer_sample_rows(x_ref, sh_ref, sc_ref, g_ref, pst_ref, cst_ref, h0_ref, weights,
                       o_ref, npool_ref, nconv_ref, nh_ref):
    (win_ref, wpool_ref, pscale_ref, cw_ref, cb_ref, wg_ref, ba_ref, bx_ref, lam_ref,
     wout_ref, lng_ref, lnb_ref) = weights
    x = x_ref[...]
    h = _modulate(x, sh_ref[...], sc_ref[...]).astype(BF16)
    proj = jnp.dot(h, win_ref[...], preferred_element_type=F32)
    u_pool = proj[:, :POOL_WIDTH]
    u_lru = proj[:, POOL_WIDTH:POOL_WIDTH + LRU_WIDTH]
    u_gate = proj[:, POOL_WIDTH + LRU_WIDTH:]

    pooled = []
    for gi, w in enumerate(POOL_WINDOWS):
        cols = slice(gi * POOL_GROUP_DIM, (gi + 1) * POOL_GROUP_DIM)
        s = u_pool[:, cols]
        for back in range(1, w):
            s = s + pst_ref[POOL_BUF - back, :, cols]
        cnt = float(min(PAST_LEN + 1, w))
        pooled.append(s / cnt - u_pool[:, cols])
    pooled = jnp.concatenate(pooled, axis=-1).astype(BF16)
    y_pool = _grouped_dot(pooled, wpool_ref, 0, len(POOL_WINDOWS)) * pscale_ref[...]
    npool_ref[0:POOL_BUF - 1] = pst_ref[1:POOL_BUF]
    npool_ref[POOL_BUF - 1] = u_pool

    xc = None
    for k in range(CONV_WIDTH - 1):
        term = cst_ref[k] * cw_ref[k:k + 1, :]
        xc = term if xc is None else xc + term
    xc = cb_ref[...] + (xc + u_lru * cw_ref[CONV_WIDTH - 1:CONV_WIDTH, :])
    nconv_ref[0:CONV_WIDTH - 2] = cst_ref[1:CONV_WIDTH - 1]
    nconv_ref[CONV_WIDTH - 2] = u_lru

    a, b = _gates(xc, wg_ref, ba_ref[...], bx_ref[...], lam_ref[...])
    hs = a * h0_ref[...] + b
    nh_ref[...] = hs
    y_lru = hs * jax.nn.gelu(u_gate)

    o_ref[...] = _mix_out(x, g_ref[...], y_pool, y_lru, wout_ref, lng_ref, lnb_ref)


def _mixer_kernel(*refs, steps, nb, layer):
    (x_ref, sh_ref, sc_ref, g_ref), weights, sample_in, _aliased, \
        (o_ref, npool_ref, nconv_ref, nh_ref), sample_out, (zbuf, cbuf, hst, hsbuf) = \
        _split(refs, 4, N_MIXER_WEIGHTS, 7, 6, 4, 4)
    weights = tuple(w.at[pl.ds(layer, 1)] if i in MIXER_ROW_VECTORS else w for i, w in enumerate(weights))
    (win_ref, wpool_ref, pscale_ref, cw_ref, cb_ref, wg_ref, ba_ref, bx_ref, lam_ref,
     wout_ref, lng_ref, lnb_ref) = weights

    pid = pl.program_id(0)
    tm = steps * nb
    pool_hist = POOL_BUF * nb
    conv_hist = (CONV_WIDTH - 1) * nb

    @pl.when(pid == 0)
    def _new_sequences():
        zbuf[0:pool_hist, :] = jnp.zeros((pool_hist, POOL_WIDTH), F32)
        cbuf[0:conv_hist, :] = jnp.zeros((conv_hist, LRU_WIDTH), F32)
        hst[...] = jnp.zeros_like(hst)

    x = x_ref[...]
    proj = jnp.dot(_modulate(x, sh_ref[...], sc_ref[...]).astype(BF16), win_ref[...], preferred_element_type=F32)
    u_pool = proj[:, :POOL_WIDTH]
    u_lru = proj[:, POOL_WIDTH:POOL_WIDTH + LRU_WIDTH]
    u_gate = proj[:, POOL_WIDTH + LRU_WIDTH:]

    zbuf[pool_hist:pool_hist + tm, :] = u_pool
    t_idx = lax.shift_right_logical(lax.broadcasted_iota(jnp.int32, (tm, POOL_GROUP_DIM), 0), nb.bit_length() - 1)
    pos1 = (pid * steps + 1) + t_idx
    cur = zbuf[...]
    first, w = -POOL_BUF, 1
    pooled = []
    for gi, wnd in enumerate(POOL_WINDOWS):
        assert wnd == 2 * w
        n = cur.shape[0]
        cur = cur[w * nb:, :] + cur[:n - w * nb, :]
        first, w = first + w, wnd
        cols = slice(gi * POOL_GROUP_DIM, (gi + 1) * POOL_GROUP_DIM)
        s = cur[-first * nb:, :POOL_GROUP_DIM]
        pooled.append(s / jnp.minimum(pos1, wnd).astype(F32) - u_pool[:, cols])
        cur = cur[:, POOL_GROUP_DIM:]
    assert first == 0
    pooled = jnp.concatenate(pooled, axis=-1).astype(BF16)
    y_pool = _grouped_dot(pooled, wpool_ref, 0, len(POOL_WINDOWS)) * pscale_ref[...]

    cbuf[conv_hist:conv_hist + tm, :] = u_lru
    for j in range(LRU_WIDTH // LANES):
        cols = slice(j * LANES, (j + 1) * LANES)
        xc = None
        for k in range(CONV_WIDTH):
            term = cbuf[k * nb:k * nb + tm, cols] * cw_ref[k:k + 1, cols]
            xc = term if xc is None else xc + term
        a, b = _gates(cb_ref[:, cols] + xc, wg_ref, ba_ref[:, cols], bx_ref[:, cols], lam_ref[:, cols], first=j)

        hcur = hst[:, cols]
        for t in range(steps):
            hcur = a[t * nb:(t + 1) * nb, :] * hcur + b[t * nb:(t + 1) * nb, :]
            hsbuf[t * nb:(t + 1) * nb, cols] = hcur
        hst[:, cols] = hcur

    y_lru = hsbuf[...] * jax.nn.gelu(u_gate)
    o_ref[...] = _mix_out(x, g_ref[...], y_pool, y_lru, wout_ref, lng_ref, lnb_ref)

    @pl.when(pid == pl.num_programs(0) - 1)
    def _final_state_and_sample_rows():
        npool_ref[...] = zbuf[tm:tm + pool_hist, :].reshape(POOL_BUF, nb, POOL_WIDTH)
        nconv_ref[...] = cbuf[tm:tm + conv_hist, :].reshape(CONV_WIDTH - 1, nb, LRU_WIDTH)
        nh_ref[...] = hst[...]
        _mixer_sample_rows(*sample_in, weights, *sample_out)

    zbuf[0:pool_hist, :] = zbuf[tm:tm + pool_hist, :]
    cbuf[0:conv_hist, :] = cbuf[tm:tm + conv_hist, :]


def _mixer_call(x2d, xs, mod, wts, sample_states, new_p, new_s, *, layer, steps, n_prompt):
    assert len(wts) == N_MIXER_WEIGHTS
    rows = x2d.shape[0]
    n_sample = xs.shape[0]
    tm = steps * n_prompt
    stacks = tuple(new_p) + tuple(new_s)
    n_in = 4 + len(wts) + 4 + len(sample_states)
    any_spec = pl.BlockSpec(memory_space=pl.ANY)
    stack_specs = [_layer_block(a.shape, layer) for a in stacks]
    rows_spec = pl.BlockSpec((tm, D_MODEL), lambda i: (i, 0))
    return pl.pallas_call(
        functools.partial(_mixer_kernel, steps=steps, nb=n_prompt, layer=layer),
        out_shape=[jax.ShapeDtypeStruct((rows, D_MODEL), F32)]
        + [jax.ShapeDtypeStruct(a.shape, F32) for a in new_p]
        + [jax.ShapeDtypeStruct(xs.shape, F32)]
        + [jax.ShapeDtypeStruct(a.shape, F32) for a in new_s],
        grid=(rows // tm,),
        in_specs=[rows_spec]
        + _mod_specs(1, False, n_sample, n_prompt)
        + [_resident(w, layer) for w in wts]
        + [_whole(xs.shape)] + _mod_specs(1, True, n_sample, n_prompt)
        + [_layer_block(a.shape, layer) for a in sample_states]
        + [any_spec] * len(stacks),
        out_specs=[rows_spec] + stack_specs[:3] + [_whole(xs.shape)] + stack_specs[3:],
        scratch_shapes=[
            pltpu.VMEM((POOL_BUF * n_prompt + tm, POOL_WIDTH), F32),
            pltpu.VMEM(((CONV_WIDTH - 1) * n_prompt + tm, LRU_WIDTH), F32),
            pltpu.VMEM((n_prompt, LRU_WIDTH), F32),
            pltpu.VMEM((tm, LRU_WIDTH), F32),
        ],
        input_output_aliases={n_in: 1, n_in + 1: 2, n_in + 2: 3, n_in + 3: 5, n_in + 4: 6, n_in + 5: 7},
        compiler_params=pltpu.CompilerParams(
            dimension_semantics=("arbitrary",),
            vmem_limit_bytes=MIXER_VMEM),
        name="mixer",
    )(x2d, mod, mod, mod, *wts, xs, mod, mod, mod, *sample_states, *stacks)


def _lane_blocks(w):
    depth, heads, c, _ = w.shape
    per = LANES // c
    eye = jnp.eye(per, dtype=w.dtype)
    blocks = jnp.einsum('lbgij,gh->lbgihj', w.reshape(depth, heads // per, per, c, c), eye)
    return blocks.reshape(depth, heads // per, LANES, LANES)


def kernel(x_prompt, x_sample, state_pool, state_conv, state_rglru, c_prompt, c_sample, w_ada, b_ada, ffn1_wgu, ffn1_wdown, w_in, w_out, w_pool, pool_scale, conv_w, conv_b, w_rg_a, b_rg_a, w_rg_x, b_rg_x, lru_lambda, ffn2_wgu, ffn2_wdown, ln_g, ln_b):
    n_prompt, seq, _ = x_prompt.shape
    n_sample = x_sample.shape[0]
    assert x_sample.shape[1] == 1 and n_sample % n_prompt == 0 and n_prompt == SUBLANES
    assert seq % MIXER_STEPS == 0 and (seq * n_prompt) % FFN_TILE == 0 and FFN_TILE % FFN_TILE_SMALL == 0

    wpool_bd = _lane_blocks(w_pool).astype(BF16)
    wgate_bd = jnp.concatenate([_lane_blocks(w_rg_a), _lane_blocks(w_rg_x)], axis=1).astype(BF16)

    c_all = jnp.concatenate([c_sample, c_prompt], axis=0)
    b_ada3 = b_ada.reshape(DEPTH, 1, b_ada.shape[-1])
    mod, wgu_bf, wd_bf = _ada_call(c_all, w_ada, b_ada3, 0, round_next=((ffn1_wgu, 0, True), (ffn1_wdown, 0, False)))

    xp = x_prompt
    xs = x_sample
    hist_major = lambda a: jnp.swapaxes(a, 1, 2)
    sample_states = (hist_major(state_pool), hist_major(state_conv), state_rglru)
    new_p = (jnp.zeros((DEPTH, POOL_BUF, n_prompt, POOL_WIDTH), F32),
             jnp.zeros((DEPTH, CONV_WIDTH - 1, n_prompt, LRU_WIDTH), F32),
             jnp.zeros((DEPTH, n_prompt, LRU_WIDTH), F32))
    new_s = tuple(jnp.zeros(a.shape, F32) for a in sample_states)

    for l in range(DEPTH):
        last = l == DEPTH - 1
        ffn = functools.partial(_ffn_call, layer=l, n_prompt=n_prompt)
        xp, xs, win_bf, wout_bf, wgu2_bf, wd2_bf = ffn(
            xp, xs, mod, wgu_bf, wd_bf, ln_g, ln_b, sub=0, ln_idx=0, batch_major_in=(l == 0),
            tm=FFN_TILE,
            round_next=((w_in, l, False), (w_out, l, False), (ffn2_wgu, l, True), (ffn2_wdown, l, False)))

        mixer_w = (win_bf, wpool_bd, pool_scale, conv_w, conv_b, wgate_bd, b_rg_a, b_rg_x, lru_lambda,
                   wout_bf, ln_g, ln_b)
        outs = _mixer_call(xp, xs, mod, mixer_w, sample_states, new_p, new_s, layer=l, steps=MIXER_STEPS,
                           n_prompt=n_prompt)
        xp, new_p, xs, new_s = outs[0], outs[1:4], outs[4], outs[5:8]

        outs = ffn(xp, xs, mod, wgu2_bf, wd2_bf, ln_g, ln_b, sub=2, ln_idx=2, batch_major_out=last,
                   tm=FFN_TILE if last else FFN_TILE_SMALL,
                   round_next=() if last else ((ffn1_wgu, l + 1, True), (ffn1_wdown, l + 1, False)),
                   next_ada=None if last else (c_all, w_ada, b_ada3, l + 1))
        xp, xs = outs[:2]
        if not last:
            mod, wgu_bf, wd_bf = outs[2:]

    return (xp, xs,
            hist_major(new_p[0]), hist_major(new_p[1]), new_p[2],
            hist_major(new_s[0]), hist_major(new_s[1]), new_s[2])
```
